```python
import math
import jax, jax.numpy as jnp
from jax import lax
import numpy as np

D_MODEL = 2048
BATCH = 16
SEQ = 2048
DEPTH = 1
DEC_BATCH = 16
DEC_SEQ = 16
PAST_LEN = 2048

CHUNK = 64
MIX_WIDTH = D_MODEL
SSD_WIDTH = MIX_WIDTH // 2
SSD_HEADDIM = 64
SSD_HEADS = SSD_WIDTH // SSD_HEADDIM
SSD_GROUPS = 4
SSD_STATE = 128
CONV_W = 4
CONV_DIM = SSD_WIDTH + 2 * SSD_GROUPS * SSD_STATE
S5_WIDTH = MIX_WIDTH - SSD_WIDTH
S5_GROUP_CH = 16
S5_GROUPS = S5_WIDTH // S5_GROUP_CH
S5_STATE = 64
D_FF = 5504
IN_PROJ_DIM = SSD_WIDTH + CONV_DIM + SSD_HEADS + S5_WIDTH
EPS = 1e-6

kernel_name = 'hybrid_ssd_s5_streaming_encoder'


def _rmsnorm(x, g):
    xf = x.astype(jnp.float32)
    xf = xf * lax.rsqrt(jnp.mean(xf * xf, axis=-1, keepdims=True) + EPS)
    return (xf * g.astype(jnp.float32)).astype(x.dtype)


def _swiglu(x, w_gate, w_up, w_down):
    return (jax.nn.silu(x @ w_gate) * (x @ w_up)) @ w_down


def _causal_conv(xbc, buf, w, b):
    seq = xbc.shape[1]
    padded = jnp.concatenate([buf.astype(xbc.dtype), xbc], axis=1)
    out = b + padded[:, 0:seq] * w[0]
    for k in range(1, CONV_W):
        out = out + padded[:, k:k + seq] * w[k]
    return jax.nn.silu(out), padded[:, seq:]


def _ssd(x, dt, a, bmat, cmat, h0):
    f32 = jnp.float32
    bsz, seq = x.shape[0], x.shape[1]
    q = min(CHUNK, seq)
    nc = seq // q
    r = SSD_HEADS // SSD_GROUPS
    xc = x.astype(f32).reshape(bsz, nc, q, SSD_GROUPS, r, SSD_HEADDIM)
    dtc = dt.astype(f32).reshape(bsz, nc, q, SSD_GROUPS, r)
    bc = bmat.astype(f32).reshape(bsz, nc, q, SSD_GROUPS, SSD_STATE)
    cc = cmat.astype(f32).reshape(bsz, nc, q, SSD_GROUPS, SSD_STATE)
    acs = jnp.cumsum(dtc * a.astype(f32).reshape(SSD_GROUPS, r), axis=2)
    seg = acs[:, :, :, None] - acs[:, :, None, :]
    causal = jnp.tril(jnp.ones((q, q), dtype=bool))[:, :, None, None]
    lmat = jnp.exp(jnp.where(causal, seg, -jnp.inf))
    xdt = xc * dtc[..., None]
    cb = jnp.einsum('bcign,bcjgn->bcijg', cc, bc)
    y_diag = jnp.einsum('bcijg,bcijgr,bcjgrp->bcigrp', cb, lmat, xdt)
    decay_end = jnp.exp(acs[:, :, -1:] - acs)
    chunk_states = jnp.einsum('bcjgn,bcjgr,bcjgrp->bcgrpn', bc, decay_end, xdt)
    chunk_decay = jnp.exp(acs[:, :, -1])

    def step(h, inp):
        dec, st = inp
        return dec[..., None, None] * h + st, h

    h_init = h0.astype(f32).reshape(bsz, SSD_GROUPS, r, SSD_HEADDIM, SSD_STATE)
    h_last, h_prev = lax.scan(step, h_init,
                              (jnp.moveaxis(chunk_decay, 1, 0), jnp.moveaxis(chunk_states, 1, 0)))
    h_prev = jnp.moveaxis(h_prev, 0, 1)
    y_off = jnp.einsum('bcign,bcgrpn,bcigr->bcigrp', cc, h_prev, jnp.exp(acs))
    y = (y_diag + y_off).reshape(bsz, seq, SSD_HEADS, SSD_HEADDIM)
    return y.astype(x.dtype), h_last.reshape(h0.shape).astype(h0.dtype)


def _s5(u, h0_re, h0_im, lam_re, lam_im, log_step, b_re, b_im, c_re, c_im, d):
    f32 = jnp.float32
    bsz, seq = u.shape[0], u.shape[1]
    ug = u.astype(f32).reshape(bsz, seq, S5_GROUPS, S5_GROUP_CH)
    lam_re = lam_re.astype(f32)
    lam_im = lam_im.astype(f32)
    step = jnp.exp(log_step.astype(f32))[:, None]
    mag = jnp.exp(lam_re * step)
    ang = lam_im * step
    lb_re = mag * jnp.cos(ang)
    lb_im = mag * jnp.sin(ang)
    den = lam_re * lam_re + lam_im * lam_im
    nr = lb_re - 1.0
    q_re = (nr * lam_re + lb_im * lam_im) / den
    q_im = (lb_im * lam_re - nr * lam_im) / den
    b_re = b_re.astype(f32)
    b_im = b_im.astype(f32)
    bb_re = q_re[..., None] * b_re - q_im[..., None] * b_im
    bb_im = q_re[..., None] * b_im + q_im[..., None] * b_re
    bu_re = jnp.einsum('blgh,gnh->blgn', ug, bb_re)
    bu_im = jnp.einsum('blgh,gnh->blgn', ug, bb_im)
    h0_re = h0_re.astype(f32)
    h0_im = h0_im.astype(f32)
    bu_re = bu_re.at[:, 0].add(lb_re * h0_re - lb_im * h0_im)
    bu_im = bu_im.at[:, 0].add(lb_re * h0_im + lb_im * h0_re)
    a_re = jnp.broadcast_to(lb_re[None, None], (1, seq, S5_GROUPS, S5_STATE))
    a_im = jnp.broadcast_to(lb_im[None, None], (1, seq, S5_GROUPS, S5_STATE))

    def combine(e1, e2):
        a1r, a1i, b1r, b1i = e1
        a2r, a2i, b2r, b2i = e2
        return (a1r * a2r - a1i * a2i, a1r * a2i + a1i * a2r,
                a2r * b1r - a2i * b1i + b2r, a2r * b1i + a2i * b1r + b2i)

    _, _, hr, hi = lax.associative_scan(combine, (a_re, a_im, bu_re, bu_im), axis=1)
    y = (jnp.einsum('blgn,ghn->blgh', hr, c_re.astype(f32))
         - jnp.einsum('blgn,ghn->blgh', hi, c_im.astype(f32))
         + d.astype(f32) * ug)
    return (y.reshape(bsz, seq, S5_WIDTH).astype(u.dtype),
            hr[:, -1].astype(h0_re.dtype), hi[:, -1].astype(h0_im.dtype))


def _mixer(u, conv_buf, h_ssd, h5_re, h5_im, p):
    bsz, seq = u.shape[0], u.shape[1]
    proj = u @ p['w_in']
    z, xbc, dt_raw, u5 = jnp.split(
        proj, [SSD_WIDTH, SSD_WIDTH + CONV_DIM, SSD_WIDTH + CONV_DIM + SSD_HEADS], axis=-1)
    xbc, new_conv = _causal_conv(xbc, conv_buf, p['conv_w'], p['conv_b'])
    xs, bm, cm = jnp.split(xbc, [SSD_WIDTH, SSD_WIDTH + SSD_GROUPS * SSD_STATE], axis=-1)
    xs = xs.reshape(bsz, seq, SSD_HEADS, SSD_HEADDIM)
    dt = jax.nn.softplus(dt_raw + p['dt_bias'])
    a = -jnp.exp(p['a_log'])
    y_ssd, new_h = _ssd(xs, dt, a,
                        bm.reshape(bsz, seq, SSD_GROUPS, SSD_STATE),
                        cm.reshape(bsz, seq, SSD_GROUPS, SSD_STATE), h_ssd)
    y_ssd = (y_ssd + p['d_ssd'][:, None] * xs).reshape(bsz, seq, SSD_WIDTH)
    y_ssd = _rmsnorm(y_ssd * jax.nn.silu(z), p['norm_ssd'])
    y5, new_re, new_im = _s5(u5, h5_re, h5_im, p['s5_lambda_re'], p['s5_lambda_im'],
                             p['s5_log_step'], p['s5_b_re'], p['s5_b_im'],
                             p['s5_c_re'], p['s5_c_im'], p['s5_d'])
    g = jax.nn.gelu(y5, approximate=False)
    y5 = _rmsnorm(g * jax.nn.sigmoid(g @ p['w_glu'] + p['b_glu']), p['norm_s5'])
    out = jnp.concatenate([y_ssd, y5], axis=-1) @ p['w_out']
    return out, new_conv, new_h, new_re, new_im


def _layer(x, conv_buf, h_ssd, h5_re, h5_im, p):
    x = x + 0.5 * _swiglu(_rmsnorm(x, p['norm_ffn1']), p['w_ffn1_gate'], p['w_ffn1_up'], p['w_ffn1_down'])
    m, new_conv, new_h, new_re, new_im = _mixer(_rmsnorm(x, p['norm_mix']), conv_buf, h_ssd, h5_re, h5_im, p)
    x = x + m
    x = x + 0.5 * _swiglu(_rmsnorm(x, p['norm_ffn2']), p['w_ffn2_gate'], p['w_ffn2_up'], p['w_ffn2_down'])
    return x, new_conv, new_h, new_re, new_im


def _trunk(x, conv_buf, h_ssd, h5_re, h5_im, params, norm_final):
    convs, hs, res, ims = [], [], [], []
    for l in range(DEPTH):
        p = {k: v[l] for k, v in params.items()}
        x, c, h, r, i = _layer(x, conv_buf[l], h_ssd[l], h5_re[l], h5_im[l], p)
        convs.append(c)
        hs.append(h)
        res.append(r)
        ims.append(i)
    return (_rmsnorm(x, norm_final), jnp.stack(convs), jnp.stack(hs),
            jnp.stack(res), jnp.stack(ims))


def setup_inputs(seed: int = 0) -> dict:
    key = jax.random.key(seed)
    ks = jax.random.split(key, 40)
    f32 = jnp.float32

    def nrm(k, shape, scale):
        return scale * jax.random.normal(k, shape, f32)

    def gain(k, shape):
        return 1.0 + 0.01 * jax.random.normal(k, shape, f32)

    L = DEPTH
    dt0 = jnp.exp(jax.random.uniform(ks[10], (L, SSD_HEADS), f32, math.log(1e-3), math.log(1e-1)))
    lam_im = math.pi * jnp.arange(S5_STATE, dtype=f32)
    return {
        'x_prompt': nrm(ks[0], (BATCH, SEQ, D_MODEL), 1.0),
        'x_sample': nrm(ks[1], (DEC_BATCH, DEC_SEQ, D_MODEL), 1.0),
        'cache_conv': nrm(ks[2], (L, DEC_BATCH, CONV_W - 1, CONV_DIM), 1.0),
        'state_ssd': nrm(ks[3], (L, DEC_BATCH, SSD_HEADS, SSD_HEADDIM, SSD_STATE), 0.1),
        'state_s5_re': nrm(ks[4], (L, DEC_BATCH, S5_GROUPS, S5_STATE), 0.5),
        'state_s5_im': nrm(ks[5], (L, DEC_BATCH, S5_GROUPS, S5_STATE), 0.5),
        'norm_ffn1': gain(ks[6], (L, D_MODEL)),
        'w_ffn1_gate': nrm(ks[7], (L, D_MODEL, D_FF), D_MODEL ** -0.5),
        'w_ffn1_up': nrm(ks[8], (L, D_MODEL, D_FF), D_MODEL ** -0.5),
        'w_ffn1_down': nrm(ks[9], (L, D_FF, D_MODEL), D_FF ** -0.5),
        'norm_mix': gain(ks[11], (L, D_MODEL)),
        'w_in': nrm(ks[12], (L, D_MODEL, IN_PROJ_DIM), D_MODEL ** -0.5),
        'conv_w': nrm(ks[13], (L, CONV_W, CONV_DIM), CONV_W ** -0.5),
        'conv_b': nrm(ks[14], (L, CONV_DIM), 0.01),
        'dt_bias': dt0 + jnp.log(-jnp.expm1(-dt0)),
        'a_log': jnp.log(jax.random.uniform(ks[15], (L, SSD_HEADS), f32, 1.0, 16.0)),
        'd_ssd': gain(ks[16], (L, SSD_HEADS)),
        'norm_ssd': gain(ks[17], (L, SSD_WIDTH)),
        's5_lambda_re': -0.5 + nrm(ks[18], (L, S5_GROUPS, S5_STATE), 0.01),
        's5_lambda_im': lam_im + nrm(ks[19], (L, S5_GROUPS, S5_STATE), 0.01),
        's5_log_step': jax.random.uniform(ks[20], (L, S5_GROUPS), f32, math.log(1e-3), math.log(1e-1)),
        's5_b_re': nrm(ks[21], (L, S5_GROUPS, S5_STATE, S5_GROUP_CH), (2 * S5_GROUP_CH) ** -0.5),
        's5_b_im': nrm(ks[22], (L, S5_GROUPS, S5_STATE, S5_GROUP_CH), (2 * S5_GROUP_CH) ** -0.5),
        's5_c_re': nrm(ks[23], (L, S5_GROUPS, S5_GROUP_CH, S5_STATE), S5_STATE ** -0.5),
        's5_c_im': nrm(ks[24], (L, S5_GROUPS, S5_GROUP_CH, S5_STATE), S5_STATE ** -0.5),
        's5_d': nrm(ks[25], (L, S5_GROUPS, S5_GROUP_CH), 1.0),
        'w_glu': nrm(ks[26], (L, S5_WIDTH, S5_WIDTH), S5_WIDTH ** -0.5),
        'b_glu': nrm(ks[27], (L, S5_WIDTH), 0.01),
        'norm_s5': gain(ks[28], (L, S5_WIDTH)),
        'w_out': nrm(ks[29], (L, MIX_WIDTH, D_MODEL), MIX_WIDTH ** -0.5),
        'norm_ffn2': gain(ks[30], (L, D_MODEL)),
        'w_ffn2_gate': nrm(ks[31], (L, D_MODEL, D_FF), D_MODEL ** -0.5),
        'w_ffn2_up': nrm(ks[32], (L, D_MODEL, D_FF), D_MODEL ** -0.5),
        'w_ffn2_down': nrm(ks[33], (L, D_FF, D_MODEL), D_FF ** -0.5),
        'norm_final': gain(ks[34], (D_MODEL,)),
    }


def reference(x_prompt, x_sample, cache_conv, state_ssd, state_s5_re, state_s5_im,
              norm_ffn1, w_ffn1_gate, w_ffn1_up, w_ffn1_down, norm_mix, w_in,
              conv_w, conv_b, dt_bias, a_log, d_ssd, norm_ssd,
              s5_lambda_re, s5_lambda_im, s5_log_step, s5_b_re, s5_b_im,
              s5_c_re, s5_c_im, s5_d, w_glu, b_glu, norm_s5, w_out,
              norm_ffn2, w_ffn2_gate, w_ffn2_up, w_ffn2_down, norm_final):
    params = {
        'norm_ffn1': norm_ffn1, 'w_ffn1_gate': w_ffn1_gate, 'w_ffn1_up': w_ffn1_up,
        'w_ffn1_down': w_ffn1_down, 'norm_mix': norm_mix, 'w_in': w_in,
        'conv_w': conv_w, 'conv_b': conv_b, 'dt_bias': dt_bias, 'a_log': a_log,
        'd_ssd': d_ssd, 'norm_ssd': norm_ssd, 's5_lambda_re': s5_lambda_re,
        's5_lambda_im': s5_lambda_im, 's5_log_step': s5_log_step, 's5_b_re': s5_b_re,
        's5_b_im': s5_b_im, 's5_c_re': s5_c_re, 's5_c_im': s5_c_im, 's5_d': s5_d,
        'w_glu': w_glu, 'b_glu': b_glu, 'norm_s5': norm_s5, 'w_out': w_out,
        'norm_ffn2': norm_ffn2, 'w_ffn2_gate': w_ffn2_gate, 'w_ffn2_up': w_ffn2_up,
        'w_ffn2_down': w_ffn2_down,
    }
    bsz = x_prompt.shape[0]
    zero_conv = jnp.zeros((DEPTH, bsz, CONV_W - 1, CONV_DIM), x_prompt.dtype)
    zero_ssd = jnp.zeros((DEPTH, bsz, SSD_HEADS, SSD_HEADDIM, SSD_STATE), state_ssd.dtype)
    zero_s5_re = jnp.zeros((DEPTH, bsz, S5_GROUPS, S5_STATE), state_s5_re.dtype)
    zero_s5_im = jnp.zeros((DEPTH, bsz, S5_GROUPS, S5_STATE), state_s5_im.dtype)
    y_prompt, conv_p, ssd_p, s5re_p, s5im_p = _trunk(
        x_prompt, zero_conv, zero_ssd, zero_s5_re, zero_s5_im, params, norm_final)
    y_sample, conv_s, ssd_s, s5re_s, s5im_s = _trunk(
        x_sample, cache_conv, state_ssd, state_s5_re, state_s5_im, params, norm_final)
    return (y_prompt, y_sample, conv_p, ssd_p, s5re_p, s5im_p, conv_s, ssd_s, s5re_s, s5im_s)
```

```python
import functools

import jax
import jax.numpy as jnp
from jax import lax
from jax.experimental import pallas as pl
from jax.experimental.pallas import tpu as pltpu

F32 = jnp.float32
BF16 = jnp.bfloat16
EPS = 1e-6

D_MODEL = 2048
D_FF = 5504
SSD_WIDTH = 1024
SSD_HEADDIM = 64
SSD_HEADS = 16
SSD_GROUPS = 4
SSD_STATE = 128
CONV_W = 4
CONV_DIM = SSD_WIDTH + 2 * SSD_GROUPS * SSD_STATE
S5_WIDTH = 1024
S5_GROUP_CH = 16
S5_GROUPS = 64
S5_STATE = 64
CHUNK = 64

LANES = 128
D_FF_PAD = 5632
FFN_TM = 512
FFN_TF = 512
DT_PAD = LANES
PROJ_COLS = CONV_DIM + SSD_WIDTH + S5_WIDTH + DT_PAD
PROJ_TN = PROJ_COLS // 3
PROJ_TM = 512
OUT_TM = 512
S5_NB = 8
S5_LC = 32
S5_SLABS = S5_GROUPS * S5_STATE // LANES
S5_QB = S5_WIDTH // LANES
S5_SLAB_GROUP = 4
VMEM_LIMIT = 56 * 1024 * 1024


def _rms(x, g):
    ms = jnp.mean(x * x, axis=-1, keepdims=True)
    return x * lax.rsqrt(ms + EPS) * g


def _silu(x):
    return x * jax.nn.sigmoid(x)


def _params(*sem):
    return pltpu.CompilerParams(dimension_semantics=sem, vmem_limit_bytes=VMEM_LIMIT)


def _ffn_kernel(x_ref, g_ref, wg_ref, wu_ref, wd_ref, gf_ref, o_ref, h_ref, *, final_norm):
    j = pl.program_id(1)

    @pl.when(j == 0)
    def _():
        x = x_ref[...]
        h_ref[...] = _rms(x, g_ref[...]).astype(BF16)
        o_ref[...] = x

    h = h_ref[...]
    gate = jnp.dot(h, wg_ref[...], preferred_element_type=F32)
    up = jnp.dot(h, wu_ref[...], preferred_element_type=F32)
    a = (0.5 * (_silu(gate) * up)).astype(BF16)
    o_ref[...] += jnp.dot(a, wd_ref[...], preferred_element_type=F32)

    if final_norm:
        @pl.when(j == pl.num_programs(1) - 1)
        def _():
            o_ref[...] = _rms(o_ref[...], gf_ref[...])


def _ffn(x, norm_g, wg, wu, wd, final_g=None):
    t = x.shape[0]
    tm = min(FFN_TM, t)
    kern = functools.partial(_ffn_kernel, final_norm=final_g is not None)
    gf = norm_g if final_g is None else final_g
    return pl.pallas_call(
        kern,
        grid=(t // tm, D_FF_PAD // FFN_TF),
        in_specs=[
            pl.BlockSpec((tm, D_MODEL), lambda i, j: (i, 0)),
            pl.BlockSpec((1, D_MODEL), lambda i, j: (0, 0)),
            pl.BlockSpec((D_MODEL, FFN_TF), lambda i, j: (0, j)),
            pl.BlockSpec((D_MODEL, FFN_TF), lambda i, j: (0, j)),
            pl.BlockSpec((FFN_TF, D_MODEL), lambda i, j: (j, 0)),
            pl.BlockSpec((1, D_MODEL), lambda i, j: (0, 0)),
        ],
        out_specs=pl.BlockSpec((tm, D_MODEL), lambda i, j: (i, 0)),
        out_shape=jax.ShapeDtypeStruct((t, D_MODEL), F32),
        scratch_shapes=[pltpu.VMEM((tm, D_MODEL), BF16)],
        compiler_params=_params("parallel", "arbitrary"),
        name="ffn",
    )(x, norm_g, wg, wu, wd, gf)


def _inproj_kernel(x_ref, g_ref, w_ref, o_ref, h_ref):
    @pl.when(pl.program_id(1) == 0)
    def _():
        h_ref[...] = _rms(x_ref[...], g_ref[...]).astype(BF16)

    o_ref[...] = jnp.dot(h_ref[...], w_ref[...], preferred_element_type=F32)


def _in_proj(x, norm_g, w):
    t = x.shape[0]
    tm = min(PROJ_TM, t)
    return pl.pallas_call(
        _inproj_kernel,
        grid=(t // tm, PROJ_COLS // PROJ_TN),
        in_specs=[
            pl.BlockSpec((tm, D_MODEL), lambda i, j: (i, 0)),
            pl.BlockSpec((1, D_MODEL), lambda i, j: (0, 0)),
            pl.BlockSpec((D_MODEL, PROJ_TN), lambda i, j: (0, j)),
        ],
        out_specs=pl.BlockSpec((tm, PROJ_TN), lambda i, j: (i, j)),
        out_shape=jax.ShapeDtypeStruct((t, PROJ_COLS), F32),
        scratch_shapes=[pltpu.VMEM((tm, D_MODEL), BF16)],
        compiler_params=_params("parallel", "arbitrary"),
        name="in_proj",
    )(x, norm_g, w)


def _expand_heads(v):
    q = v.shape[0]
    lane = lax.broadcasted_iota(jnp.int32, (q, LANES), 1)
    blocks = []
    for k in range(SSD_WIDTH // LANES):
        a = jnp.broadcast_to(v[:, 2 * k:2 * k + 1], (q, LANES))
        b = jnp.broadcast_to(v[:, 2 * k + 1:2 * k + 2], (q, LANES))
        blocks.append(jnp.where(lane < SSD_HEADDIM, a, b))
    return jnp.concatenate(blocks, axis=1)


def _ssd_kernel(xbc_ref, z_ref, dtr_ref, cw_ref, cb_ref, dtb_ref, alog_ref, d_ref, nrm_ref,
                conv0_ref, h0_ref, y_ref, convo_ref, ht_ref, pad_ref, *, valid):
    q = CHUNK
    c = pl.program_id(1)
    gw = SSD_WIDTH // SSD_GROUPS

    @pl.when(c == 0)
    def _():
        pad_ref[5:8, :] = conv0_ref[0]
        ht_ref[0] = h0_ref[0]

    raw = xbc_ref[0]
    pad_ref[8:8 + q, :] = raw
    cw = cw_ref[...]
    conv = (cb_ref[...] + pad_ref[5:5 + q, :] * cw[0:1, :] + pad_ref[6:6 + q, :] * cw[1:2, :]
            + pad_ref[7:7 + q, :] * cw[2:3, :] + raw * cw[3:4, :])
    pad_ref[5:8, :] = pad_ref[8 + valid - 3:8 + valid, :]
    act = _silu(conv)
    xs = act[:, :SSD_WIDTH]
    bm = act[:, SSD_WIDTH:SSD_WIDTH + SSD_GROUPS * SSD_STATE].astype(BF16)
    cm = act[:, SSD_WIDTH + SSD_GROUPS * SSD_STATE:].astype(BF16)

    dtc = jax.nn.softplus(dtr_ref[0] + dtb_ref[...])
    if valid < q:
        dtc = jnp.where(lax.broadcasted_iota(jnp.int32, dtc.shape, 0) < valid, dtc, 0.0)
    da = dtc * (-jnp.exp(alog_ref[...]))
    tril = (lax.broadcasted_iota(jnp.int32, (q, q), 1)
            <= lax.broadcasted_iota(jnp.int32, (q, q), 0)).astype(F32)
    acs_c = jnp.dot(tril, da, precision=lax.Precision.HIGHEST, preferred_element_type=F32)
    dt_e = _expand_heads(dtc)
    acs_e = _expand_heads(acs_c)
    row = lax.broadcasted_iota(jnp.int32, (q, SSD_WIDTH), 0)
    jj = lax.broadcasted_iota(jnp.int32, (q, SSD_WIDTH), 1) & (SSD_HEADDIM - 1)
    acs_row = jnp.sum(jnp.where(jj == row, acs_e, 0.0), axis=0, keepdims=True)
    lmat = jnp.exp(jnp.where(jj <= row, acs_e - acs_row, -jnp.inf))
    last = acs_e[q - 1:q, :]
    eacs = jnp.exp(acs_e)
    dec_end = jnp.exp(last - acs_e)
    cdec = jnp.exp(last)
    xdt = xs * dt_e
    xdt_b = xdt.astype(BF16)
    xd_b = (xdt * dec_end).astype(BF16)
    head_of_lane = lax.broadcasted_iota(jnp.int32, (q, gw), 1) >> 6

    ys = []
    for g in range(SSD_GROUPS):
        sl = slice(g * gw, (g + 1) * gw)
        bg = bm[:, g * SSD_STATE:(g + 1) * SSD_STATE]
        cg = cm[:, g * SSD_STATE:(g + 1) * SSD_STATE]
        b4 = jnp.concatenate([bg] * (SSD_HEADS // SSD_GROUPS), axis=0)
        cbt = lax.dot_general(cg, b4, (((1,), (1,)), ((), ())), preferred_element_type=F32)
        sg = (cbt * lmat[:, sl]).astype(BF16)
        xg = xdt_b[:, sl]
        xbd = jnp.concatenate(
            [jnp.where(head_of_lane == hh, xg, jnp.zeros_like(xg)) for hh in range(SSD_HEADS // SSD_GROUPS)],
            axis=0)
        ydiag = jnp.dot(sg, xbd, preferred_element_type=F32)
        htg = ht_ref[0, :, sl]
        yoff = jnp.dot(cg, htg.astype(BF16), preferred_element_type=F32)
        ys.append(ydiag + eacs[:, sl] * yoff)
        st = lax.dot_general(bg, xd_b[:, sl], (((0,), (0,)), ((), ())), preferred_element_type=F32)
        ht_ref[0, :, sl] = cdec[:, sl] * htg + st

    y = jnp.concatenate(ys, axis=1) + d_ref[...] * xs
    y = y * _silu(z_ref[0])
    y_ref[0] = _rms(y, nrm_ref[...]).astype(BF16)

    @pl.when(c == pl.num_programs(1) - 1)
    def _():
        convo_ref[0] = pad_ref[5:8, :]


def _ssd(proj3, valid, cw, cb, dtb, alog, d_e, nrm, conv0, ht0):
    bsz, lp, _ = proj3.shape
    q = CHUNK
    kern = functools.partial(_ssd_kernel, valid=valid)
    row = lambda n: pl.BlockSpec((1, n), lambda b, c: (0, 0))
    return pl.pallas_call(
        kern,
        grid=(bsz, lp // q),
        in_specs=[
            pl.BlockSpec((1, q, CONV_DIM), lambda b, c: (b, c, 0)),
            pl.BlockSpec((1, q, SSD_WIDTH), lambda b, c: (b, c, CONV_DIM // SSD_WIDTH)),
            pl.BlockSpec((1, q, DT_PAD), lambda b, c: (b, c, (PROJ_COLS - DT_PAD) // DT_PAD)),
            pl.BlockSpec((CONV_W, CONV_DIM), lambda b, c: (0, 0)),
            row(CONV_DIM), row(DT_PAD), row(DT_PAD), row(SSD_WIDTH), row(SSD_WIDTH),
            pl.BlockSpec((1, CONV_W - 1, CONV_DIM), lambda b, c: (b, 0, 0)),
            pl.BlockSpec((1, SSD_STATE, SSD_WIDTH), lambda b, c: (b, 0, 0)),
        ],
        out_specs=[
            pl.BlockSpec((1, q, SSD_WIDTH), lambda b, c: (b, c, 0)),
            pl.BlockSpec((1, CONV_W - 1, CONV_DIM), lambda b, c: (b, 0, 0)),
            pl.BlockSpec((1, SSD_STATE, SSD_WIDTH), lambda b, c: (b, 0, 0)),
        ],
        out_shape=[
            jax.ShapeDtypeStruct((bsz, lp, SSD_WIDTH), BF16),
            jax.ShapeDtypeStruct((bsz, CONV_W - 1, CONV_DIM), F32),
            jax.ShapeDtypeStruct((bsz, SSD_STATE, SSD_WIDTH), F32),
        ],
        scratch_shapes=[pltpu.VMEM((8 + q, CONV_DIM), F32)],
        compiler_params=_params("parallel", "arbitrary"),
        name="ssd",
    )(proj3, proj3, proj3, cw, cb, dtb, alog, d_e, nrm, conv0, ht0)


def _s5_prep_kernel(lre_ref, lim_ref, lstep_ref, bre_ref, bim_ref, lbr_ref, lbi_ref, bbr_ref, bbi_ref):
    lam_re = lre_ref[...]
    lam_im = lim_ref[...]
    step = jnp.exp(lstep_ref[...])
    mag = jnp.exp(lam_re * step)
    ang = lam_im * step
    lb_re = mag * jnp.cos(ang)
    lb_im = mag * jnp.sin(ang)
    den = lam_re * lam_re + lam_im * lam_im
    nr = lb_re - 1.0
    q_re = (nr * lam_re + lb_im * lam_im) / den
    q_im = (lb_im * lam_re - nr * lam_im) / den
    lbr_ref[...] = lb_re
    lbi_ref[...] = lb_im
    b_re = bre_ref[...]
    b_im = bim_ref[...]
    bbr_ref[...] = q_re[None] * b_re - q_im[None] * b_im
    bbi_ref[...] = q_re[None] * b_im + q_im[None] * b_re


def _s5_prep(lam_re, lam_im, log_step_e, b_re_t, b_im_t):
    gn = jax.ShapeDtypeStruct((S5_GROUPS, S5_STATE), F32)
    hgn = jax.ShapeDtypeStruct((S5_GROUP_CH, S5_GROUPS, S5_STATE), F32)
    return pl.pallas_call(_s5_prep_kernel, out_shape=[gn, gn, hgn, hgn], name="s5_prep")(
        lam_re, lam_im, log_step_e, b_re_t, b_im_t)


def _s5_kernel(u_ref, bdr_ref, bdi_ref, cdr_ref, cdi_ref, lbr_ref, lbi_ref, d_ref, wglu_ref, bglu_ref,
               nrm_ref, h0r_ref, h0i_ref, y_ref, hor_ref, hoi_ref, up_ref, hr_ref, hi_ref, yp_ref,
               *, lc, pitch):
    c = pl.program_id(1)

    @pl.when(c == 0)
    def _():
        hor_ref[...] = h0r_ref[...]
        hoi_ref[...] = h0i_ref[...]
        up_ref[...] = jnp.zeros_like(up_ref)

    for b in range(S5_NB):
        up_ref[b * pitch:b * pitch + lc, :] = u_ref[b]
    u = up_ref[...]
    ub = u.astype(BF16)

    for qb in range(S5_QB):
        uq = ub[:, qb * LANES:(qb + 1) * LANES]
        bu_r = jnp.dot(uq, bdr_ref[qb], preferred_element_type=F32)
        bu_i = jnp.dot(uq, bdi_ref[qb], preferred_element_type=F32)
        for k in range(S5_SLABS // S5_QB):
            hr_ref[qb * 4 + k] = bu_r[:, k * LANES:(k + 1) * LANES]
            hi_ref[qb * 4 + k] = bu_i[:, k * LANES:(k + 1) * LANES]

    for sg in range(S5_SLABS // S5_SLAB_GROUP):
        slabs = [sg * S5_SLAB_GROUP + k for k in range(S5_SLAB_GROUP)]
        lanes = [slice(s * LANES, (s + 1) * LANES) for s in slabs]
        lr = [jnp.broadcast_to(lbr_ref[:, ln], (S5_NB, LANES)) for ln in lanes]
        li = [jnp.broadcast_to(lbi_ref[:, ln], (S5_NB, LANES)) for ln in lanes]
        init = tuple(hor_ref[:, ln] for ln in lanes) + tuple(hoi_ref[:, ln] for ln in lanes)

        def body(t, carry, slabs=slabs, lr=lr, li=li):
            new_r, new_i = [], []
            for k, s in enumerate(slabs):
                h_r, h_i = carry[k], carry[S5_SLAB_GROUP + k]
                rows = pl.ds(t, S5_NB, stride=pitch)
                n_r = lr[k] * h_r - li[k] * h_i + hr_ref[s, rows, :]
                n_i = lr[k] * h_i + li[k] * h_r + hi_ref[s, rows, :]
                hr_ref[s, rows, :] = n_r
                hi_ref[s, rows, :] = n_i
                new_r.append(n_r)
                new_i.append(n_i)
            return tuple(new_r) + tuple(new_i)

        fin = lax.fori_loop(0, lc, body, init)
        for k, ln in enumerate(lanes):
            hor_ref[:, ln] = fin[k]
            hoi_ref[:, ln] = fin[S5_SLAB_GROUP + k]

    for qb in range(S5_QB):
        h_re = jnp.concatenate([hr_ref[qb * 4 + k] for k in range(4)], axis=1).astype(BF16)
        h_im = jnp.concatenate([hi_ref[qb * 4 + k] for k in range(4)], axis=1).astype(BF16)
        ln = slice(qb * LANES, (qb + 1) * LANES)
        yq = (jnp.dot(h_re, cdr_ref[qb], preferred_element_type=F32)
              - jnp.dot(h_im, cdi_ref[qb], preferred_element_type=F32))
        yp_ref[:, ln] = yq + d_ref[:, ln] * u[:, ln]

    y5 = yp_ref[...]
    g = 0.5 * y5 * (1.0 + lax.erf(y5 * (0.5 ** 0.5)))
    gl = jnp.dot(g.astype(BF16), wglu_ref[...], preferred_element_type=F32) + bglu_ref[...]
    yp_ref[...] = _rms(g * jax.nn.sigmoid(gl), nrm_ref[...])
    for b in range(S5_NB):
        y_ref[b] = yp_ref[b * pitch:b * pitch + lc, :].astype(BF16)


def _s5(proj3, bdr, bdi, cdr, cdi, lbr, lbi, d5, wglu, bglu, nrm, h0r, h0i):
    bsz, l, _ = proj3.shape
    lc = min(S5_LC, l)
    pitch = lc + 4
    mp = S5_NB * pitch
    kern = functools.partial(_s5_kernel, lc=lc, pitch=pitch)
    ns = S5_GROUPS * S5_STATE
    full = lambda *shape: pl.BlockSpec(shape, lambda b, c: (0,) * len(shape))
    state = pl.BlockSpec((S5_NB, ns), lambda b, c: (b, 0))
    return pl.pallas_call(
        kern,
        grid=(bsz // S5_NB, l // lc),
        in_specs=[
            pl.BlockSpec((S5_NB, lc, S5_WIDTH), lambda b, c: (b, c, (CONV_DIM + SSD_WIDTH) // S5_WIDTH)),
            full(S5_QB, LANES, 4 * LANES), full(S5_QB, LANES, 4 * LANES),
            full(S5_QB, 4 * LANES, LANES), full(S5_QB, 4 * LANES, LANES),
            full(1, ns), full(1, ns), full(1, S5_WIDTH),
            full(S5_WIDTH, S5_WIDTH), full(1, S5_WIDTH), full(1, S5_WIDTH),
            state, state,
        ],
        out_specs=[pl.BlockSpec((S5_NB, lc, S5_WIDTH), lambda b, c: (b, c, 0)), state, state],
        out_shape=[
            jax.ShapeDtypeStruct((bsz, l, S5_WIDTH), BF16),
            jax.ShapeDtypeStruct((bsz, ns), F32),
            jax.ShapeDtypeStruct((bsz, ns), F32),
        ],
        scratch_shapes=[
            pltpu.VMEM((mp, S5_WIDTH), F32),
            pltpu.VMEM((S5_SLABS, mp, LANES), F32),
            pltpu.VMEM((S5_SLABS, mp, LANES), F32),
            pltpu.VMEM((mp, S5_WIDTH), F32),
        ],
        compiler_params=_params("parallel", "arbitrary"),
        name="s5",
    )(proj3, bdr, bdi, cdr, cdi, lbr, lbi, d5, wglu, bglu, nrm, h0r, h0i)


def _outproj_kernel(x_ref, ya_ref, yb_ref, w_ref, o_ref):
    o_ref[...] = (x_ref[...]
                  + jnp.dot(ya_ref[...], w_ref[:SSD_WIDTH, :], preferred_element_type=F32)
                  + jnp.dot(yb_ref[...], w_ref[SSD_WIDTH:, :], preferred_element_type=F32))


def _out_proj(x, ya, yb, w):
    t = x.shape[0]
    tm = min(OUT_TM, t)
    return pl.pallas_call(
        _outproj_kernel,
        grid=(t // tm,),
        in_specs=[
            pl.BlockSpec((tm, D_MODEL), lambda i: (i, 0)),
            pl.BlockSpec((tm, SSD_WIDTH), lambda i: (i, 0)),
            pl.BlockSpec((tm, S5_WIDTH), lambda i: (i, 0)),
            pl.BlockSpec((D_MODEL, D_MODEL), lambda i: (0, 0)),
        ],
        out_specs=pl.BlockSpec((tm, D_MODEL), lambda i: (i, 0)),
        out_shape=jax.ShapeDtypeStruct((t, D_MODEL), F32),
        compiler_params=_params("parallel"),
        name="out_proj",
    )(x, ya, yb, w)


def _prepare(p):
    w = {}
    pad_ff = D_FF_PAD - D_FF
    for n in ("1", "2"):
        w["norm_ffn" + n] = p["norm_ffn" + n]
        w["wg" + n] = jnp.pad(p["w_ffn%s_gate" % n][0].astype(BF16), ((0, 0), (0, pad_ff)))
        w["wu" + n] = jnp.pad(p["w_ffn%s_up" % n][0].astype(BF16), ((0, 0), (0, pad_ff)))
        w["wd" + n] = jnp.pad(p["w_ffn%s_down" % n][0].astype(BF16), ((0, pad_ff), (0, 0)))
    w_in = p["w_in"][0]
    c0, c1, c2 = SSD_WIDTH, SSD_WIDTH + CONV_DIM, SSD_WIDTH + CONV_DIM + SSD_HEADS
    w["w_in"] = jnp.concatenate(
        [w_in[:, c0:c1], w_in[:, :c0], w_in[:, c2:],
         jnp.pad(w_in[:, c1:c2], ((0, 0), (0, DT_PAD - SSD_HEADS)))], axis=1).astype(BF16)
    w["norm_mix"] = p["norm_mix"]
    w["conv_w"] = p["conv_w"][0]
    w["conv_b"] = p["conv_b"]
    w["dt_bias"] = jnp.pad(p["dt_bias"], ((0, 0), (0, DT_PAD - SSD_HEADS)))
    w["a_log"] = jnp.pad(p["a_log"], ((0, 0), (0, DT_PAD - SSD_HEADS)))
    w["d_ssd"] = jnp.repeat(p["d_ssd"][0], SSD_HEADDIM)[None]
    w["norm_ssd"] = p["norm_ssd"]

    lbr, lbi, bbr, bbi = _s5_prep(
        p["s5_lambda_re"][0], p["s5_lambda_im"][0],
        jnp.broadcast_to(p["s5_log_step"][0][:, None], (S5_GROUPS, S5_STATE)),
        jnp.transpose(p["s5_b_re"][0], (2, 0, 1)), jnp.transpose(p["s5_b_im"][0], (2, 0, 1)))
    eye = jnp.eye(S5_QB, dtype=F32)
    gl = S5_GROUPS // S5_QB

    def bd(bb):
        t = jnp.einsum("hqgn,gk->qghkn", bb.reshape(S5_GROUP_CH, S5_QB, gl, S5_STATE), eye)
        return t.reshape(S5_QB, LANES, gl * S5_STATE).astype(BF16)

    def cd(cc):
        t = jnp.einsum("qghn,gk->qgnkh", cc.reshape(S5_QB, gl, S5_GROUP_CH, S5_STATE), eye)
        return t.reshape(S5_QB, gl * S5_STATE, LANES).astype(BF16)

    w["bdr"], w["bdi"] = bd(bbr), bd(bbi)
    w["cdr"], w["cdi"] = cd(p["s5_c_re"][0]), cd(p["s5_c_im"][0])
    w["lbr"] = lbr.reshape(1, -1)
    w["lbi"] = lbi.reshape(1, -1)
    w["s5_d"] = p["s5_d"][0].reshape(1, -1)
    w["w_glu"] = p["w_glu"][0].astype(BF16)
    w["b_glu"] = p["b_glu"]
    w["norm_s5"] = p["norm_s5"]
    w["w_out"] = p["w_out"][0].astype(BF16)
    w["norm_final"] = p["norm_final"][None]
    return w


def _trunk(x, conv0, ssd0, s5r0, s5i0, w):
    bsz, l, _ = x.shape
    t = bsz * l
    x1 = _ffn(x.reshape(t, D_MODEL), w["norm_ffn1"], w["wg1"], w["wu1"], w["wd1"])
    proj = _in_proj(x1, w["norm_mix"], w["w_in"]).reshape(bsz, l, PROJ_COLS)

    if l % CHUNK == 0:
        proj_ssd, valid = proj, CHUNK
    else:
        assert l < CHUNK
        proj_ssd, valid = jnp.pad(proj, ((0, 0), (0, CHUNK - l), (0, 0))), l
    ht0 = jnp.swapaxes(ssd0.reshape(bsz, SSD_WIDTH, SSD_STATE), 1, 2)
    y_ssd, conv_new, ht_new = _ssd(proj_ssd, valid, w["conv_w"], w["conv_b"], w["dt_bias"], w["a_log"],
                                   w["d_ssd"], w["norm_ssd"], conv0, ht0)
    ssd_new = jnp.swapaxes(ht_new, 1, 2).reshape(bsz, SSD_HEADS, SSD_HEADDIM, SSD_STATE)

    y_s5, s5r_new, s5i_new = _s5(proj, w["bdr"], w["bdi"], w["cdr"], w["cdi"], w["lbr"], w["lbi"],
                                 w["s5_d"], w["w_glu"], w["b_glu"], w["norm_s5"],
                                 s5r0.reshape(bsz, -1), s5i0.reshape(bsz, -1))

    x2 = _out_proj(x1, y_ssd[:, :l].reshape(t, SSD_WIDTH), y_s5.reshape(t, S5_WIDTH), w["w_out"])
    y = _ffn(x2, w["norm_ffn2"], w["wg2"], w["wu2"], w["wd2"], final_g=w["norm_final"])
    return (y.reshape(bsz, l, D_MODEL), conv_new[None], ssd_new[None],
            s5r_new.reshape(1, bsz, S5_GROUPS, S5_STATE), s5i_new.reshape(1, bsz, S5_GROUPS, S5_STATE))


def kernel(x_prompt, x_sample, cache_conv, state_ssd, state_s5_re, state_s5_im, norm_ffn1, w_ffn1_gate, w_ffn1_up, w_ffn1_down, norm_mix, w_in, conv_w, conv_b, dt_bias, a_log, d_ssd, norm_ssd, s5_lambda_re, s5_lambda_im, s5_log_step, s5_b_re, s5_b_im, s5_c_re, s5_c_im, s5_d, w_glu, b_glu, norm_s5, w_out, norm_ffn2, w_ffn2_gate, w_ffn2_up, w_ffn2_down, norm_final):
    p = dict(norm_ffn1=norm_ffn1, w_ffn1_gate=w_ffn1_gate, w_ffn1_up=w_ffn1_up, w_ffn1_down=w_ffn1_down,
             norm_mix=norm_mix, w_in=w_in, conv_w=conv_w, conv_b=conv_b, dt_bias=dt_bias, a_log=a_log,
             d_ssd=d_ssd, norm_ssd=norm_ssd, s5_lambda_re=s5_lambda_re, s5_lambda_im=s5_lambda_im,
             s5_log_step=s5_log_step, s5_b_re=s5_b_re, s5_b_im=s5_b_im, s5_c_re=s5_c_re, s5_c_im=s5_c_im,
             s5_d=s5_d, w_glu=w_glu, b_glu=b_glu, norm_s5=norm_s5, w_out=w_out, norm_ffn2=norm_ffn2,
             w_ffn2_gate=w_ffn2_gate, w_ffn2_up=w_ffn2_up, w_ffn2_down=w_ffn2_down, norm_final=norm_final)
    w = _prepare(p)
    bsz = x_prompt.shape[0]
    zeros = lambda *s: jnp.zeros(s, F32)
    out_p = _trunk(x_prompt, zeros(bsz, CONV_W - 1, CONV_DIM),
                   zeros(bsz, SSD_HEADS, SSD_HEADDIM, SSD_STATE),
                   zeros(bsz, S5_GROUPS, S5_STATE), zeros(bsz, S5_GROUPS, S5_STATE), w)
    out_s = _trunk(x_sample, cache_conv[0], state_ssd[0], state_s5_re[0], state_s5_im[0], w)
    return (out_p[0], out_s[0]) + out_p[1:] + out_s[1:]
```

```python
import functools

import jax
import jax.numpy as jnp
from jax import lax
from jax.experimental import pallas as pl
from jax.experimental.pallas import tpu as pltpu

F32 = jnp.float32
BF16 = jnp.bfloat16
EPS = 1e-6

D_MODEL = 2048
D_FF = 5504
SSD_WIDTH = 1024
SSD_HEADDIM = 64
SSD_HEADS = 16
SSD_GROUPS = 4
SSD_STATE = 128
CONV_W = 4
CONV_DIM = SSD_WIDTH + 2 * SSD_GROUPS * SSD_STATE
S5_WIDTH = 1024
S5_GROUP_CH = 16
S5_GROUPS = 64
S5_STATE = 64
CHUNK = 64

LANES = 128
D_FF_PAD = 5632
FFN_TM = 512
FFN_TF = 512
DT_PAD = LANES
PROJ_COLS = CONV_DIM + SSD_WIDTH + S5_WIDTH + DT_PAD
PROJ_TN = PROJ_COLS // 3
PROJ_TM = 1024
OUT_TM = 512
S5_NB = 8
S5_LC = 32
S5_SLABS = S5_GROUPS * S5_STATE // LANES
S5_QB = S5_WIDTH // LANES
VMEM_LIMIT = 56 * 1024 * 1024


def _rms(x, g):
    ms = jnp.mean(x * x, axis=-1, keepdims=True)
    return x * lax.rsqrt(ms + EPS) * g


def _silu(x):
    return x * jax.nn.sigmoid(x)


def _params(*sem):
    return pltpu.CompilerParams(dimension_semantics=sem, vmem_limit_bytes=VMEM_LIMIT)


def _ffn_kernel(x_ref, g_ref, wg_ref, wu_ref, wd_ref, gf_ref, o_ref, h_ref, *, final_norm):
    j = pl.program_id(1)

    @pl.when(j == 0)
    def _():
        x = x_ref[...]
        h_ref[...] = _rms(x, g_ref[...]).astype(BF16)
        o_ref[...] = x

    h = h_ref[...]
    gate = jnp.dot(h, wg_ref[...], preferred_element_type=F32)
    up = jnp.dot(h, wu_ref[...], preferred_element_type=F32)
    a = (0.5 * (_silu(gate) * up)).astype(BF16)
    o_ref[...] += jnp.dot(a, wd_ref[...], preferred_element_type=F32)

    if final_norm:
        @pl.when(j == pl.num_programs(1) - 1)
        def _():
            o_ref[...] = _rms(o_ref[...], gf_ref[...])


def _ffn(x, norm_g, wg, wu, wd, final_g=None):
    t = x.shape[0]
    tm = min(FFN_TM, t)
    kern = functools.partial(_ffn_kernel, final_norm=final_g is not None)
    gf = norm_g if final_g is None else final_g
    return pl.pallas_call(
        kern,
        grid=(t // tm, D_FF_PAD // FFN_TF),
        in_specs=[
            pl.BlockSpec((tm, D_MODEL), lambda i, j: (i, 0)),
            pl.BlockSpec((1, D_MODEL), lambda i, j: (0, 0)),
            pl.BlockSpec((D_MODEL, FFN_TF), lambda i, j: (0, j)),
            pl.BlockSpec((D_MODEL, FFN_TF), lambda i, j: (0, j)),
            pl.BlockSpec((FFN_TF, D_MODEL), lambda i, j: (j, 0)),
            pl.BlockSpec((1, D_MODEL), lambda i, j: (0, 0)),
        ],
        out_specs=pl.BlockSpec((tm, D_MODEL), lambda i, j: (i, 0)),
        out_shape=jax.ShapeDtypeStruct((t, D_MODEL), F32),
        scratch_shapes=[pltpu.VMEM((tm, D_MODEL), BF16)],
        compiler_params=_params("parallel", "arbitrary"),
        name="ffn",
    )(x, norm_g, wg, wu, wd, gf)


def _inproj_kernel(x_ref, g_ref, w_ref, o_ref, h_ref):
    @pl.when(pl.program_id(1) == 0)
    def _():
        h_ref[...] = _rms(x_ref[...], g_ref[...]).astype(BF16)

    o_ref[...] = jnp.dot(h_ref[...], w_ref[...], preferred_element_type=F32)


def _in_proj(x, norm_g, w):
    t = x.shape[0]
    tm = min(PROJ_TM, t)
    return pl.pallas_call(
        _inproj_kernel,
        grid=(t // tm, PROJ_COLS // PROJ_TN),
        in_specs=[
            pl.BlockSpec((tm, D_MODEL), lambda i, j: (i, 0)),
            pl.BlockSpec((1, D_MODEL), lambda i, j: (0, 0)),
            pl.BlockSpec((D_MODEL, PROJ_TN), lambda i, j: (0, j)),
        ],
        out_specs=pl.BlockSpec((tm, PROJ_TN), lambda i, j: (i, j)),
        out_shape=jax.ShapeDtypeStruct((t, PROJ_COLS), F32),
        scratch_shapes=[pltpu.VMEM((tm, D_MODEL), BF16)],
        compiler_params=_params("parallel", "arbitrary"),
        name="in_proj",
    )(x, norm_g, w)


def _expand_heads(v):
    q = v.shape[0]
    lane = lax.broadcasted_iota(jnp.int32, (q, LANES), 1)
    blocks = []
    for k in range(SSD_WIDTH // LANES):
        a = jnp.broadcast_to(v[:, 2 * k:2 * k + 1], (q, LANES))
        b = jnp.broadcast_to(v[:, 2 * k + 1:2 * k + 2], (q, LANES))
        blocks.append(jnp.where(lane < SSD_HEADDIM, a, b))
    return jnp.concatenate(blocks, axis=1)


def _ssd_kernel(xbc_ref, z_ref, dtr_ref, cw_ref, cb_ref, dtb_ref, alog_ref, d_ref, nrm_ref,
                conv0_ref, h0_ref, y_ref, convo_ref, ht_ref, pad_ref, *, valid):
    q = CHUNK
    c = pl.program_id(1)
    gw = SSD_WIDTH // SSD_GROUPS

    @pl.when(c == 0)
    def _():
        pad_ref[5:8, :] = conv0_ref[0]
        ht_ref[0] = h0_ref[0]

    raw = xbc_ref[0]
    pad_ref[8:8 + q, :] = raw
    cw = cw_ref[...]
    conv = (cb_ref[...] + pad_ref[5:5 + q, :] * cw[0:1, :] + pad_ref[6:6 + q, :] * cw[1:2, :]
            + pad_ref[7:7 + q, :] * cw[2:3, :] + raw * cw[3:4, :])
    pad_ref[5:8, :] = pad_ref[8 + valid - 3:8 + valid, :]
    act = _silu(conv)
    xs = act[:, :SSD_WIDTH]
    bm = act[:, SSD_WIDTH:SSD_WIDTH + SSD_GROUPS * SSD_STATE].astype(BF16)
    cm = act[:, SSD_WIDTH + SSD_GROUPS * SSD_STATE:].astype(BF16)

    dtc = jax.nn.softplus(dtr_ref[0] + dtb_ref[...])
    if valid < q:
        dtc = jnp.where(lax.broadcasted_iota(jnp.int32, dtc.shape, 0) < valid, dtc, 0.0)
    da = dtc * (-jnp.exp(alog_ref[...]))
    tril = (lax.broadcasted_iota(jnp.int32, (q, q), 1)
            <= lax.broadcasted_iota(jnp.int32, (q, q), 0)).astype(F32)
    acs_c = jnp.dot(tril, da, precision=lax.Precision.HIGHEST, preferred_element_type=F32)
    dt_e = _expand_heads(dtc)
    acs_e = _expand_heads(acs_c)
    row = lax.broadcasted_iota(jnp.int32, (q, SSD_WIDTH), 0)
    jj = lax.broadcasted_iota(jnp.int32, (q, SSD_WIDTH), 1) & (SSD_HEADDIM - 1)
    acs_row = jnp.sum(jnp.where(jj == row, acs_e, 0.0), axis=0, keepdims=True)
    lmat = jnp.exp(jnp.where(jj <= row, acs_e - acs_row, -jnp.inf))
    last = acs_e[q - 1:q, :]
    eacs = jnp.exp(acs_e)
    dec_end = jnp.exp(last - acs_e)
    cdec = jnp.exp(last)
    xdt = xs * dt_e
    xdt_b = xdt.astype(BF16)
    xd_b = (xdt * dec_end).astype(BF16)
    head_of_lane = lax.broadcasted_iota(jnp.int32, (q, gw), 1) >> 6

    ys = []
    for g in range(SSD_GROUPS):
        sl = slice(g * gw, (g + 1) * gw)
        bg = bm[:, g * SSD_STATE:(g + 1) * SSD_STATE]
        cg = cm[:, g * SSD_STATE:(g + 1) * SSD_STATE]
        b4 = jnp.concatenate([bg] * (SSD_HEADS // SSD_GROUPS), axis=0)
        cbt = lax.dot_general(cg, b4, (((1,), (1,)), ((), ())), preferred_element_type=F32)
        sg = (cbt * lmat[:, sl]).astype(BF16)
        xg = xdt_b[:, sl]
        xbd = jnp.concatenate(
            [jnp.where(head_of_lane == hh, xg, jnp.zeros_like(xg)) for hh in range(SSD_HEADS // SSD_GROUPS)],
            axis=0)
        ydiag = jnp.dot(sg, xbd, preferred_element_type=F32)
        htg = ht_ref[0, :, sl]
        yoff = jnp.dot(cg, htg.astype(BF16), preferred_element_type=F32)
        ys.append(ydiag + eacs[:, sl] * yoff)
        st = lax.dot_general(bg, xd_b[:, sl], (((0,), (0,)), ((), ())), preferred_element_type=F32)
        ht_ref[0, :, sl] = cdec[:, sl] * htg + st

    y = jnp.concatenate(ys, axis=1) + d_ref[...] * xs
    y = y * _silu(z_ref[0])
    y_ref[0] = _rms(y, nrm_ref[...]).astype(BF16)

    @pl.when(c == pl.num_programs(1) - 1)
    def _():
        convo_ref[0] = pad_ref[5:8, :]


def _ssd(proj3, valid, cw, cb, dtb, alog, d_e, nrm, conv0, ht0):
    bsz, lp, _ = proj3.shape
    q = CHUNK
    kern = functools.partial(_ssd_kernel, valid=valid)
    row = lambda n: pl.BlockSpec((1, n), lambda b, c: (0, 0))
    return pl.pallas_call(
        kern,
        grid=(bsz, lp // q),
        in_specs=[
            pl.BlockSpec((1, q, CONV_DIM), lambda b, c: (b, c, 0)),
            pl.BlockSpec((1, q, SSD_WIDTH), lambda b, c: (b, c, CONV_DIM // SSD_WIDTH)),
            pl.BlockSpec((1, q, DT_PAD), lambda b, c: (b, c, (PROJ_COLS - DT_PAD) // DT_PAD)),
            pl.BlockSpec((CONV_W, CONV_DIM), lambda b, c: (0, 0)),
            row(CONV_DIM), row(DT_PAD), row(DT_PAD), row(SSD_WIDTH), row(SSD_WIDTH),
            pl.BlockSpec((1, CONV_W - 1, CONV_DIM), lambda b, c: (b, 0, 0)),
            pl.BlockSpec((1, SSD_STATE, SSD_WIDTH), lambda b, c: (b, 0, 0)),
        ],
        out_specs=[
            pl.BlockSpec((1, q, SSD_WIDTH), lambda b, c: (b, c, 0)),
            pl.BlockSpec((1, CONV_W - 1, CONV_DIM), lambda b, c: (b, 0, 0)),
            pl.BlockSpec((1, SSD_STATE, SSD_WIDTH), lambda b, c: (b, 0, 0)),
        ],
        out_shape=[
            jax.ShapeDtypeStruct((bsz, lp, SSD_WIDTH), BF16),
            jax.ShapeDtypeStruct((bsz, CONV_W - 1, CONV_DIM), F32),
            jax.ShapeDtypeStruct((bsz, SSD_STATE, SSD_WIDTH), F32),
        ],
        scratch_shapes=[pltpu.VMEM((8 + q, CONV_DIM), F32)],
        compiler_params=_params("parallel", "arbitrary"),
        name="ssd",
    )(proj3, proj3, proj3, cw, cb, dtb, alog, d_e, nrm, conv0, ht0)


def _s5_prep_kernel(lre_ref, lim_ref, lstep_ref, bre_ref, bim_ref, lbr_ref, lbi_ref, bbr_ref, bbi_ref):
    lam_re = lre_ref[...]
    lam_im = lim_ref[...]
    step = jnp.exp(lstep_ref[...])
    mag = jnp.exp(lam_re * step)
    ang = lam_im * step
    lb_re = mag * jnp.cos(ang)
    lb_im = mag * jnp.sin(ang)
    den = lam_re * lam_re + lam_im * lam_im
    nr = lb_re - 1.0
    q_re = (nr * lam_re + lb_im * lam_im) / den
    q_im = (lb_im * lam_re - nr * lam_im) / den
    lbr_ref[...] = lb_re
    lbi_ref[...] = lb_im
    b_re = bre_ref[...]
    b_im = bim_ref[...]
    bbr_ref[...] = q_re[None] * b_re - q_im[None] * b_im
    bbi_ref[...] = q_re[None] * b_im + q_im[None] * b_re


def _s5_prep(lam_re, lam_im, log_step_e, b_re_t, b_im_t):
    gn = jax.ShapeDtypeStruct((S5_GROUPS, S5_STATE), F32)
    hgn = jax.ShapeDtypeStruct((S5_GROUP_CH, S5_GROUPS, S5_STATE), F32)
    return pl.pallas_call(_s5_prep_kernel, out_shape=[gn, gn, hgn, hgn], name="s5_prep")(
        lam_re, lam_im, log_step_e, b_re_t, b_im_t)


def _s5_kernel(u_ref, bdr_ref, bdi_ref, cdr_ref, cdi_ref, lbr_ref, lbi_ref, d_ref, wglu_ref, bglu_ref,
               nrm_ref, h0r_ref, h0i_ref, y_ref, hor_ref, hoi_ref, up_ref, yp_ref, *slab_refs,
               lc, pitch):
    c = pl.program_id(1)

    @pl.when(c == 0)
    def _():
        hor_ref[...] = h0r_ref[...]
        hoi_ref[...] = h0i_ref[...]
        up_ref[...] = jnp.zeros_like(up_ref)

    for b in range(S5_NB):
        up_ref[b * pitch:b * pitch + lc, :] = u_ref[b]
    u = up_ref[...]
    ub = u.astype(BF16)

    spb = S5_SLABS // S5_QB
    hr_refs, hi_refs = slab_refs[:S5_QB], slab_refs[S5_QB:]

    def input_stage(qb):
        uq = ub[:, qb * LANES:(qb + 1) * LANES]
        bu_r = jnp.dot(uq, bdr_ref[qb], preferred_element_type=F32)
        bu_i = jnp.dot(uq, bdi_ref[qb], preferred_element_type=F32)
        for k in range(spb):
            hr_refs[qb][k] = bu_r[:, k * LANES:(k + 1) * LANES]
            hi_refs[qb][k] = bu_i[:, k * LANES:(k + 1) * LANES]

    def scan_stage(qb):
        hr_ref, hi_ref = hr_refs[qb], hi_refs[qb]
        for k in range(spb):
            sl = slice((qb * spb + k) * LANES, (qb * spb + k + 1) * LANES)
            lr = jnp.broadcast_to(lbr_ref[:, sl], (S5_NB, LANES))
            li = jnp.broadcast_to(lbi_ref[:, sl], (S5_NB, LANES))
            h_r, h_i = hor_ref[:, sl], hoi_ref[:, sl]
            for t in range(lc):
                rows = pl.ds(t, S5_NB, stride=pitch)
                n_r = lr * h_r - li * h_i + hr_ref[k, rows, :]
                n_i = lr * h_i + li * h_r + hi_ref[k, rows, :]
                hr_ref[k, rows, :] = n_r
                hi_ref[k, rows, :] = n_i
                h_r, h_i = n_r, n_i
            hor_ref[:, sl] = h_r
            hoi_ref[:, sl] = h_i

    def output_stage(qb):
        ln = slice(qb * LANES, (qb + 1) * LANES)
        h_re = jnp.concatenate([hr_refs[qb][k] for k in range(spb)], axis=1).astype(BF16)
        h_im = jnp.concatenate([hi_refs[qb][k] for k in range(spb)], axis=1).astype(BF16)
        yq = (jnp.dot(h_re, cdr_ref[qb], preferred_element_type=F32)
              - jnp.dot(h_im, cdi_ref[qb], preferred_element_type=F32))
        yp_ref[:, ln] = yq + d_ref[:, ln] * u[:, ln]

    for step in range(S5_QB + 2):
        if step < S5_QB:
            input_stage(step)
        if 0 <= step - 1 < S5_QB:
            scan_stage(step - 1)
        if 0 <= step - 2 < S5_QB:
            output_stage(step - 2)

    y5 = yp_ref[...]
    g = 0.5 * y5 * (1.0 + lax.erf(y5 * (0.5 ** 0.5)))
    gl = jnp.dot(g.astype(BF16), wglu_ref[...], preferred_element_type=F32) + bglu_ref[...]
    yp_ref[...] = _rms(g * jax.nn.sigmoid(gl), nrm_ref[...])
    for b in range(S5_NB):
        y_ref[b] = yp_ref[b * pitch:b * pitch + lc, :].astype(BF16)


def _s5(proj3, bdr, bdi, cdr, cdi, lbr, lbi, d5, wglu, bglu, nrm, h0r, h0i):
    bsz, l, _ = proj3.shape
    lc = min(S5_LC, l)
    pitch = lc + 4
    mp = S5_NB * pitch
    kern = functools.partial(_s5_kernel, lc=lc, pitch=pitch)
    ns = S5_GROUPS * S5_STATE
    full = lambda *shape: pl.BlockSpec(shape, lambda b, c: (0,) * len(shape))
    state = pl.BlockSpec((S5_NB, ns), lambda b, c: (b, 0))
    return pl.pallas_call(
        kern,
        grid=(bsz // S5_NB, l // lc),
        in_specs=[
            pl.BlockSpec((S5_NB, lc, S5_WIDTH), lambda b, c: (b, c, (CONV_DIM + SSD_WIDTH) // S5_WIDTH)),
            full(S5_QB, LANES, 4 * LANES), full(S5_QB, LANES, 4 * LANES),
            full(S5_QB, 4 * LANES, LANES), full(S5_QB, 4 * LANES, LANES),
            full(1, ns), full(1, ns), full(1, S5_WIDTH),
            full(S5_WIDTH, S5_WIDTH), full(1, S5_WIDTH), full(1, S5_WIDTH),
            state, state,
        ],
        out_specs=[pl.BlockSpec((S5_NB, lc, S5_WIDTH), lambda b, c: (b, c, 0)), state, state],
        out_shape=[
            jax.ShapeDtypeStruct((bsz, l, S5_WIDTH), BF16),
            jax.ShapeDtypeStruct((bsz, ns), F32),
            jax.ShapeDtypeStruct((bsz, ns), F32),
        ],
        scratch_shapes=[
            pltpu.VMEM((mp, S5_WIDTH), F32),
            pltpu.VMEM((mp, S5_WIDTH), F32),
        ] + [pltpu.VMEM((S5_SLABS // S5_QB, mp, LANES), F32)] * (2 * S5_QB),
        compiler_params=_params("parallel", "arbitrary"),
        name="s5",
    )(proj3, bdr, bdi, cdr, cdi, lbr, lbi, d5, wglu, bglu, nrm, h0r, h0i)


def _outproj_kernel(x_ref, ya_ref, yb_ref, w_ref, o_ref):
    o_ref[...] = (x_ref[...]
                  + jnp.dot(ya_ref[...], w_ref[:SSD_WIDTH, :], preferred_element_type=F32)
                  + jnp.dot(yb_ref[...], w_ref[SSD_WIDTH:, :], preferred_element_type=F32))


def _out_proj(x, ya, yb, w):
    t = x.shape[0]
    tm = min(OUT_TM, t)
    return pl.pallas_call(
        _outproj_kernel,
        grid=(t // tm,),
        in_specs=[
            pl.BlockSpec((tm, D_MODEL), lambda i: (i, 0)),
            pl.BlockSpec((tm, SSD_WIDTH), lambda i: (i, 0)),
            pl.BlockSpec((tm, S5_WIDTH), lambda i: (i, 0)),
            pl.BlockSpec((D_MODEL, D_MODEL), lambda i: (0, 0)),
        ],
        out_specs=pl.BlockSpec((tm, D_MODEL), lambda i: (i, 0)),
        out_shape=jax.ShapeDtypeStruct((t, D_MODEL), F32),
        compiler_params=_params("parallel"),
        name="out_proj",
    )(x, ya, yb, w)


def _prepare(p):
    w = {}
    pad_ff = D_FF_PAD - D_FF
    for n in ("1", "2"):
        w["norm_ffn" + n] = p["norm_ffn" + n]
        w["wg" + n] = jnp.pad(p["w_ffn%s_gate" % n][0], ((0, 0), (0, pad_ff))).astype(BF16)
        w["wu" + n] = jnp.pad(p["w_ffn%s_up" % n][0], ((0, 0), (0, pad_ff))).astype(BF16)
        w["wd" + n] = jnp.pad(p["w_ffn%s_down" % n][0], ((0, pad_ff), (0, 0))).astype(BF16)
    w_in = p["w_in"][0]
    c0, c1, c2 = SSD_WIDTH, SSD_WIDTH + CONV_DIM, SSD_WIDTH + CONV_DIM + SSD_HEADS
    w["w_in"] = jnp.concatenate(
        [w_in[:, c0:c1], w_in[:, :c0], w_in[:, c2:],
         jnp.pad(w_in[:, c1:c2], ((0, 0), (0, DT_PAD - SSD_HEADS)))], axis=1).astype(BF16)
    w["norm_mix"] = p["norm_mix"]
    w["conv_w"] = p["conv_w"][0]
    w["conv_b"] = p["conv_b"]
    w["dt_bias"] = jnp.pad(p["dt_bias"], ((0, 0), (0, DT_PAD - SSD_HEADS)))
    w["a_log"] = jnp.pad(p["a_log"], ((0, 0), (0, DT_PAD - SSD_HEADS)))
    w["d_ssd"] = jnp.repeat(p["d_ssd"][0], SSD_HEADDIM)[None]
    w["norm_ssd"] = p["norm_ssd"]

    lbr, lbi, bbr, bbi = _s5_prep(
        p["s5_lambda_re"][0], p["s5_lambda_im"][0],
        jnp.broadcast_to(p["s5_log_step"][0][:, None], (S5_GROUPS, S5_STATE)),
        jnp.transpose(p["s5_b_re"][0], (2, 0, 1)), jnp.transpose(p["s5_b_im"][0], (2, 0, 1)))
    eye = jnp.eye(S5_QB, dtype=F32)
    gl = S5_GROUPS // S5_QB

    def bd(bb):
        t = jnp.einsum("hqgn,gk->qghkn", bb.reshape(S5_GROUP_CH, S5_QB, gl, S5_STATE), eye)
        return t.reshape(S5_QB, LANES, gl * S5_STATE).astype(BF16)

    def cd(cc):
        t = jnp.einsum("qghn,gk->qgnkh", cc.reshape(S5_QB, gl, S5_GROUP_CH, S5_STATE), eye)
        return t.reshape(S5_QB, gl * S5_STATE, LANES).astype(BF16)

    w["bdr"], w["bdi"] = bd(bbr), bd(bbi)
    w["cdr"], w["cdi"] = cd(p["s5_c_re"][0]), cd(p["s5_c_im"][0])
    w["lbr"] = lbr.reshape(1, -1)
    w["lbi"] = lbi.reshape(1, -1)
    w["s5_d"] = p["s5_d"][0].reshape(1, -1)
    w["w_glu"] = p["w_glu"][0].astype(BF16)
    w["b_glu"] = p["b_glu"]
    w["norm_s5"] = p["norm_s5"]
    w["w_out"] = p["w_out"][0].astype(BF16)
    w["norm_final"] = p["norm_final"][None]
    return w


def _trunk(x, conv0, ssd0, s5r0, s5i0, w):
    bsz, l, _ = x.shape
    t = bsz * l
    x1 = _ffn(x.reshape(t, D_MODEL), w["norm_ffn1"], w["wg1"], w["wu1"], w["wd1"])
    proj = _in_proj(x1, w["norm_mix"], w["w_in"]).reshape(bsz, l, PROJ_COLS)

    if l % CHUNK == 0:
        proj_ssd, valid = proj, CHUNK
    else:
        assert l < CHUNK
        proj_ssd, valid = jnp.pad(proj, ((0, 0), (0, CHUNK - l), (0, 0))), l
    ht0 = jnp.swapaxes(ssd0.reshape(bsz, SSD_WIDTH, SSD_STATE), 1, 2)
    y_ssd, conv_new, ht_new = _ssd(proj_ssd, valid, w["conv_w"], w["conv_b"], w["dt_bias"], w["a_log"],
                                   w["d_ssd"], w["norm_ssd"], conv0, ht0)
    ssd_new = jnp.swapaxes(ht_new, 1, 2).reshape(bsz, SSD_HEADS, SSD_HEADDIM, SSD_STATE)

    y_s5, s5r_new, s5i_new = _s5(proj, w["bdr"], w["bdi"], w["cdr"], w["cdi"], w["lbr"], w["lbi"],
                                 w["s5_d"], w["w_glu"], w["b_glu"], w["norm_s5"],
                                 s5r0.reshape(bsz, -1), s5i0.reshape(bsz, -1))

    x2 = _out_proj(x1, y_ssd[:, :l].reshape(t, SSD_WIDTH), y_s5.reshape(t, S5_WIDTH), w["w_out"])
    y = _ffn(x2, w["norm_ffn2"], w["wg2"], w["wu2"], w["wd2"], final_g=w["norm_final"])
    return (y.reshape(bsz, l, D_MODEL), conv_new[None], ssd_new[None],
            s5r_new.reshape(1, bsz, S5_GROUPS, S5_STATE), s5i_new.reshape(1, bsz, S5_GROUPS, S5_STATE))


def kernel(x_prompt, x_sample, cache_conv, state_ssd, state_s5_re, state_s5_im, norm_ffn1, w_ffn1_gate, w_ffn1_up, w_ffn1_down, norm_mix, w_in, conv_w, conv_b, dt_bias, a_log, d_ssd, norm_ssd, s5_lambda_re, s5_lambda_im, s5_log_step, s5_b_re, s5_b_im, s5_c_re, s5_c_im, s5_d, w_glu, b_glu, norm_s5, w_out, norm_ffn2, w_ffn2_gate, w_ffn2_up, w_ffn2_down, norm_final):
    p = dict(norm_ffn1=norm_ffn1, w_ffn1_gate=w_ffn1_gate, w_ffn1_up=w_ffn1_up, w_ffn1_down=w_ffn1_down,
             norm_mix=norm_mix, w_in=w_in, conv_w=conv_w, conv_b=conv_b, dt_bias=dt_bias, a_log=a_log,
             d_ssd=d_ssd, norm_ssd=norm_ssd, s5_lambda_re=s5_lambda_re, s5_lambda_im=s5_lambda_im,
             s5_log_step=s5_log_step, s5_b_re=s5_b_re, s5_b_im=s5_b_im, s5_c_re=s5_c_re, s5_c_im=s5_c_im,
             s5_d=s5_d, w_glu=w_glu, b_glu=b_glu, norm_s5=norm_s5, w_out=w_out, norm_ffn2=norm_ffn2,
             w_ffn2_gate=w_ffn2_gate, w_ffn2_up=w_ffn2_up, w_ffn2_down=w_ffn2_down, norm_final=norm_final)
    w = _prepare(p)
    bsz = x_prompt.shape[0]
    zeros = lambda *s: jnp.zeros(s, F32)
    out_p = _trunk(x_prompt, zeros(bsz, CONV_W - 1, CONV_DIM),
                   zeros(bsz, SSD_HEADS, SSD_HEADDIM, SSD_STATE),
                   zeros(bsz, S5_GROUPS, S5_STATE), zeros(bsz, S5_GROUPS, S5_STATE), w)
    out_s = _trunk(x_sample, cache_conv[0], state_ssd[0], state_s5_re[0], state_s5_im[0], w)
    return (out_p[0], out_s[0]) + out_p[1:] + out_s[1:]
```

```python
import functools

import jax
import jax.numpy as jnp
from jax import lax
from jax.experimental import pallas as pl
from jax.experimental.pallas import tpu as pltpu

F32 = jnp.float32
BF16 = jnp.bfloat16
EPS = 1e-6

D_MODEL = 2048
D_FF = 5504
SSD_WIDTH = 1024
SSD_HEADDIM = 64
SSD_HEADS = 16
SSD_GROUPS = 4
SSD_STATE = 128
CONV_W = 4
CONV_DIM = SSD_WIDTH + 2 * SSD_GROUPS * SSD_STATE
S5_WIDTH = 1024
S5_GROUP_CH = 16
S5_GROUPS = 64
S5_STATE = 64
CHUNK = 64
SSD_NSUB = 2

LANES = 128
D_FF_PAD = 5632
FFN_TM = 1024
FFN_TF = 512
DT_PAD = LANES
PROJ_COLS = CONV_DIM + SSD_WIDTH + S5_WIDTH + DT_PAD
PROJ_TN = PROJ_COLS // 3
PROJ_TM = 1024
OUT_TM = 512
S5_NB = 8
S5_LC = 32
S5_SLABS = S5_GROUPS * S5_STATE // LANES
S5_QB = S5_WIDTH // LANES
VMEM_LIMIT = 56 * 1024 * 1024


def _rms(x, g):
    ms = jnp.mean(x * x, axis=-1, keepdims=True)
    return x * lax.rsqrt(ms + EPS) * g


def _silu(x):
    return x * jax.nn.sigmoid(x)


def _params(*sem):
    return pltpu.CompilerParams(dimension_semantics=sem, vmem_limit_bytes=VMEM_LIMIT)


def _ffn_kernel(x_ref, g_ref, wg_ref, wu_ref, wd_ref, gf_ref, o_ref, h_ref, *, final_norm):
    j = pl.program_id(1)

    @pl.when(j == 0)
    def _():
        x = x_ref[...]
        h_ref[...] = _rms(x, g_ref[...]).astype(BF16)
        o_ref[...] = x

    h = h_ref[...]
    gate = jnp.dot(h, wg_ref[...], preferred_element_type=F32)
    up = jnp.dot(h, wu_ref[...], preferred_element_type=F32)
    a = (0.5 * (_silu(gate) * up)).astype(BF16)
    o_ref[...] += jnp.dot(a, wd_ref[...], preferred_element_type=F32)

    if final_norm:
        @pl.when(j == pl.num_programs(1) - 1)
        def _():
            o_ref[...] = _rms(o_ref[...], gf_ref[...])


def _ffn(x, norm_g, wg, wu, wd, final_g=None):
    t = x.shape[0]
    tm = min(FFN_TM, t)
    kern = functools.partial(_ffn_kernel, final_norm=final_g is not None)
    gf = norm_g if final_g is None else final_g
    return pl.pallas_call(
        kern,
        grid=(t // tm, D_FF_PAD // FFN_TF),
        in_specs=[
            pl.BlockSpec((tm, D_MODEL), lambda i, j: (i, 0)),
            pl.BlockSpec((1, D_MODEL), lambda i, j: (0, 0)),
            pl.BlockSpec((D_MODEL, FFN_TF), lambda i, j: (0, j)),
            pl.BlockSpec((D_MODEL, FFN_TF), lambda i, j: (0, j)),
            pl.BlockSpec((FFN_TF, D_MODEL), lambda i, j: (j, 0)),
            pl.BlockSpec((1, D_MODEL), lambda i, j: (0, 0)),
        ],
        out_specs=pl.BlockSpec((tm, D_MODEL), lambda i, j: (i, 0)),
        out_shape=jax.ShapeDtypeStruct((t, D_MODEL), F32),
        scratch_shapes=[pltpu.VMEM((tm, D_MODEL), BF16)],
        compiler_params=_params("parallel", "arbitrary"),
        name="ffn",
    )(x, norm_g, wg, wu, wd, gf)


def _inproj_kernel(x_ref, g_ref, w_ref, o_ref, h_ref):
    @pl.when(pl.program_id(1) == 0)
    def _():
        h_ref[...] = _rms(x_ref[...], g_ref[...]).astype(BF16)

    o_ref[...] = jnp.dot(h_ref[...], w_ref[...], preferred_element_type=F32)


def _in_proj(x, norm_g, w):
    t = x.shape[0]
    tm = min(PROJ_TM, t)
    return pl.pallas_call(
        _inproj_kernel,
        grid=(t // tm, PROJ_COLS // PROJ_TN),
        in_specs=[
            pl.BlockSpec((tm, D_MODEL), lambda i, j: (i, 0)),
            pl.BlockSpec((1, D_MODEL), lambda i, j: (0, 0)),
            pl.BlockSpec((D_MODEL, PROJ_TN), lambda i, j: (0, j)),
        ],
        out_specs=pl.BlockSpec((tm, PROJ_TN), lambda i, j: (i, j)),
        out_shape=jax.ShapeDtypeStruct((t, PROJ_COLS), F32),
        scratch_shapes=[pltpu.VMEM((tm, D_MODEL), BF16)],
        compiler_params=_params("parallel", "arbitrary"),
        name="in_proj",
    )(x, norm_g, w)


def _expand_heads(v):
    q = v.shape[0]
    lane = lax.broadcasted_iota(jnp.int32, (q, LANES), 1)
    blocks = []
    for k in range(SSD_WIDTH // LANES):
        a = jnp.broadcast_to(v[:, 2 * k:2 * k + 1], (q, LANES))
        b = jnp.broadcast_to(v[:, 2 * k + 1:2 * k + 2], (q, LANES))
        blocks.append(jnp.where(lane < SSD_HEADDIM, a, b))
    return jnp.concatenate(blocks, axis=1)


def _slab(s):
    return slice(s * LANES, (s + 1) * LANES)


def _ssd_chunk(dtr_ref, cw_ref, cb_ref, dtb_ref, alog_ref, d_ref, nrm_ref, y_ref, ht_ref, pad_ref, zs_ref,
               *, base, valid):
    q = CHUNK
    hq = q // 2
    gw = SSD_WIDTH // SSD_GROUPS
    slab = _slab
    row_time = lambda r: ((r & (hq - 1)) << 1) | (r >> (hq.bit_length() - 1))

    def split_rows(ref3, s, start):
        return [ref3[s, pl.ds(start + e, hq, stride=2), :] for e in range(2)]

    cw = cw_ref[...]
    cb = cb_ref[...]
    cols = []
    for s in range(CONV_DIM // LANES):
        taps = [split_rows(pad_ref, s, base + 5 + k) for k in range(CONV_W)]
        halves = []
        for e in range(2):
            acc = cb[:, slab(s)] + taps[0][e] * cw[0:1, slab(s)]
            for k in range(1, CONV_W):
                acc = acc + taps[k][e] * cw[k:k + 1, slab(s)]
            halves.append(acc)
        cols.append(jnp.concatenate(halves, axis=0))
    act = _silu(jnp.concatenate(cols, axis=1))
    xs = act[:, :SSD_WIDTH]
    bm = act[:, SSD_WIDTH:SSD_WIDTH + SSD_GROUPS * SSD_STATE].astype(BF16)
    cm = act[:, SSD_WIDTH + SSD_GROUPS * SSD_STATE:].astype(BF16)

    dt_raw = jnp.concatenate([dtr_ref[0, pl.ds(base + e, hq, stride=2), :] for e in range(2)], axis=0)
    dtc = jax.nn.softplus(dt_raw + dtb_ref[...])
    if valid < q:
        dtc = jnp.where(row_time(lax.broadcasted_iota(jnp.int32, dtc.shape, 0)) < valid, dtc, 0.0)
    da = dtc * (-jnp.exp(alog_ref[...]))
    tril = (row_time(lax.broadcasted_iota(jnp.int32, (q, q), 1))
            <= row_time(lax.broadcasted_iota(jnp.int32, (q, q), 0))).astype(F32)
    acs_c = jnp.dot(tril, da, precision=lax.Precision.HIGHEST, preferred_element_type=F32)
    dt_e = _expand_heads(dtc)
    acs_e = _expand_heads(acs_c)
    row = lax.broadcasted_iota(jnp.int32, (q, SSD_WIDTH), 0)
    jj = lax.broadcasted_iota(jnp.int32, (q, SSD_WIDTH), 1) & (SSD_HEADDIM - 1)
    acs_row = jnp.sum(jnp.where(jj == row, acs_e, 0.0), axis=0, keepdims=True)
    causal = row_time(jj) <= row_time(row)
    lmat = jnp.exp(jnp.where(causal, acs_e - acs_row, -jnp.inf))
    last = acs_e[q - 1:q, :]
    eacs = jnp.exp(acs_e)
    dec_end = jnp.exp(last - acs_e)
    cdec = jnp.exp(last)
    xdt = xs * dt_e
    xdt_b = xdt.astype(BF16)
    xd_b = (xdt * dec_end).astype(BF16)
    head_of_lane = lax.broadcasted_iota(jnp.int32, (q, gw), 1) >> 6

    ys = []
    for g in range(SSD_GROUPS):
        sl = slice(g * gw, (g + 1) * gw)
        bg = bm[:, g * SSD_STATE:(g + 1) * SSD_STATE]
        cg = cm[:, g * SSD_STATE:(g + 1) * SSD_STATE]
        b4 = jnp.concatenate([bg] * (SSD_HEADS // SSD_GROUPS), axis=0)
        cbt = lax.dot_general(cg, b4, (((1,), (1,)), ((), ())), preferred_element_type=F32)
        sg = (cbt * lmat[:, sl]).astype(BF16)
        xg = xdt_b[:, sl]
        xbd = jnp.concatenate(
            [jnp.where(head_of_lane == hh, xg, jnp.zeros_like(xg)) for hh in range(SSD_HEADS // SSD_GROUPS)],
            axis=0)
        ydiag = jnp.dot(sg, xbd, preferred_element_type=F32)
        htg = ht_ref[0, :, sl]
        yoff = jnp.dot(cg, htg.astype(BF16), preferred_element_type=F32)
        ys.append(ydiag + eacs[:, sl] * yoff)
        st = lax.dot_general(bg, xd_b[:, sl], (((0,), (0,)), ((), ())), preferred_element_type=F32)
        ht_ref[0, :, sl] = cdec[:, sl] * htg + st

    y = jnp.concatenate(ys, axis=1) + d_ref[...] * xs
    zcols = [jnp.concatenate(split_rows(zs_ref, s, base), axis=0) for s in range(SSD_WIDTH // LANES)]
    y = y * _silu(jnp.concatenate(zcols, axis=1))
    yn = _rms(y, nrm_ref[...]).astype(BF16)
    unperm = (row_time(lax.broadcasted_iota(jnp.int32, (q, q), 1))
              == lax.broadcasted_iota(jnp.int32, (q, q), 0)).astype(BF16)
    y_ref[0, base:base + q, :] = jnp.dot(unperm, yn, preferred_element_type=F32).astype(BF16)


def _ssd_kernel(xbc_ref, z_ref, dtr_ref, cw_ref, cb_ref, dtb_ref, alog_ref, d_ref, nrm_ref,
                conv0_ref, h0_ref, y_ref, convo_ref, ht_ref, pad_ref, zs_ref, *, valid, nsub):
    c = pl.program_id(1)
    rows = nsub * CHUNK
    last = rows - CHUNK + valid

    @pl.when(c == 0)
    def _():
        for s in range(CONV_DIM // LANES):
            pad_ref[s, 5:8, :] = conv0_ref[0, :, _slab(s)]
        ht_ref[0] = h0_ref[0]

    for s in range(CONV_DIM // LANES):
        pad_ref[s, 8:8 + rows, :] = xbc_ref[0, :, _slab(s)]
    for s in range(SSD_WIDTH // LANES):
        zs_ref[s] = z_ref[0, :, _slab(s)]
    for i in range(nsub):
        _ssd_chunk(dtr_ref, cw_ref, cb_ref, dtb_ref, alog_ref, d_ref, nrm_ref, y_ref, ht_ref, pad_ref, zs_ref,
                   base=i * CHUNK, valid=valid if i == nsub - 1 else CHUNK)
    for s in range(CONV_DIM // LANES):
        pad_ref[s, 5:8, :] = pad_ref[s, 8 + last - 3:8 + last, :]

    @pl.when(c == pl.num_programs(1) - 1)
    def _():
        for s in range(CONV_DIM // LANES):
            convo_ref[0, :, _slab(s)] = pad_ref[s, 5:8, :]


def _ssd(proj3, valid, cw, cb, dtb, alog, d_e, nrm, conv0, ht0):
    bsz, lp, _ = proj3.shape
    nsub = min(SSD_NSUB, lp // CHUNK)
    q = nsub * CHUNK
    kern = functools.partial(_ssd_kernel, valid=valid, nsub=nsub)
    row = lambda n: pl.BlockSpec((1, n), lambda b, c: (0, 0))
    return pl.pallas_call(
        kern,
        grid=(bsz, lp // q),
        in_specs=[
            pl.BlockSpec((1, q, CONV_DIM), lambda b, c: (b, c, 0)),
            pl.BlockSpec((1, q, SSD_WIDTH), lambda b, c: (b, c, CONV_DIM // SSD_WIDTH)),
            pl.BlockSpec((1, q, DT_PAD), lambda b, c: (b, c, (PROJ_COLS - DT_PAD) // DT_PAD)),
            pl.BlockSpec((CONV_W, CONV_DIM), lambda b, c: (0, 0)),
            row(CONV_DIM), row(DT_PAD), row(DT_PAD), row(SSD_WIDTH), row(SSD_WIDTH),
            pl.BlockSpec((1, CONV_W - 1, CONV_DIM), lambda b, c: (b, 0, 0)),
            pl.BlockSpec((1, SSD_STATE, SSD_WIDTH), lambda b, c: (b, 0, 0)),
        ],
        out_specs=[
            pl.BlockSpec((1, q, SSD_WIDTH), lambda b, c: (b, c, 0)),
            pl.BlockSpec((1, CONV_W - 1, CONV_DIM), lambda b, c: (b, 0, 0)),
            pl.BlockSpec((1, SSD_STATE, SSD_WIDTH), lambda b, c: (b, 0, 0)),
        ],
        out_shape=[
            jax.ShapeDtypeStruct((bsz, lp, SSD_WIDTH), BF16),
            jax.ShapeDtypeStruct((bsz, CONV_W - 1, CONV_DIM), F32),
            jax.ShapeDtypeStruct((bsz, SSD_STATE, SSD_WIDTH), F32),
        ],
        scratch_shapes=[pltpu.VMEM((CONV_DIM // LANES, 8 + q, LANES), F32),
                        pltpu.VMEM((SSD_WIDTH // LANES, q, LANES), F32)],
        compiler_params=_params("parallel", "arbitrary"),
        name="ssd",
    )(proj3, proj3, proj3, cw, cb, dtb, alog, d_e, nrm, conv0, ht0)


def _s5_prep_kernel(lre_ref, lim_ref, lstep_ref, bre_ref, bim_ref, lbr_ref, lbi_ref, bbr_ref, bbi_ref):
    lam_re = lre_ref[...]
    lam_im = lim_ref[...]
    step = jnp.exp(lstep_ref[...])
    mag = jnp.exp(lam_re * step)
    ang = lam_im * step
    lb_re = mag * jnp.cos(ang)
    lb_im = mag * jnp.sin(ang)
    den = lam_re * lam_re + lam_im * lam_im
    nr = lb_re - 1.0
    q_re = (nr * lam_re + lb_im * lam_im) / den
    q_im = (lb_im * lam_re - nr * lam_im) / den
    lbr_ref[...] = lb_re
    lbi_ref[...] = lb_im
    b_re = bre_ref[...]
    b_im = bim_ref[...]
    bbr_ref[...] = q_re[None] * b_re - q_im[None] * b_im
    bbi_ref[...] = q_re[None] * b_im + q_im[None] * b_re


def _s5_prep(lam_re, lam_im, log_step_e, b_re_t, b_im_t):
    gn = jax.ShapeDtypeStruct((S5_GROUPS, S5_STATE), F32)
    hgn = jax.ShapeDtypeStruct((S5_GROUP_CH, S5_GROUPS, S5_STATE), F32)
    return pl.pallas_call(_s5_prep_kernel, out_shape=[gn, gn, hgn, hgn], name="s5_prep")(
        lam_re, lam_im, log_step_e, b_re_t, b_im_t)


def _s5_kernel(u_ref, bdr_ref, bdi_ref, cdr_ref, cdi_ref, lbr_ref, lbi_ref, d_ref, wglu_ref, bglu_ref,
               nrm_ref, h0r_ref, h0i_ref, y_ref, hor_ref, hoi_ref, up_ref, yp_ref, *slab_refs,
               lc, pitch):
    c = pl.program_id(1)

    @pl.when(c == 0)
    def _():
        hor_ref[...] = h0r_ref[...]
        hoi_ref[...] = h0i_ref[...]
        up_ref[...] = jnp.zeros_like(up_ref)

    for b in range(S5_NB):
        up_ref[b * pitch:b * pitch + lc, :] = u_ref[b]
    u = up_ref[...]
    ub = u.astype(BF16)

    spb = S5_SLABS // S5_QB
    hr_refs, hi_refs = slab_refs[:S5_QB], slab_refs[S5_QB:]

    def input_stage(qb):
        uq = ub[:, qb * LANES:(qb + 1) * LANES]
        bu_r = jnp.dot(uq, bdr_ref[qb], preferred_element_type=F32)
        bu_i = jnp.dot(uq, bdi_ref[qb], preferred_element_type=F32)
        for k in range(spb):
            hr_refs[qb][k] = bu_r[:, k * LANES:(k + 1) * LANES]
            hi_refs[qb][k] = bu_i[:, k * LANES:(k + 1) * LANES]

    def scan_stage(qb):
        hr_ref, hi_ref = hr_refs[qb], hi_refs[qb]
        for k in range(spb):
            sl = slice((qb * spb + k) * LANES, (qb * spb + k + 1) * LANES)
            lr = jnp.broadcast_to(lbr_ref[:, sl], (S5_NB, LANES))
            li = jnp.broadcast_to(lbi_ref[:, sl], (S5_NB, LANES))
            h_r, h_i = hor_ref[:, sl], hoi_ref[:, sl]
            for t in range(lc):
                rows = pl.ds(t, S5_NB, stride=pitch)
                n_r = lr * h_r - li * h_i + hr_ref[k, rows, :]
                n_i = lr * h_i + li * h_r + hi_ref[k, rows, :]
                hr_ref[k, rows, :] = n_r
                hi_ref[k, rows, :] = n_i
                h_r, h_i = n_r, n_i
            hor_ref[:, sl] = h_r
            hoi_ref[:, sl] = h_i

    def output_stage(qb):
        ln = slice(qb * LANES, (qb + 1) * LANES)
        h_re = jnp.concatenate([hr_refs[qb][k] for k in range(spb)], axis=1).astype(BF16)
        h_im = jnp.concatenate([hi_refs[qb][k] for k in range(spb)], axis=1).astype(BF16)
        yq = (jnp.dot(h_re, cdr_ref[qb], preferred_element_type=F32)
              - jnp.dot(h_im, cdi_ref[qb], preferred_element_type=F32))
        yp_ref[:, ln] = yq + d_ref[:, ln] * u[:, ln]

    for step in range(S5_QB + 2):
        if step < S5_QB:
            input_stage(step)
        if 0 <= step - 1 < S5_QB:
            scan_stage(step - 1)
        if 0 <= step - 2 < S5_QB:
            output_stage(step - 2)

    y5 = yp_ref[...]
    g = 0.5 * y5 * (1.0 + lax.erf(y5 * (0.5 ** 0.5)))
    gl = jnp.dot(g.astype(BF16), wglu_ref[...], preferred_element_type=F32) + bglu_ref[...]
    yp_ref[...] = _rms(g * jax.nn.sigmoid(gl), nrm_ref[...])
    for b in range(S5_NB):
        y_ref[b] = yp_ref[b * pitch:b * pitch + lc, :].astype(BF16)


def _s5(proj3, bdr, bdi, cdr, cdi, lbr, lbi, d5, wglu, bglu, nrm, h0r, h0i):
    bsz, l, _ = proj3.shape
    lc = min(S5_LC, l)
    pitch = lc + 4
    mp = S5_NB * pitch
    kern = functools.partial(_s5_kernel, lc=lc, pitch=pitch)
    ns = S5_GROUPS * S5_STATE
    full = lambda *shape: pl.BlockSpec(shape, lambda b, c: (0,) * len(shape))
    state = pl.BlockSpec((S5_NB, ns), lambda b, c: (b, 0))
    return pl.pallas_call(
        kern,
        grid=(bsz // S5_NB, l // lc),
        in_specs=[
            pl.BlockSpec((S5_NB, lc, S5_WIDTH), lambda b, c: (b, c, (CONV_DIM + SSD_WIDTH) // S5_WIDTH)),
            full(S5_QB, LANES, 4 * LANES), full(S5_QB, LANES, 4 * LANES),
            full(S5_QB, 4 * LANES, LANES), full(S5_QB, 4 * LANES, LANES),
            full(1, ns), full(1, ns), full(1, S5_WIDTH),
            full(S5_WIDTH, S5_WIDTH), full(1, S5_WIDTH), full(1, S5_WIDTH),
            state, state,
        ],
        out_specs=[pl.BlockSpec((S5_NB, lc, S5_WIDTH), lambda b, c: (b, c, 0)), state, state],
        out_shape=[
            jax.ShapeDtypeStruct((bsz, l, S5_WIDTH), BF16),
            jax.ShapeDtypeStruct((bsz, ns), F32),
            jax.ShapeDtypeStruct((bsz, ns), F32),
        ],
        scratch_shapes=[
            pltpu.VMEM((mp, S5_WIDTH), F32),
            pltpu.VMEM((mp, S5_WIDTH), F32),
        ] + [pltpu.VMEM((S5_SLABS // S5_QB, mp, LANES), F32)] * (2 * S5_QB),
        compiler_params=_params("parallel", "arbitrary"),
        name="s5",
    )(proj3, bdr, bdi, cdr, cdi, lbr, lbi, d5, wglu, bglu, nrm, h0r, h0i)


def _outproj_kernel(x_ref, ya_ref, yb_ref, w_ref, o_ref):
    o_ref[...] = (x_ref[...]
                  + jnp.dot(ya_ref[...], w_ref[:SSD_WIDTH, :], preferred_element_type=F32)
                  + jnp.dot(yb_ref[...], w_ref[SSD_WIDTH:, :], preferred_element_type=F32))


def _out_proj(x, ya, yb, w):
    t = x.shape[0]
    tm = min(OUT_TM, t)
    return pl.pallas_call(
        _outproj_kernel,
        grid=(t // tm,),
        in_specs=[
            pl.BlockSpec((tm, D_MODEL), lambda i: (i, 0)),
            pl.BlockSpec((tm, SSD_WIDTH), lambda i: (i, 0)),
            pl.BlockSpec((tm, S5_WIDTH), lambda i: (i, 0)),
            pl.BlockSpec((D_MODEL, D_MODEL), lambda i: (0, 0)),
        ],
        out_specs=pl.BlockSpec((tm, D_MODEL), lambda i: (i, 0)),
        out_shape=jax.ShapeDtypeStruct((t, D_MODEL), F32),
        compiler_params=_params("parallel"),
        name="out_proj",
    )(x, ya, yb, w)


def _prepare(p):
    w = {}
    pad_ff = D_FF_PAD - D_FF
    for n in ("1", "2"):
        w["norm_ffn" + n] = p["norm_ffn" + n]
        w["wg" + n] = jnp.pad(p["w_ffn%s_gate" % n][0], ((0, 0), (0, pad_ff))).astype(BF16)
        w["wu" + n] = jnp.pad(p["w_ffn%s_up" % n][0], ((0, 0), (0, pad_ff))).astype(BF16)
        w["wd" + n] = jnp.pad(p["w_ffn%s_down" % n][0], ((0, pad_ff), (0, 0))).astype(BF16)
    w_in = p["w_in"][0]
    c0, c1, c2 = SSD_WIDTH, SSD_WIDTH + CONV_DIM, SSD_WIDTH + CONV_DIM + SSD_HEADS
    w["w_in"] = jnp.concatenate(
        [w_in[:, c0:c1], w_in[:, :c0], w_in[:, c2:],
         jnp.pad(w_in[:, c1:c2], ((0, 0), (0, DT_PAD - SSD_HEADS)))], axis=1).astype(BF16)
    w["norm_mix"] = p["norm_mix"]
    w["conv_w"] = p["conv_w"][0]
    w["conv_b"] = p["conv_b"]
    w["dt_bias"] = jnp.pad(p["dt_bias"], ((0, 0), (0, DT_PAD - SSD_HEADS)))
    w["a_log"] = jnp.pad(p["a_log"], ((0, 0), (0, DT_PAD - SSD_HEADS)))
    w["d_ssd"] = jnp.repeat(p["d_ssd"][0], SSD_HEADDIM)[None]
    w["norm_ssd"] = p["norm_ssd"]

    lbr, lbi, bbr, bbi = _s5_prep(
        p["s5_lambda_re"][0], p["s5_lambda_im"][0],
        jnp.broadcast_to(p["s5_log_step"][0][:, None], (S5_GROUPS, S5_STATE)),
        jnp.transpose(p["s5_b_re"][0], (2, 0, 1)), jnp.transpose(p["s5_b_im"][0], (2, 0, 1)))
    eye = jnp.eye(S5_QB, dtype=F32)
    gl = S5_GROUPS // S5_QB

    def bd(bb):
        t = jnp.einsum("hqgn,gk->qghkn", bb.reshape(S5_GROUP_CH, S5_QB, gl, S5_STATE), eye)
        return t.reshape(S5_QB, LANES, gl * S5_STATE).astype(BF16)

    def cd(cc):
        t = jnp.einsum("qghn,gk->qgnkh", cc.reshape(S5_QB, gl, S5_GROUP_CH, S5_STATE), eye)
        return t.reshape(S5_QB, gl * S5_STATE, LANES).astype(BF16)

    w["bdr"], w["bdi"] = bd(bbr), bd(bbi)
    w["cdr"], w["cdi"] = cd(p["s5_c_re"][0]), cd(p["s5_c_im"][0])
    w["lbr"] = lbr.reshape(1, -1)
    w["lbi"] = lbi.reshape(1, -1)
    w["s5_d"] = p["s5_d"][0].reshape(1, -1)
    w["w_glu"] = p["w_glu"][0].astype(BF16)
    w["b_glu"] = p["b_glu"]
    w["norm_s5"] = p["norm_s5"]
    w["w_out"] = p["w_out"][0].astype(BF16)
    w["norm_final"] = p["norm_final"][None]
    return w


def _trunk(x, conv0, ssd0, s5r0, s5i0, w):
    bsz, l, _ = x.shape
    t = bsz * l
    x1 = _ffn(x.reshape(t, D_MODEL), w["norm_ffn1"], w["wg1"], w["wu1"], w["wd1"])
    proj = _in_proj(x1, w["norm_mix"], w["w_in"]).reshape(bsz, l, PROJ_COLS)

    if l % CHUNK == 0:
        proj_ssd, valid = proj, CHUNK
    else:
        assert l < CHUNK
        proj_ssd, valid = jnp.pad(proj, ((0, 0), (0, CHUNK - l), (0, 0))), l
    ht0 = jnp.swapaxes(ssd0.reshape(bsz, SSD_WIDTH, SSD_STATE), 1, 2)
    y_ssd, conv_new, ht_new = _ssd(proj_ssd, valid, w["conv_w"], w["conv_b"], w["dt_bias"], w["a_log"],
                                   w["d_ssd"], w["norm_ssd"], conv0, ht0)
    ssd_new = jnp.swapaxes(ht_new, 1, 2).reshape(bsz, SSD_HEADS, SSD_HEADDIM, SSD_STATE)

    y_s5, s5r_new, s5i_new = _s5(proj, w["bdr"], w["bdi"], w["cdr"], w["cdi"], w["lbr"], w["lbi"],
                                 w["s5_d"], w["w_glu"], w["b_glu"], w["norm_s5"],
                                 s5r0.reshape(bsz, -1), s5i0.reshape(bsz, -1))

    x2 = _out_proj(x1, y_ssd[:, :l].reshape(t, SSD_WIDTH), y_s5.reshape(t, S5_WIDTH), w["w_out"])
    y = _ffn(x2, w["norm_ffn2"], w["wg2"], w["wu2"], w["wd2"], final_g=w["norm_final"])
    return (y.reshape(bsz, l, D_MODEL), conv_new[None], ssd_new[None],
            s5r_new.reshape(1, bsz, S5_GROUPS, S5_STATE), s5i_new.reshape(1, bsz, S5_GROUPS, S5_STATE))


def kernel(x_prompt, x_sample, cache_conv, state_ssd, state_s5_re, state_s5_im, norm_ffn1, w_ffn1_gate, w_ffn1_up, w_ffn1_down, norm_mix, w_in, conv_w, conv_b, dt_bias, a_log, d_ssd, norm_ssd, s5_lambda_re, s5_lambda_im, s5_log_step, s5_b_re, s5_b_im, s5_c_re, s5_c_im, s5_d, w_glu, b_glu, norm_s5, w_out, norm_ffn2, w_ffn2_gate, w_ffn2_up, w_ffn2_down, norm_final):
    p = dict(norm_ffn1=norm_ffn1, w_ffn1_gate=w_ffn1_gate, w_ffn1_up=w_ffn1_up, w_ffn1_down=w_ffn1_down,
             norm_mix=norm_mix, w_in=w_in, conv_w=conv_w, conv_b=conv_b, dt_bias=dt_bias, a_log=a_log,
             d_ssd=d_ssd, norm_ssd=norm_ssd, s5_lambda_re=s5_lambda_re, s5_lambda_im=s5_lambda_im,
             s5_log_step=s5_log_step, s5_b_re=s5_b_re, s5_b_im=s5_b_im, s5_c_re=s5_c_re, s5_c_im=s5_c_im,
             s5_d=s5_d, w_glu=w_glu, b_glu=b_glu, norm_s5=norm_s5, w_out=w_out, norm_ffn2=norm_ffn2,
             w_ffn2_gate=w_ffn2_gate, w_ffn2_up=w_ffn2_up, w_ffn2_down=w_ffn2_down, norm_final=norm_final)
    w = _prepare(p)
    bsz = x_prompt.shape[0]
    zeros = lambda *s: jnp.zeros(s, F32)
    out_p = _trunk(x_prompt, zeros(bsz, CONV_W - 1, CONV_DIM),
                   zeros(bsz, SSD_HEADS, SSD_HEADDIM, SSD_STATE),
                   zeros(bsz, S5_GROUPS, S5_STATE), zeros(bsz, S5_GROUPS, S5_STATE), w)
    out_s = _trunk(x_sample, cache_conv[0], state_ssd[0], state_s5_re[0], state_s5_im[0], w)
    return (out_p[0], out_s[0]) + out_p[1:] + out_s[1:]
```

```python
import functools

import jax
import jax.numpy as jnp
from jax import lax
from jax.experimental import pallas as pl
from jax.experimental.pallas import tpu as pltpu

F32 = jnp.float32
BF16 = jnp.bfloat16
EPS = 1e-6

D_MODEL = 2048
D_FF = 5504
SSD_WIDTH = 1024
SSD_HEADDIM = 64
SSD_HEADS = 16
SSD_GROUPS = 4
SSD_STATE = 128
CONV_W = 4
CONV_DIM = SSD_WIDTH + 2 * SSD_GROUPS * SSD_STATE
S5_WIDTH = 1024
S5_GROUP_CH = 16
S5_GROUPS = 64
S5_STATE = 64
CHUNK = 64
SSD_NSUB = 2

LANES = 128
D_FF_PAD = 5632
FFN_TM = 1024
FFN_TF = 512
DT_PAD = LANES
PROJ_COLS = CONV_DIM + SSD_WIDTH + S5_WIDTH + DT_PAD
PROJ_TM = 512
OUT_TM = 512
S5_NB = 8
S5_LC = 64
S5_SLABS = S5_GROUPS * S5_STATE // LANES
S5_QB = S5_WIDTH // LANES
VMEM_LIMIT = 56 * 1024 * 1024


def _rms(x, g):
    ms = jnp.mean(x * x, axis=-1, keepdims=True)
    return x * lax.rsqrt(ms + EPS) * g


def _silu(x):
    return x * jax.nn.sigmoid(x)


def _params(*sem):
    return pltpu.CompilerParams(dimension_semantics=sem, vmem_limit_bytes=VMEM_LIMIT)


def _ffn_kernel(x_ref, g_ref, wg_ref, wu_ref, wd_ref, gf_ref, o_ref, h_ref, *, final_norm):
    j = pl.program_id(1)

    @pl.when(j == 0)
    def _():
        x = x_ref[...]
        h_ref[...] = _rms(x, g_ref[...]).astype(BF16)
        o_ref[...] = x

    h = h_ref[...]
    gate = jnp.dot(h, wg_ref[...], preferred_element_type=F32)
    up = jnp.dot(h, wu_ref[...], preferred_element_type=F32)
    a = (0.5 * (_silu(gate) * up)).astype(BF16)
    o_ref[...] += jnp.dot(a, wd_ref[...], preferred_element_type=F32)

    if final_norm:
        @pl.when(j == pl.num_programs(1) - 1)
        def _():
            o_ref[...] = _rms(o_ref[...], gf_ref[...])


def _ffn(x, norm_g, wg, wu, wd, final_g=None):
    t = x.shape[0]
    tm = min(FFN_TM, t)
    kern = functools.partial(_ffn_kernel, final_norm=final_g is not None)
    gf = norm_g if final_g is None else final_g
    return pl.pallas_call(
        kern,
        grid=(t // tm, D_FF_PAD // FFN_TF),
        in_specs=[
            pl.BlockSpec((tm, D_MODEL), lambda i, j: (i, 0)),
            pl.BlockSpec((1, D_MODEL), lambda i, j: (0, 0)),
            pl.BlockSpec((D_MODEL, FFN_TF), lambda i, j: (0, j)),
            pl.BlockSpec((D_MODEL, FFN_TF), lambda i, j: (0, j)),
            pl.BlockSpec((FFN_TF, D_MODEL), lambda i, j: (j, 0)),
            pl.BlockSpec((1, D_MODEL), lambda i, j: (0, 0)),
        ],
        out_specs=pl.BlockSpec((tm, D_MODEL), lambda i, j: (i, 0)),
        out_shape=jax.ShapeDtypeStruct((t, D_MODEL), F32),
        scratch_shapes=[pltpu.VMEM((tm, D_MODEL), BF16)],
        compiler_params=_params("parallel", "arbitrary"),
        name="ffn",
    )(x, norm_g, wg, wu, wd, gf)


def _inproj_kernel(x_ref, g_ref, w_ref, o_ref):
    h = _rms(x_ref[...], g_ref[...]).astype(BF16)
    o_ref[...] = jnp.dot(h, w_ref[...], preferred_element_type=F32)


def _in_proj(x, norm_g, w):
    t = x.shape[0]
    tm = min(PROJ_TM, t)
    return pl.pallas_call(
        _inproj_kernel,
        grid=(t // tm,),
        in_specs=[
            pl.BlockSpec((tm, D_MODEL), lambda i: (i, 0)),
            pl.BlockSpec((1, D_MODEL), lambda i: (0, 0)),
            pl.BlockSpec((D_MODEL, PROJ_COLS), lambda i: (0, 0), pipeline_mode=pl.Buffered(1)),
        ],
        out_specs=pl.BlockSpec((tm, PROJ_COLS), lambda i: (i, 0)),
        out_shape=jax.ShapeDtypeStruct((t, PROJ_COLS), F32),
        compiler_params=_params("parallel"),
        name="in_proj",
    )(x, norm_g, w)


def _expand_heads(v):
    q = v.shape[0]
    lane = lax.broadcasted_iota(jnp.int32, (q, LANES), 1)
    blocks = []
    for k in range(SSD_WIDTH // LANES):
        a = jnp.broadcast_to(v[:, 2 * k:2 * k + 1], (q, LANES))
        b = jnp.broadcast_to(v[:, 2 * k + 1:2 * k + 2], (q, LANES))
        blocks.append(jnp.where(lane < SSD_HEADDIM, a, b))
    return jnp.concatenate(blocks, axis=1)


def _slab(s):
    return slice(s * LANES, (s + 1) * LANES)


def _ssd_chunk(dtr_ref, cw_ref, cb_ref, dtb_ref, alog_ref, d_ref, nrm_ref, y_ref, ht_ref, pad_ref, zs_ref,
               *, base, valid):
    q = CHUNK
    hq = q // 2
    gw = SSD_WIDTH // SSD_GROUPS
    slab = _slab
    row_time = lambda r: ((r & (hq - 1)) << 1) | (r >> (hq.bit_length() - 1))

    def split_rows(ref3, s, start):
        return [ref3[s, pl.ds(start + e, hq, stride=2), :] for e in range(2)]

    cw = cw_ref[...]
    cb = cb_ref[...]
    cols = []
    for s in range(CONV_DIM // LANES):
        taps = [split_rows(pad_ref, s, base + 5 + k) for k in range(CONV_W)]
        halves = []
        for e in range(2):
            acc = cb[:, slab(s)] + taps[0][e] * cw[0:1, slab(s)]
            for k in range(1, CONV_W):
                acc = acc + taps[k][e] * cw[k:k + 1, slab(s)]
            halves.append(acc)
        cols.append(jnp.concatenate(halves, axis=0))
    act = _silu(jnp.concatenate(cols, axis=1))
    xs = act[:, :SSD_WIDTH]
    bm = act[:, SSD_WIDTH:SSD_WIDTH + SSD_GROUPS * SSD_STATE].astype(BF16)
    cm = act[:, SSD_WIDTH + SSD_GROUPS * SSD_STATE:].astype(BF16)

    dt_raw = jnp.concatenate([dtr_ref[0, pl.ds(base + e, hq, stride=2), :] for e in range(2)], axis=0)
    dtc = jax.nn.softplus(dt_raw + dtb_ref[...])
    if valid < q:
        dtc = jnp.where(row_time(lax.broadcasted_iota(jnp.int32, dtc.shape, 0)) < valid, dtc, 0.0)
    da = dtc * (-jnp.exp(alog_ref[...]))
    tril = (row_time(lax.broadcasted_iota(jnp.int32, (q, q), 1))
            <= row_time(lax.broadcasted_iota(jnp.int32, (q, q), 0))).astype(F32)
    acs_c = jnp.dot(tril, da, precision=lax.Precision.HIGHEST, preferred_element_type=F32)
    dt_e = _expand_heads(dtc)
    acs_e = _expand_heads(acs_c)
    row = lax.broadcasted_iota(jnp.int32, (q, SSD_WIDTH), 0)
    jj = lax.broadcasted_iota(jnp.int32, (q, SSD_WIDTH), 1) & (SSD_HEADDIM - 1)
    acs_row = jnp.sum(jnp.where(jj == row, acs_e, 0.0), axis=0, keepdims=True)
    causal = row_time(jj) <= row_time(row)
    lmat = jnp.exp(jnp.where(causal, acs_e - acs_row, -jnp.inf))
    last = acs_e[q - 1:q, :]
    eacs = jnp.exp(acs_e)
    dec_end = jnp.exp(last - acs_e)
    cdec = jnp.exp(last)
    xdt = xs * dt_e
    xdt_b = xdt.astype(BF16)
    xd_b = (xdt * dec_end).astype(BF16)
    head_of_lane = lax.broadcasted_iota(jnp.int32, (q, gw), 1) >> 6

    ys = []
    for g in range(SSD_GROUPS):
        sl = slice(g * gw, (g + 1) * gw)
        bg = bm[:, g * SSD_STATE:(g + 1) * SSD_STATE]
        cg = cm[:, g * SSD_STATE:(g + 1) * SSD_STATE]
        b4 = jnp.concatenate([bg] * (SSD_HEADS // SSD_GROUPS), axis=0)
        cbt = lax.dot_general(cg, b4, (((1,), (1,)), ((), ())), preferred_element_type=F32)
        sg = (cbt * lmat[:, sl]).astype(BF16)
        xg = xdt_b[:, sl]
        xbd = jnp.concatenate(
            [jnp.where(head_of_lane == hh, xg, jnp.zeros_like(xg)) for hh in range(SSD_HEADS // SSD_GROUPS)],
            axis=0)
        ydiag = jnp.dot(sg, xbd, preferred_element_type=F32)
        htg = ht_ref[0, :, sl]
        yoff = jnp.dot(cg, htg.astype(BF16), preferred_element_type=F32)
        ys.append(ydiag + eacs[:, sl] * yoff)
        st = lax.dot_general(bg, xd_b[:, sl], (((0,), (0,)), ((), ())), preferred_element_type=F32)
        ht_ref[0, :, sl] = cdec[:, sl] * htg + st

    y = jnp.concatenate(ys, axis=1) + d_ref[...] * xs
    zcols = [jnp.concatenate(split_rows(zs_ref, s, base), axis=0) for s in range(SSD_WIDTH // LANES)]
    y = y * _silu(jnp.concatenate(zcols, axis=1))
    yn = _rms(y, nrm_ref[...]).astype(BF16)
    unperm = (row_time(lax.broadcasted_iota(jnp.int32, (q, q), 1))
              == lax.broadcasted_iota(jnp.int32, (q, q), 0)).astype(BF16)
    y_ref[0, base:base + q, :] = jnp.dot(unperm, yn, preferred_element_type=F32).astype(BF16)


def _ssd_kernel(xbc_ref, z_ref, dtr_ref, cw_ref, cb_ref, dtb_ref, alog_ref, d_ref, nrm_ref,
                conv0_ref, h0_ref, y_ref, convo_ref, ht_ref, pad_ref, zs_ref, *, valid, nsub):
    c = pl.program_id(1)
    rows = nsub * CHUNK
    last = rows - CHUNK + valid

    @pl.when(c == 0)
    def _():
        for s in range(CONV_DIM // LANES):
            pad_ref[s, 5:8, :] = conv0_ref[0, :, _slab(s)]
        ht_ref[0] = h0_ref[0]

    for s in range(CONV_DIM // LANES):
        pad_ref[s, 8:8 + rows, :] = xbc_ref[0, :, _slab(s)]
    for s in range(SSD_WIDTH // LANES):
        zs_ref[s] = z_ref[0, :, _slab(s)]
    for i in range(nsub):
        _ssd_chunk(dtr_ref, cw_ref, cb_ref, dtb_ref, alog_ref, d_ref, nrm_ref, y_ref, ht_ref, pad_ref, zs_ref,
                   base=i * CHUNK, valid=valid if i == nsub - 1 else CHUNK)
    for s in range(CONV_DIM // LANES):
        pad_ref[s, 5:8, :] = pad_ref[s, 8 + last - 3:8 + last, :]

    @pl.when(c == pl.num_programs(1) - 1)
    def _():
        for s in range(CONV_DIM // LANES):
            convo_ref[0, :, _slab(s)] = pad_ref[s, 5:8, :]


def _ssd(proj3, valid, cw, cb, dtb, alog, d_e, nrm, conv0, ht0):
    bsz, lp, _ = proj3.shape
    nsub = min(SSD_NSUB, lp // CHUNK)
    q = nsub * CHUNK
    kern = functools.partial(_ssd_kernel, valid=valid, nsub=nsub)
    row = lambda n: pl.BlockSpec((1, n), lambda b, c: (0, 0))
    return pl.pallas_call(
        kern,
        grid=(bsz, lp // q),
        in_specs=[
            pl.BlockSpec((1, q, CONV_DIM), lambda b, c: (b, c, 0)),
            pl.BlockSpec((1, q, SSD_WIDTH), lambda b, c: (b, c, CONV_DIM // SSD_WIDTH)),
            pl.BlockSpec((1, q, DT_PAD), lambda b, c: (b, c, (PROJ_COLS - DT_PAD) // DT_PAD)),
            pl.BlockSpec((CONV_W, CONV_DIM), lambda b, c: (0, 0)),
            row(CONV_DIM), row(DT_PAD), row(DT_PAD), row(SSD_WIDTH), row(SSD_WIDTH),
            pl.BlockSpec((1, CONV_W - 1, CONV_DIM), lambda b, c: (b, 0, 0)),
            pl.BlockSpec((1, SSD_STATE, SSD_WIDTH), lambda b, c: (b, 0, 0)),
        ],
        out_specs=[
            pl.BlockSpec((1, q, SSD_WIDTH), lambda b, c: (b, c, 0)),
            pl.BlockSpec((1, CONV_W - 1, CONV_DIM), lambda b, c: (b, 0, 0)),
            pl.BlockSpec((1, SSD_STATE, SSD_WIDTH), lambda b, c: (b, 0, 0)),
        ],
        out_shape=[
            jax.ShapeDtypeStruct((bsz, lp, SSD_WIDTH), BF16),
            jax.ShapeDtypeStruct((bsz, CONV_W - 1, CONV_DIM), F32),
            jax.ShapeDtypeStruct((bsz, SSD_STATE, SSD_WIDTH), F32),
        ],
        scratch_shapes=[pltpu.VMEM((CONV_DIM // LANES, 8 + q, LANES), F32),
                        pltpu.VMEM((SSD_WIDTH // LANES, q, LANES), F32)],
        compiler_params=_params("parallel", "arbitrary"),
        name="ssd",
    )(proj3, proj3, proj3, cw, cb, dtb, alog, d_e, nrm, conv0, ht0)


def _s5_prep_kernel(lre_ref, lim_ref, lstep_ref, bre_ref, bim_ref, lbr_ref, lbi_ref, bbr_ref, bbi_ref):
    lam_re = lre_ref[...]
    lam_im = lim_ref[...]
    step = jnp.exp(lstep_ref[...])
    mag = jnp.exp(lam_re * step)
    ang = lam_im * step
    lb_re = mag * jnp.cos(ang)
    lb_im = mag * jnp.sin(ang)
    den = lam_re * lam_re + lam_im * lam_im
    nr = lb_re - 1.0
    q_re = (nr * lam_re + lb_im * lam_im) / den
    q_im = (lb_im * lam_re - nr * lam_im) / den
    lbr_ref[...] = lb_re
    lbi_ref[...] = lb_im
    b_re = bre_ref[...]
    b_im = bim_ref[...]
    bbr_ref[...] = q_re[None] * b_re - q_im[None] * b_im
    bbi_ref[...] = q_re[None] * b_im + q_im[None] * b_re


def _s5_prep(lam_re, lam_im, log_step_e, b_re_t, b_im_t):
    gn = jax.ShapeDtypeStruct((S5_GROUPS, S5_STATE), F32)
    hgn = jax.ShapeDtypeStruct((S5_GROUP_CH, S5_GROUPS, S5_STATE), F32)
    return pl.pallas_call(_s5_prep_kernel, out_shape=[gn, gn, hgn, hgn], name="s5_prep")(
        lam_re, lam_im, log_step_e, b_re_t, b_im_t)


def _s5_kernel(u_ref, bdr_ref, bdi_ref, cdr_ref, cdi_ref, lbr_ref, lbi_ref, d_ref, wglu_ref, bglu_ref,
               nrm_ref, h0r_ref, h0i_ref, y_ref, hor_ref, hoi_ref, up_ref, yp_ref, *slab_refs,
               lc, pitch):
    c = pl.program_id(1)

    @pl.when(c == 0)
    def _():
        hor_ref[...] = h0r_ref[...]
        hoi_ref[...] = h0i_ref[...]
        up_ref[...] = jnp.zeros_like(up_ref)

    for b in range(S5_NB):
        up_ref[b * pitch:b * pitch + lc, :] = u_ref[b]
    u = up_ref[...]
    ub = u.astype(BF16)

    spb = S5_SLABS // S5_QB
    hr_refs, hi_refs = slab_refs[:S5_QB], slab_refs[S5_QB:]

    def input_stage(qb):
        uq = ub[:, qb * LANES:(qb + 1) * LANES]
        bu_r = jnp.dot(uq, bdr_ref[qb], preferred_element_type=F32)
        bu_i = jnp.dot(uq, bdi_ref[qb], preferred_element_type=F32)
        for k in range(spb):
            hr_refs[qb][k] = bu_r[:, k * LANES:(k + 1) * LANES]
            hi_refs[qb][k] = bu_i[:, k * LANES:(k + 1) * LANES]

    def scan_stage(qb):
        hr_ref, hi_ref = hr_refs[qb], hi_refs[qb]
        for k in range(spb):
            sl = slice((qb * spb + k) * LANES, (qb * spb + k + 1) * LANES)
            lr = jnp.broadcast_to(lbr_ref[:, sl], (S5_NB, LANES))
            li = jnp.broadcast_to(lbi_ref[:, sl], (S5_NB, LANES))
            h_r, h_i = hor_ref[:, sl], hoi_ref[:, sl]
            for t in range(lc):
                rows = pl.ds(t, S5_NB, stride=pitch)
                n_r = lr * h_r - li * h_i + hr_ref[k, rows, :]
                n_i = lr * h_i + li * h_r + hi_ref[k, rows, :]
                hr_ref[k, rows, :] = n_r
                hi_ref[k, rows, :] = n_i
                h_r, h_i = n_r, n_i
            hor_ref[:, sl] = h_r
            hoi_ref[:, sl] = h_i

    def output_stage(qb):
        ln = slice(qb * LANES, (qb + 1) * LANES)
        h_re = jnp.concatenate([hr_refs[qb][k] for k in range(spb)], axis=1).astype(BF16)
        h_im = jnp.concatenate([hi_refs[qb][k] for k in range(spb)], axis=1).astype(BF16)
        yq = (jnp.dot(h_re, cdr_ref[qb], preferred_element_type=F32)
              - jnp.dot(h_im, cdi_ref[qb], preferred_element_type=F32))
        yp_ref[:, ln] = yq + d_ref[:, ln] * u[:, ln]

    for step in range(S5_QB + 2):
        if step < S5_QB:
            input_stage(step)
        if 0 <= step - 1 < S5_QB:
            scan_stage(step - 1)
        if 0 <= step - 2 < S5_QB:
            output_stage(step - 2)

    y5 = yp_ref[...]
    g = 0.5 * y5 * (1.0 + lax.erf(y5 * (0.5 ** 0.5)))
    gl = jnp.dot(g.astype(BF16), wglu_ref[...], preferred_element_type=F32) + bglu_ref[...]
    yp_ref[...] = _rms(g * jax.nn.sigmoid(gl), nrm_ref[...])
    for b in range(S5_NB):
        y_ref[b] = yp_ref[b * pitch:b * pitch + lc, :].astype(BF16)


def _s5(proj3, bdr, bdi, cdr, cdi, lbr, lbi, d5, wglu, bglu, nrm, h0r, h0i):
    bsz, l, _ = proj3.shape
    lc = min(S5_LC, l)
    pitch = lc + 4
    mp = S5_NB * pitch
    kern = functools.partial(_s5_kernel, lc=lc, pitch=pitch)
    ns = S5_GROUPS * S5_STATE
    full = lambda *shape: pl.BlockSpec(shape, lambda b, c: (0,) * len(shape))
    state = pl.BlockSpec((S5_NB, ns), lambda b, c: (b, 0))
    return pl.pallas_call(
        kern,
        grid=(bsz // S5_NB, l // lc),
        in_specs=[
            pl.BlockSpec((S5_NB, lc, S5_WIDTH), lambda b, c: (b, c, (CONV_DIM + SSD_WIDTH) // S5_WIDTH)),
            full(S5_QB, LANES, 4 * LANES), full(S5_QB, LANES, 4 * LANES),
            full(S5_QB, 4 * LANES, LANES), full(S5_QB, 4 * LANES, LANES),
            full(1, ns), full(1, ns), full(1, S5_WIDTH),
            full(S5_WIDTH, S5_WIDTH), full(1, S5_WIDTH), full(1, S5_WIDTH),
            state, state,
        ],
        out_specs=[pl.BlockSpec((S5_NB, lc, S5_WIDTH), lambda b, c: (b, c, 0)), state, state],
        out_shape=[
            jax.ShapeDtypeStruct((bsz, l, S5_WIDTH), BF16),
            jax.ShapeDtypeStruct((bsz, ns), F32),
            jax.ShapeDtypeStruct((bsz, ns), F32),
        ],
        scratch_shapes=[
            pltpu.VMEM((mp, S5_WIDTH), F32),
            pltpu.VMEM((mp, S5_WIDTH), F32),
        ] + [pltpu.VMEM((S5_SLABS // S5_QB, mp, LANES), F32)] * (2 * S5_QB),
        compiler_params=_params("parallel", "arbitrary"),
        name="s5",
    )(proj3, bdr, bdi, cdr, cdi, lbr, lbi, d5, wglu, bglu, nrm, h0r, h0i)


def _outproj_kernel(x_ref, ya_ref, yb_ref, w_ref, o_ref):
    o_ref[...] = (x_ref[...]
                  + jnp.dot(ya_ref[...], w_ref[:SSD_WIDTH, :], preferred_element_type=F32)
                  + jnp.dot(yb_ref[...], w_ref[SSD_WIDTH:, :], preferred_element_type=F32))


def _out_proj(x, ya, yb, w):
    t = x.shape[0]
    tm = min(OUT_TM, t)
    return pl.pallas_call(
        _outproj_kernel,
        grid=(t // tm,),
        in_specs=[
            pl.BlockSpec((tm, D_MODEL), lambda i: (i, 0)),
            pl.BlockSpec((tm, SSD_WIDTH), lambda i: (i, 0)),
            pl.BlockSpec((tm, S5_WIDTH), lambda i: (i, 0)),
            pl.BlockSpec((D_MODEL, D_MODEL), lambda i: (0, 0)),
        ],
        out_specs=pl.BlockSpec((tm, D_MODEL), lambda i: (i, 0)),
        out_shape=jax.ShapeDtypeStruct((t, D_MODEL), F32),
        compiler_params=_params("parallel"),
        name="out_proj",
    )(x, ya, yb, w)


def _prepare(p):
    w = {}
    pad_ff = D_FF_PAD - D_FF
    for n in ("1", "2"):
        w["norm_ffn" + n] = p["norm_ffn" + n]
        zc = jnp.zeros((D_MODEL, pad_ff), BF16)
        w["wg" + n] = jnp.concatenate([p["w_ffn%s_gate" % n][0].astype(BF16), zc], axis=1)
        w["wu" + n] = jnp.concatenate([p["w_ffn%s_up" % n][0].astype(BF16), zc], axis=1)
        w["wd" + n] = jnp.concatenate([p["w_ffn%s_down" % n][0].astype(BF16), zc.T], axis=0)
    w_in = p["w_in"][0]
    c0, c1, c2 = SSD_WIDTH, SSD_WIDTH + CONV_DIM, SSD_WIDTH + CONV_DIM + SSD_HEADS
    w["w_in"] = jnp.concatenate(
        [w_in[:, c0:c1], w_in[:, :c0], w_in[:, c2:],
         jnp.pad(w_in[:, c1:c2], ((0, 0), (0, DT_PAD - SSD_HEADS)))], axis=1).astype(BF16)
    w["norm_mix"] = p["norm_mix"]
    w["conv_w"] = p["conv_w"][0]
    w["conv_b"] = p["conv_b"]
    w["dt_bias"] = jnp.pad(p["dt_bias"], ((0, 0), (0, DT_PAD - SSD_HEADS)))
    w["a_log"] = jnp.pad(p["a_log"], ((0, 0), (0, DT_PAD - SSD_HEADS)))
    w["d_ssd"] = jnp.repeat(p["d_ssd"][0], SSD_HEADDIM)[None]
    w["norm_ssd"] = p["norm_ssd"]

    lbr, lbi, bbr, bbi = _s5_prep(
        p["s5_lambda_re"][0], p["s5_lambda_im"][0],
        jnp.broadcast_to(p["s5_log_step"][0][:, None], (S5_GROUPS, S5_STATE)),
        jnp.transpose(p["s5_b_re"][0], (2, 0, 1)), jnp.transpose(p["s5_b_im"][0], (2, 0, 1)))
    eye = jnp.eye(S5_QB, dtype=F32)
    gl = S5_GROUPS // S5_QB

    def bd(bb):
        t = jnp.einsum("hqgn,gk->qghkn", bb.reshape(S5_GROUP_CH, S5_QB, gl, S5_STATE), eye)
        return t.reshape(S5_QB, LANES, gl * S5_STATE).astype(BF16)

    def cd(cc):
        t = jnp.einsum("qghn,gk->qgnkh", cc.reshape(S5_QB, gl, S5_GROUP_CH, S5_STATE), eye)
        return t.reshape(S5_QB, gl * S5_STATE, LANES).astype(BF16)

    w["bdr"], w["bdi"] = bd(bbr), bd(bbi)
    w["cdr"], w["cdi"] = cd(p["s5_c_re"][0]), cd(p["s5_c_im"][0])
    w["lbr"] = lbr.reshape(1, -1)
    w["lbi"] = lbi.reshape(1, -1)
    w["s5_d"] = p["s5_d"][0].reshape(1, -1)
    w["w_glu"] = p["w_glu"][0].astype(BF16)
    w["b_glu"] = p["b_glu"]
    w["norm_s5"] = p["norm_s5"]
    w["w_out"] = p["w_out"][0].astype(BF16)
    w["norm_final"] = p["norm_final"][None]
    return w


def _trunk(x, conv0, ssd0, s5r0, s5i0, w):
    bsz, l, _ = x.shape
    t = bsz * l
    x1 = _ffn(x.reshape(t, D_MODEL), w["norm_ffn1"], w["wg1"], w["wu1"], w["wd1"])
    proj = _in_proj(x1, w["norm_mix"], w["w_in"]).reshape(bsz, l, PROJ_COLS)

    if l % CHUNK == 0:
        proj_ssd, valid = proj, CHUNK
    else:
        assert l < CHUNK
        proj_ssd, valid = jnp.pad(proj, ((0, 0), (0, CHUNK - l), (0, 0))), l
    ht0 = jnp.swapaxes(ssd0.reshape(bsz, SSD_WIDTH, SSD_STATE), 1, 2)
    y_ssd, conv_new, ht_new = _ssd(proj_ssd, valid, w["conv_w"], w["conv_b"], w["dt_bias"], w["a_log"],
                                   w["d_ssd"], w["norm_ssd"], conv0, ht0)
    ssd_new = jnp.swapaxes(ht_new, 1, 2).reshape(bsz, SSD_HEADS, SSD_HEADDIM, SSD_STATE)

    y_s5, s5r_new, s5i_new = _s5(proj, w["bdr"], w["bdi"], w["cdr"], w["cdi"], w["lbr"], w["lbi"],
                                 w["s5_d"], w["w_glu"], w["b_glu"], w["norm_s5"],
                                 s5r0.reshape(bsz, -1), s5i0.reshape(bsz, -1))

    x2 = _out_proj(x1, y_ssd[:, :l].reshape(t, SSD_WIDTH), y_s5.reshape(t, S5_WIDTH), w["w_out"])
    y = _ffn(x2, w["norm_ffn2"], w["wg2"], w["wu2"], w["wd2"], final_g=w["norm_final"])
    return (y.reshape(bsz, l, D_MODEL), conv_new[None], ssd_new[None],
            s5r_new.reshape(1, bsz, S5_GROUPS, S5_STATE), s5i_new.reshape(1, bsz, S5_GROUPS, S5_STATE))


def kernel(x_prompt, x_sample, cache_conv, state_ssd, state_s5_re, state_s5_im, norm_ffn1, w_ffn1_gate, w_ffn1_up, w_ffn1_down, norm_mix, w_in, conv_w, conv_b, dt_bias, a_log, d_ssd, norm_ssd, s5_lambda_re, s5_lambda_im, s5_log_step, s5_b_re, s5_b_im, s5_c_re, s5_c_im, s5_d, w_glu, b_glu, norm_s5, w_out, norm_ffn2, w_ffn2_gate, w_ffn2_up, w_ffn2_down, norm_final):
    p = dict(norm_ffn1=norm_ffn1, w_ffn1_gate=w_ffn1_gate, w_ffn1_up=w_ffn1_up, w_ffn1_down=w_ffn1_down,
             norm_mix=norm_mix, w_in=w_in, conv_w=conv_w, conv_b=conv_b, dt_bias=dt_bias, a_log=a_log,
             d_ssd=d_ssd, norm_ssd=norm_ssd, s5_lambda_re=s5_lambda_re, s5_lambda_im=s5_lambda_im,
             s5_log_step=s5_log_step, s5_b_re=s5_b_re, s5_b_im=s5_b_im, s5_c_re=s5_c_re, s5_c_im=s5_c_im,
             s5_d=s5_d, w_glu=w_glu, b_glu=b_glu, norm_s5=norm_s5, w_out=w_out, norm_ffn2=norm_ffn2,
             w_ffn2_gate=w_ffn2_gate, w_ffn2_up=w_ffn2_up, w_ffn2_down=w_ffn2_down, norm_final=norm_final)
    w = _prepare(p)
    bsz = x_prompt.shape[0]
    zeros = lambda *s: jnp.zeros(s, F32)
    out_p = _trunk(x_prompt, zeros(bsz, CONV_W - 1, CONV_DIM),
                   zeros(bsz, SSD_HEADS, SSD_HEADDIM, SSD_STATE),
                   zeros(bsz, S5_GROUPS, S5_STATE), zeros(bsz, S5_GROUPS, S5_STATE), w)
    out_s = _trunk(x_sample, cache_conv[0], state_ssd[0], state_s5_re[0], state_s5_im[0], w)
    return (out_p[0], out_s[0]) + out_p[1:] + out_s[1:]
```

```python
import functools

import jax
import jax.numpy as jnp
from jax import lax
from jax.experimental import pallas as pl
from jax.experimental.pallas import tpu as pltpu

F32 = jnp.float32
BF16 = jnp.bfloat16
EPS = 1e-6

D_MODEL = 2048
D_FF = 5504
SSD_WIDTH = 1024
SSD_HEADDIM = 64
SSD_HEADS = 16
SSD_GROUPS = 4
SSD_STATE = 128
CONV_W = 4
CONV_DIM = SSD_WIDTH + 2 * SSD_GROUPS * SSD_STATE
S5_WIDTH = 1024
S5_GROUP_CH = 16
S5_GROUPS = 64
S5_STATE = 64
CHUNK = 64
SSD_NSUB = 4

LANES = 128
D_FF_PAD = 5632
FFN_TM = 1024
FFN_TF = 512
DT_PAD = LANES
PROJ_COLS = CONV_DIM + SSD_WIDTH + S5_WIDTH + DT_PAD
PROJ_TM = 512
OUT_TM = 512
S5_NB = 8
S5_LC = 64
S5_SLABS = S5_GROUPS * S5_STATE // LANES
S5_QB = S5_WIDTH // LANES
VMEM_LIMIT = 56 * 1024 * 1024


def _rms(x, g):
    ms = jnp.mean(x * x, axis=-1, keepdims=True)
    return x * lax.rsqrt(ms + EPS) * g


def _silu(x):
    return x * jax.nn.sigmoid(x)


def _params(*sem):
    return pltpu.CompilerParams(dimension_semantics=sem, vmem_limit_bytes=VMEM_LIMIT)


def _ffn_kernel(x_ref, g_ref, wg_ref, wu_ref, wd_ref, gf_ref, o_ref, h_ref, *, final_norm):
    j = pl.program_id(1)

    @pl.when(j == 0)
    def _():
        x = x_ref[...]
        h_ref[...] = _rms(x, g_ref[...]).astype(BF16)
        o_ref[...] = x

    h = h_ref[...]
    gate = jnp.dot(h, wg_ref[...], preferred_element_type=F32)
    up = jnp.dot(h, wu_ref[...], preferred_element_type=F32)
    a = (0.5 * (_silu(gate) * up)).astype(BF16)
    o_ref[...] += jnp.dot(a, wd_ref[...], preferred_element_type=F32)

    if final_norm:
        @pl.when(j == pl.num_programs(1) - 1)
        def _():
            o_ref[...] = _rms(o_ref[...], gf_ref[...])


def _ffn(x, norm_g, wg, wu, wd, final_g=None):
    t = x.shape[0]
    tm = min(FFN_TM, t)
    kern = functools.partial(_ffn_kernel, final_norm=final_g is not None)
    gf = norm_g if final_g is None else final_g
    return pl.pallas_call(
        kern,
        grid=(t // tm, D_FF_PAD // FFN_TF),
        in_specs=[
            pl.BlockSpec((tm, D_MODEL), lambda i, j: (i, 0)),
            pl.BlockSpec((1, D_MODEL), lambda i, j: (0, 0)),
            pl.BlockSpec((D_MODEL, FFN_TF), lambda i, j: (0, j)),
            pl.BlockSpec((D_MODEL, FFN_TF), lambda i, j: (0, j)),
            pl.BlockSpec((FFN_TF, D_MODEL), lambda i, j: (j, 0)),
            pl.BlockSpec((1, D_MODEL), lambda i, j: (0, 0)),
        ],
        out_specs=pl.BlockSpec((tm, D_MODEL), lambda i, j: (i, 0)),
        out_shape=jax.ShapeDtypeStruct((t, D_MODEL), F32),
        scratch_shapes=[pltpu.VMEM((tm, D_MODEL), BF16)],
        compiler_params=_params("parallel", "arbitrary"),
        name="ffn",
    )(x, norm_g, wg, wu, wd, gf)


def _inproj_kernel(x_ref, g_ref, w_ref, o_ref):
    h = _rms(x_ref[...], g_ref[...]).astype(BF16)
    o_ref[...] = jnp.dot(h, w_ref[...], preferred_element_type=F32)


def _in_proj(x, norm_g, w):
    t = x.shape[0]
    tm = min(PROJ_TM, t)
    return pl.pallas_call(
        _inproj_kernel,
        grid=(t // tm,),
        in_specs=[
            pl.BlockSpec((tm, D_MODEL), lambda i: (i, 0)),
            pl.BlockSpec((1, D_MODEL), lambda i: (0, 0)),
            pl.BlockSpec((D_MODEL, PROJ_COLS), lambda i: (0, 0), pipeline_mode=pl.Buffered(1)),
        ],
        out_specs=pl.BlockSpec((tm, PROJ_COLS), lambda i: (i, 0)),
        out_shape=jax.ShapeDtypeStruct((t, PROJ_COLS), F32),
        compiler_params=_params("parallel"),
        name="in_proj",
    )(x, norm_g, w)


def _expand_heads(v):
    q = v.shape[0]
    lane = lax.broadcasted_iota(jnp.int32, (q, LANES), 1)
    blocks = []
    for k in range(SSD_WIDTH // LANES):
        a = jnp.broadcast_to(v[:, 2 * k:2 * k + 1], (q, LANES))
        b = jnp.broadcast_to(v[:, 2 * k + 1:2 * k + 2], (q, LANES))
        blocks.append(jnp.where(lane < SSD_HEADDIM, a, b))
    return jnp.concatenate(blocks, axis=1)


def _slab(s):
    return slice(s * LANES, (s + 1) * LANES)


def _ssd_chunk_local(dtr_ref, cw_ref, cb_ref, dtb_ref, alog_ref, d_ref, pad_ref, zs_ref, *, base, valid):
    q = CHUNK
    hq = q // 2
    gw = SSD_WIDTH // SSD_GROUPS
    slab = _slab
    row_time = lambda r: ((r & (hq - 1)) << 1) | (r >> (hq.bit_length() - 1))

    def split_rows(ref3, s, start):
        return [ref3[s, pl.ds(start + e, hq, stride=2), :] for e in range(2)]

    cw = cw_ref[...]
    cb = cb_ref[...]
    cols = []
    for s in range(CONV_DIM // LANES):
        taps = [split_rows(pad_ref, s, base + 5 + k) for k in range(CONV_W)]
        halves = []
        for e in range(2):
            acc = cb[:, slab(s)] + taps[0][e] * cw[0:1, slab(s)]
            for k in range(1, CONV_W):
                acc = acc + taps[k][e] * cw[k:k + 1, slab(s)]
            halves.append(acc)
        cols.append(jnp.concatenate(halves, axis=0))
    act = _silu(jnp.concatenate(cols, axis=1))
    xs = act[:, :SSD_WIDTH]
    bm = act[:, SSD_WIDTH:SSD_WIDTH + SSD_GROUPS * SSD_STATE].astype(BF16)
    cm = act[:, SSD_WIDTH + SSD_GROUPS * SSD_STATE:].astype(BF16)

    dt_raw = jnp.concatenate([dtr_ref[0, pl.ds(base + e, hq, stride=2), :] for e in range(2)], axis=0)
    dtc = jax.nn.softplus(dt_raw + dtb_ref[...])
    if valid < q:
        dtc = jnp.where(row_time(lax.broadcasted_iota(jnp.int32, dtc.shape, 0)) < valid, dtc, 0.0)
    da = dtc * (-jnp.exp(alog_ref[...]))
    tril = (row_time(lax.broadcasted_iota(jnp.int32, (q, q), 1))
            <= row_time(lax.broadcasted_iota(jnp.int32, (q, q), 0))).astype(F32)
    acs_c = jnp.dot(tril, da, precision=lax.Precision.HIGHEST, preferred_element_type=F32)
    dt_e = _expand_heads(dtc)
    acs_e = _expand_heads(acs_c)
    row = lax.broadcasted_iota(jnp.int32, (q, SSD_WIDTH), 0)
    jj = lax.broadcasted_iota(jnp.int32, (q, SSD_WIDTH), 1) & (SSD_HEADDIM - 1)
    acs_row = jnp.sum(jnp.where(jj == row, acs_e, 0.0), axis=0, keepdims=True)
    causal = row_time(jj) <= row_time(row)
    lmat = jnp.exp(jnp.where(causal, acs_e - acs_row, -jnp.inf))
    last = acs_e[q - 1:q, :]
    eacs = jnp.exp(acs_e)
    dec_end = jnp.exp(last - acs_e)
    cdec = jnp.exp(last)
    xdt = xs * dt_e
    xdt_b = xdt.astype(BF16)
    xd_b = (xdt * dec_end).astype(BF16)
    head_of_lane = lax.broadcasted_iota(jnp.int32, (q, gw), 1) >> 6

    ydiags, cgs, sts = [], [], []
    for g in range(SSD_GROUPS):
        sl = slice(g * gw, (g + 1) * gw)
        bg = bm[:, g * SSD_STATE:(g + 1) * SSD_STATE]
        cg = cm[:, g * SSD_STATE:(g + 1) * SSD_STATE]
        b4 = jnp.concatenate([bg] * (SSD_HEADS // SSD_GROUPS), axis=0)
        cbt = lax.dot_general(cg, b4, (((1,), (1,)), ((), ())), preferred_element_type=F32)
        sg = (cbt * lmat[:, sl]).astype(BF16)
        xg = xdt_b[:, sl]
        xbd = jnp.concatenate(
            [jnp.where(head_of_lane == hh, xg, jnp.zeros_like(xg)) for hh in range(SSD_HEADS // SSD_GROUPS)],
            axis=0)
        ydiags.append(jnp.dot(sg, xbd, preferred_element_type=F32))
        cgs.append(cg)
        sts.append(lax.dot_general(bg, xd_b[:, sl], (((0,), (0,)), ((), ())), preferred_element_type=F32))
    zcols = [jnp.concatenate(split_rows(zs_ref, s, base), axis=0) for s in range(SSD_WIDTH // LANES)]
    gate = _silu(jnp.concatenate(zcols, axis=1))
    return dict(ydiags=ydiags, cgs=cgs, sts=sts, eacs=eacs, cdec=cdec, skip=d_ref[...] * xs, gate=gate)


def _ssd_chunk_state(loc, ht_ref):
    gw = SSD_WIDTH // SSD_GROUPS
    ys = []
    for g in range(SSD_GROUPS):
        sl = slice(g * gw, (g + 1) * gw)
        htg = ht_ref[0, :, sl]
        yoff = jnp.dot(loc["cgs"][g], htg.astype(BF16), preferred_element_type=F32)
        ys.append(loc["ydiags"][g] + loc["eacs"][:, sl] * yoff)
        ht_ref[0, :, sl] = loc["cdec"][:, sl] * htg + loc["sts"][g]
    return (jnp.concatenate(ys, axis=1) + loc["skip"]) * loc["gate"]


def _ssd_chunk_out(y, nrm_ref, y_ref, *, base):
    q = CHUNK
    hq = q // 2
    row_time = lambda r: ((r & (hq - 1)) << 1) | (r >> (hq.bit_length() - 1))
    yn = _rms(y, nrm_ref[...]).astype(BF16)
    unperm = (row_time(lax.broadcasted_iota(jnp.int32, (q, q), 1))
              == lax.broadcasted_iota(jnp.int32, (q, q), 0)).astype(BF16)
    y_ref[0, base:base + q, :] = jnp.dot(unperm, yn, preferred_element_type=F32).astype(BF16)


def _ssd_kernel(xbc_ref, z_ref, dtr_ref, cw_ref, cb_ref, dtb_ref, alog_ref, d_ref, nrm_ref,
                conv0_ref, h0_ref, y_ref, convo_ref, ht_ref, pad_ref, zs_ref, *, valid, nsub):
    c = pl.program_id(1)
    rows = nsub * CHUNK
    last = rows - CHUNK + valid

    @pl.when(c == 0)
    def _():
        for s in range(CONV_DIM // LANES):
            pad_ref[s, 5:8, :] = conv0_ref[0, :, _slab(s)]
        ht_ref[0] = h0_ref[0]

    for s in range(CONV_DIM // LANES):
        pad_ref[s, 8:8 + rows, :] = xbc_ref[0, :, _slab(s)]
    for s in range(SSD_WIDTH // LANES):
        zs_ref[s] = z_ref[0, :, _slab(s)]
    locs = [_ssd_chunk_local(dtr_ref, cw_ref, cb_ref, dtb_ref, alog_ref, d_ref, pad_ref, zs_ref,
                             base=i * CHUNK, valid=valid if i == nsub - 1 else CHUNK) for i in range(nsub)]
    ys = [_ssd_chunk_state(loc, ht_ref) for loc in locs]
    for i in range(nsub):
        _ssd_chunk_out(ys[i], nrm_ref, y_ref, base=i * CHUNK)
    for s in range(CONV_DIM // LANES):
        pad_ref[s, 5:8, :] = pad_ref[s, 8 + last - 3:8 + last, :]

    @pl.when(c == pl.num_programs(1) - 1)
    def _():
        for s in range(CONV_DIM // LANES):
            convo_ref[0, :, _slab(s)] = pad_ref[s, 5:8, :]


def _ssd(proj3, valid, cw, cb, dtb, alog, d_e, nrm, conv0, ht0):
    bsz, lp, _ = proj3.shape
    nsub = min(SSD_NSUB, lp // CHUNK)
    q = nsub * CHUNK
    kern = functools.partial(_ssd_kernel, valid=valid, nsub=nsub)
    row = lambda n: pl.BlockSpec((1, n), lambda b, c: (0, 0))
    return pl.pallas_call(
        kern,
        grid=(bsz, lp // q),
        in_specs=[
            pl.BlockSpec((1, q, CONV_DIM), lambda b, c: (b, c, 0)),
            pl.BlockSpec((1, q, SSD_WIDTH), lambda b, c: (b, c, CONV_DIM // SSD_WIDTH)),
            pl.BlockSpec((1, q, DT_PAD), lambda b, c: (b, c, (PROJ_COLS - DT_PAD) // DT_PAD)),
            pl.BlockSpec((CONV_W, CONV_DIM), lambda b, c: (0, 0)),
            row(CONV_DIM), row(DT_PAD), row(DT_PAD), row(SSD_WIDTH), row(SSD_WIDTH),
            pl.BlockSpec((1, CONV_W - 1, CONV_DIM), lambda b, c: (b, 0, 0)),
            pl.BlockSpec((1, SSD_STATE, SSD_WIDTH), lambda b, c: (b, 0, 0)),
        ],
        out_specs=[
            pl.BlockSpec((1, q, SSD_WIDTH), lambda b, c: (b, c, 0)),
            pl.BlockSpec((1, CONV_W - 1, CONV_DIM), lambda b, c: (b, 0, 0)),
            pl.BlockSpec((1, SSD_STATE, SSD_WIDTH), lambda b, c: (b, 0, 0)),
        ],
        out_shape=[
            jax.ShapeDtypeStruct((bsz, lp, SSD_WIDTH), BF16),
            jax.ShapeDtypeStruct((bsz, CONV_W - 1, CONV_DIM), F32),
            jax.ShapeDtypeStruct((bsz, SSD_STATE, SSD_WIDTH), F32),
        ],
        scratch_shapes=[pltpu.VMEM((CONV_DIM // LANES, 8 + q, LANES), F32),
                        pltpu.VMEM((SSD_WIDTH // LANES, q, LANES), F32)],
        compiler_params=_params("parallel", "arbitrary"),
        name="ssd",
    )(proj3, proj3, proj3, cw, cb, dtb, alog, d_e, nrm, conv0, ht0)


def _s5_prep_kernel(lre_ref, lim_ref, lstep_ref, bre_ref, bim_ref, lbr_ref, lbi_ref, bbr_ref, bbi_ref):
    lam_re = lre_ref[...]
    lam_im = lim_ref[...]
    step = jnp.exp(lstep_ref[...])
    mag = jnp.exp(lam_re * step)
    ang = lam_im * step
    lb_re = mag * jnp.cos(ang)
    lb_im = mag * jnp.sin(ang)
    den = lam_re * lam_re + lam_im * lam_im
    nr = lb_re - 1.0
    q_re = (nr * lam_re + lb_im * lam_im) / den
    q_im = (lb_im * lam_re - nr * lam_im) / den
    lbr_ref[...] = lb_re
    lbi_ref[...] = lb_im
    b_re = bre_ref[...]
    b_im = bim_ref[...]
    bbr_ref[...] = q_re[None] * b_re - q_im[None] * b_im
    bbi_ref[...] = q_re[None] * b_im + q_im[None] * b_re


def _s5_prep(lam_re, lam_im, log_step_e, b_re_t, b_im_t):
    gn = jax.ShapeDtypeStruct((S5_GROUPS, S5_STATE), F32)
    hgn = jax.ShapeDtypeStruct((S5_GROUP_CH, S5_GROUPS, S5_STATE), F32)
    return pl.pallas_call(_s5_prep_kernel, out_shape=[gn, gn, hgn, hgn], name="s5_prep")(
        lam_re, lam_im, log_step_e, b_re_t, b_im_t)


def _s5_kernel(u_ref, bdr_ref, bdi_ref, cdr_ref, cdi_ref, lbr_ref, lbi_ref, d_ref, h0r_ref, h0i_ref,
               y_ref, hor_ref, hoi_ref, up_ref, yp_ref, *slab_refs, lc, pitch):
    c = pl.program_id(1)

    @pl.when(c == 0)
    def _():
        hor_ref[...] = h0r_ref[...]
        hoi_ref[...] = h0i_ref[...]
        up_ref[...] = jnp.zeros_like(up_ref)

    for b in range(S5_NB):
        up_ref[b * pitch:b * pitch + lc, :] = u_ref[b]
    u = up_ref[...]
    ub = u.astype(BF16)

    spb = S5_SLABS // S5_QB
    hr_refs, hi_refs = slab_refs[:S5_QB], slab_refs[S5_QB:]

    def input_stage(qb):
        uq = ub[:, qb * LANES:(qb + 1) * LANES]
        bu_r = jnp.dot(uq, bdr_ref[qb], preferred_element_type=F32)
        bu_i = jnp.dot(uq, bdi_ref[qb], preferred_element_type=F32)
        for k in range(spb):
            hr_refs[qb][k] = bu_r[:, k * LANES:(k + 1) * LANES]
            hi_refs[qb][k] = bu_i[:, k * LANES:(k + 1) * LANES]

    def scan_stage(qb):
        hr_ref, hi_ref = hr_refs[qb], hi_refs[qb]
        for k in range(spb):
            sl = slice((qb * spb + k) * LANES, (qb * spb + k + 1) * LANES)
            lr = jnp.broadcast_to(lbr_ref[:, sl], (S5_NB, LANES))
            li = jnp.broadcast_to(lbi_ref[:, sl], (S5_NB, LANES))
            h_r, h_i = hor_ref[:, sl], hoi_ref[:, sl]
            for t in range(lc):
                rows = pl.ds(t, S5_NB, stride=pitch)
                n_r = lr * h_r - li * h_i + hr_ref[k, rows, :]
                n_i = lr * h_i + li * h_r + hi_ref[k, rows, :]
                hr_ref[k, rows, :] = n_r
                hi_ref[k, rows, :] = n_i
                h_r, h_i = n_r, n_i
            hor_ref[:, sl] = h_r
            hoi_ref[:, sl] = h_i

    def output_stage(qb):
        ln = slice(qb * LANES, (qb + 1) * LANES)
        h_re = jnp.concatenate([hr_refs[qb][k] for k in range(spb)], axis=1).astype(BF16)
        h_im = jnp.concatenate([hi_refs[qb][k] for k in range(spb)], axis=1).astype(BF16)
        yq = (jnp.dot(h_re, cdr_ref[qb], preferred_element_type=F32)
              - jnp.dot(h_im, cdi_ref[qb], preferred_element_type=F32))
        yp_ref[:, ln] = yq + d_ref[:, ln] * u[:, ln]

    for step in range(S5_QB + 2):
        if step < S5_QB:
            input_stage(step)
        if 0 <= step - 1 < S5_QB:
            scan_stage(step - 1)
        if 0 <= step - 2 < S5_QB:
            output_stage(step - 2)

    for b in range(S5_NB):
        y_ref[b] = yp_ref[b * pitch:b * pitch + lc, :]


def _s5(proj3, bdr, bdi, cdr, cdi, lbr, lbi, d5, h0r, h0i):
    bsz, l, _ = proj3.shape
    lc = min(S5_LC, l)
    pitch = lc + 4
    mp = S5_NB * pitch
    kern = functools.partial(_s5_kernel, lc=lc, pitch=pitch)
    ns = S5_GROUPS * S5_STATE
    full = lambda *shape: pl.BlockSpec(shape, lambda b, c: (0,) * len(shape))
    state = pl.BlockSpec((S5_NB, ns), lambda b, c: (b, 0))
    return pl.pallas_call(
        kern,
        grid=(bsz // S5_NB, l // lc),
        in_specs=[
            pl.BlockSpec((S5_NB, lc, S5_WIDTH), lambda b, c: (b, c, (CONV_DIM + SSD_WIDTH) // S5_WIDTH)),
            full(S5_QB, LANES, 4 * LANES), full(S5_QB, LANES, 4 * LANES),
            full(S5_QB, 4 * LANES, LANES), full(S5_QB, 4 * LANES, LANES),
            full(1, ns), full(1, ns), full(1, S5_WIDTH),
            state, state,
        ],
        out_specs=[pl.BlockSpec((S5_NB, lc, S5_WIDTH), lambda b, c: (b, c, 0)), state, state],
        out_shape=[
            jax.ShapeDtypeStruct((bsz, l, S5_WIDTH), F32),
            jax.ShapeDtypeStruct((bsz, ns), F32),
            jax.ShapeDtypeStruct((bsz, ns), F32),
        ],
        scratch_shapes=[
            pltpu.VMEM((mp, S5_WIDTH), F32),
            pltpu.VMEM((mp, S5_WIDTH), F32),
        ] + [pltpu.VMEM((S5_SLABS // S5_QB, mp, LANES), F32)] * (2 * S5_QB),
        compiler_params=_params("parallel", "arbitrary"),
        name="s5",
    )(proj3, bdr, bdi, cdr, cdi, lbr, lbi, d5, h0r, h0i)


def _outproj_kernel(x_ref, ya_ref, y5_ref, wglu_ref, bglu_ref, nrm_ref, w_ref, o_ref):
    y5 = y5_ref[...]
    g = 0.5 * y5 * (1.0 + lax.erf(y5 * (0.5 ** 0.5)))
    gl = jnp.dot(g.astype(BF16), wglu_ref[...], preferred_element_type=F32) + bglu_ref[...]
    yb = _rms(g * jax.nn.sigmoid(gl), nrm_ref[...]).astype(BF16)
    o_ref[...] = (x_ref[...]
                  + jnp.dot(ya_ref[...], w_ref[:SSD_WIDTH, :], preferred_element_type=F32)
                  + jnp.dot(yb, w_ref[SSD_WIDTH:, :], preferred_element_type=F32))


def _out_proj(x, ya, y5, wglu, bglu, nrm, w):
    t = x.shape[0]
    tm = min(OUT_TM, t)
    const = lambda *shape: pl.BlockSpec(shape, lambda i: (0,) * len(shape), pipeline_mode=pl.Buffered(1))
    return pl.pallas_call(
        _outproj_kernel,
        grid=(t // tm,),
        in_specs=[
            pl.BlockSpec((tm, D_MODEL), lambda i: (i, 0)),
            pl.BlockSpec((tm, SSD_WIDTH), lambda i: (i, 0)),
            pl.BlockSpec((tm, S5_WIDTH), lambda i: (i, 0)),
            const(S5_WIDTH, S5_WIDTH), const(1, S5_WIDTH), const(1, S5_WIDTH),
            const(D_MODEL, D_MODEL),
        ],
        out_specs=pl.BlockSpec((tm, D_MODEL), lambda i: (i, 0)),
        out_shape=jax.ShapeDtypeStruct((t, D_MODEL), F32),
        compiler_params=_params("parallel"),
        name="out_proj",
    )(x, ya, y5, wglu, bglu, nrm, w)


def _castpad_kernel(x_ref, o_ref, *, nvalid):
    j = pl.program_id(0)

    @pl.when(j < nvalid)
    def _():
        o_ref[...] = x_ref[...].astype(BF16)

    @pl.when(j >= nvalid)
    def _():
        o_ref[...] = jnp.zeros_like(o_ref)


def _cast_pad(w, axis):
    nvalid = D_FF // LANES
    other = w.shape[1 - axis]
    if axis == 1:
        block, out_shape = (other, LANES), (other, D_FF_PAD)
        in_map, out_map = (lambda j: (0, jnp.minimum(j, nvalid - 1))), (lambda j: (0, j))
    else:
        block, out_shape = (LANES, other), (D_FF_PAD, other)
        in_map, out_map = (lambda j: (jnp.minimum(j, nvalid - 1), 0)), (lambda j: (j, 0))
    return pl.pallas_call(
        functools.partial(_castpad_kernel, nvalid=nvalid),
        grid=(D_FF_PAD // LANES,),
        in_specs=[pl.BlockSpec(block, in_map)],
        out_specs=pl.BlockSpec(block, out_map),
        out_shape=jax.ShapeDtypeStruct(out_shape, BF16),
        compiler_params=_params("parallel"),
        name="cast_pad",
    )(w)


def _prepare(p):
    w = {}
    for n in ("1", "2"):
        w["norm_ffn" + n] = p["norm_ffn" + n]
        w["wg" + n] = _cast_pad(p["w_ffn%s_gate" % n][0], 1)
        w["wu" + n] = _cast_pad(p["w_ffn%s_up" % n][0], 1)
        w["wd" + n] = _cast_pad(p["w_ffn%s_down" % n][0], 0)
    w_in = p["w_in"][0]
    c0, c1, c2 = SSD_WIDTH, SSD_WIDTH + CONV_DIM, SSD_WIDTH + CONV_DIM + SSD_HEADS
    w["w_in"] = jnp.concatenate(
        [w_in[:, c0:c1], w_in[:, :c0], w_in[:, c2:],
         jnp.pad(w_in[:, c1:c2], ((0, 0), (0, DT_PAD - SSD_HEADS)))], axis=1).astype(BF16)
    w["norm_mix"] = p["norm_mix"]
    w["conv_w"] = p["conv_w"][0]
    w["conv_b"] = p["conv_b"]
    w["dt_bias"] = jnp.pad(p["dt_bias"], ((0, 0), (0, DT_PAD - SSD_HEADS)))
    w["a_log"] = jnp.pad(p["a_log"], ((0, 0), (0, DT_PAD - SSD_HEADS)))
    w["d_ssd"] = jnp.repeat(p["d_ssd"][0], SSD_HEADDIM)[None]
    w["norm_ssd"] = p["norm_ssd"]

    lbr, lbi, bbr, bbi = _s5_prep(
        p["s5_lambda_re"][0], p["s5_lambda_im"][0],
        jnp.broadcast_to(p["s5_log_step"][0][:, None], (S5_GROUPS, S5_STATE)),
        jnp.transpose(p["s5_b_re"][0], (2, 0, 1)), jnp.transpose(p["s5_b_im"][0], (2, 0, 1)))
    eye = jnp.eye(S5_QB, dtype=F32)
    gl = S5_GROUPS // S5_QB

    def bd(bb):
        t = jnp.einsum("hqgn,gk->qghkn", bb.reshape(S5_GROUP_CH, S5_QB, gl, S5_STATE), eye)
        return t.reshape(S5_QB, LANES, gl * S5_STATE).astype(BF16)

    def cd(cc):
        t = jnp.einsum("qghn,gk->qgnkh", cc.reshape(S5_QB, gl, S5_GROUP_CH, S5_STATE), eye)
        return t.reshape(S5_QB, gl * S5_STATE, LANES).astype(BF16)

    w["bdr"], w["bdi"] = bd(bbr), bd(bbi)
    w["cdr"], w["cdi"] = cd(p["s5_c_re"][0]), cd(p["s5_c_im"][0])
    w["lbr"] = lbr.reshape(1, -1)
    w["lbi"] = lbi.reshape(1, -1)
    w["s5_d"] = p["s5_d"][0].reshape(1, -1)
    w["w_glu"] = p["w_glu"][0].astype(BF16)
    w["b_glu"] = p["b_glu"]
    w["norm_s5"] = p["norm_s5"]
    w["w_out"] = p["w_out"][0].astype(BF16)
    w["norm_final"] = p["norm_final"][None]
    return w


def _trunk(x, conv0, ssd0, s5r0, s5i0, w):
    bsz, l, _ = x.shape
    t = bsz * l
    x1 = _ffn(x.reshape(t, D_MODEL), w["norm_ffn1"], w["wg1"], w["wu1"], w["wd1"])
    proj = _in_proj(x1, w["norm_mix"], w["w_in"]).reshape(bsz, l, PROJ_COLS)

    if l % CHUNK == 0:
        proj_ssd, valid = proj, CHUNK
    else:
        assert l < CHUNK
        proj_ssd, valid = jnp.pad(proj, ((0, 0), (0, CHUNK - l), (0, 0))), l
    ht0 = jnp.swapaxes(ssd0.reshape(bsz, SSD_WIDTH, SSD_STATE), 1, 2)
    y_ssd, conv_new, ht_new = _ssd(proj_ssd, valid, w["conv_w"], w["conv_b"], w["dt_bias"], w["a_log"],
                                   w["d_ssd"], w["norm_ssd"], conv0, ht0)
    ssd_new = jnp.swapaxes(ht_new, 1, 2).reshape(bsz, SSD_HEADS, SSD_HEADDIM, SSD_STATE)

    y_s5, s5r_new, s5i_new = _s5(proj, w["bdr"], w["bdi"], w["cdr"], w["cdi"], w["lbr"], w["lbi"],
                                 w["s5_d"], s5r0.reshape(bsz, -1), s5i0.reshape(bsz, -1))

    x2 = _out_proj(x1, y_ssd[:, :l].reshape(t, SSD_WIDTH), y_s5.reshape(t, S5_WIDTH),
                   w["w_glu"], w["b_glu"], w["norm_s5"], w["w_out"])
    y = _ffn(x2, w["norm_ffn2"], w["wg2"], w["wu2"], w["wd2"], final_g=w["norm_final"])
    return (y.reshape(bsz, l, D_MODEL), conv_new[None], ssd_new[None],
            s5r_new.reshape(1, bsz, S5_GROUPS, S5_STATE), s5i_new.reshape(1, bsz, S5_GROUPS, S5_STATE))


def kernel(x_prompt, x_sample, cache_conv, state_ssd, state_s5_re, state_s5_im, norm_ffn1, w_ffn1_gate, w_ffn1_up, w_ffn1_down, norm_mix, w_in, conv_w, conv_b, dt_bias, a_log, d_ssd, norm_ssd, s5_lambda_re, s5_lambda_im, s5_log_step, s5_b_re, s5_b_im, s5_c_re, s5_c_im, s5_d, w_glu, b_glu, norm_s5, w_out, norm_ffn2, w_ffn2_gate, w_ffn2_up, w_ffn2_down, norm_final):
    p = dict(norm_ffn1=norm_ffn1, w_ffn1_gate=w_ffn1_gate, w_ffn1_up=w_ffn1_up, w_ffn1_down=w_ffn1_down,
             norm_mix=norm_mix, w_in=w_in, conv_w=conv_w, conv_b=conv_b, dt_bias=dt_bias, a_log=a_log,
             d_ssd=d_ssd, norm_ssd=norm_ssd, s5_lambda_re=s5_lambda_re, s5_lambda_im=s5_lambda_im,
             s5_log_step=s5_log_step, s5_b_re=s5_b_re, s5_b_im=s5_b_im, s5_c_re=s5_c_re, s5_c_im=s5_c_im,
             s5_d=s5_d, w_glu=w_glu, b_glu=b_glu, norm_s5=norm_s5, w_out=w_out, norm_ffn2=norm_ffn2,
             w_ffn2_gate=w_ffn2_gate, w_ffn2_up=w_ffn2_up, w_ffn2_down=w_ffn2_down, norm_final=norm_final)
    w = _prepare(p)
    bsz = x_prompt.shape[0]
    zeros = lambda *s: jnp.zeros(s, F32)
    out_p = _trunk(x_prompt, zeros(bsz, CONV_W - 1, CONV_DIM),
                   zeros(bsz, SSD_HEADS, SSD_HEADDIM, SSD_STATE),
                   zeros(bsz, S5_GROUPS, S5_STATE), zeros(bsz, S5_GROUPS, S5_STATE), w)
    out_s = _trunk(x_sample, cache_conv[0], state_ssd[0], state_s5_re[0], state_s5_im[0], w)
    return (out_p[0], out_s[0]) + out_p[1:] + out_s[1:]
```

```python
import functools

import jax
import jax.numpy as jnp
from jax import lax
from jax.experimental import pallas as pl
from jax.experimental.pallas import tpu as pltpu

F32 = jnp.float32
BF16 = jnp.bfloat16
EPS = 1e-6

D_MODEL = 2048
D_FF = 5504
SSD_WIDTH = 1024
SSD_HEADDIM = 64
SSD_HEADS = 16
SSD_GROUPS = 4
SSD_STATE = 128
CONV_W = 4
CONV_DIM = SSD_WIDTH + 2 * SSD_GROUPS * SSD_STATE
S5_WIDTH = 1024
S5_GROUP_CH = 16
S5_GROUPS = 64
S5_STATE = 64
CHUNK = 64
SSD_NSUB = 4

LANES = 128
D_FF_PAD = 5632
FFN_TM = 1024
FFN_TF = 512
DT_PAD = LANES
PROJ_COLS = CONV_DIM + SSD_WIDTH + S5_WIDTH + DT_PAD
PROJ_TM = 512
MIX_TM = 512
PROJ_PIECE = 256
OUT_TM = 512
S5_NB = 8
S5_LC = 64
S5_SLABS = S5_GROUPS * S5_STATE // LANES
S5_QB = S5_WIDTH // LANES
VMEM_LIMIT = 56 * 1024 * 1024
MIX_VMEM_LIMIT = 60 * 1024 * 1024


def _rms(x, g):
    ms = jnp.mean(x * x, axis=-1, keepdims=True)
    return x * lax.rsqrt(ms + EPS) * g


def _silu(x):
    return x * jax.nn.sigmoid(x)


def _params(*sem, vmem_limit=VMEM_LIMIT):
    return pltpu.CompilerParams(dimension_semantics=sem, vmem_limit_bytes=vmem_limit)


def _ffn_kernel(x_ref, g_ref, wg_ref, wu_ref, wd_ref, gf_ref, o_ref, h_ref, *, final_norm):
    j = pl.program_id(1)

    @pl.when(j == 0)
    def _():
        x = x_ref[...]
        h_ref[...] = _rms(x, g_ref[...]).astype(BF16)
        o_ref[...] = x

    h = h_ref[...]
    gate = jnp.dot(h, wg_ref[...], preferred_element_type=F32)
    up = jnp.dot(h, wu_ref[...], preferred_element_type=F32)
    a = (0.5 * (_silu(gate) * up)).astype(BF16)
    o_ref[...] += jnp.dot(a, wd_ref[...], preferred_element_type=F32)

    if final_norm:
        @pl.when(j == pl.num_programs(1) - 1)
        def _():
            o_ref[...] = _rms(o_ref[...], gf_ref[...])


def _ffn(x, norm_g, wg, wu, wd, final_g=None):
    t = x.shape[0]
    tm = min(FFN_TM, t)
    kern = functools.partial(_ffn_kernel, final_norm=final_g is not None)
    gf = norm_g if final_g is None else final_g
    return pl.pallas_call(
        kern,
        grid=(t // tm, D_FF_PAD // FFN_TF),
        in_specs=[
            pl.BlockSpec((tm, D_MODEL), lambda i, j: (i, 0)),
            pl.BlockSpec((1, D_MODEL), lambda i, j: (0, 0)),
            pl.BlockSpec((D_MODEL, FFN_TF), lambda i, j: (0, j)),
            pl.BlockSpec((D_MODEL, FFN_TF), lambda i, j: (0, j)),
            pl.BlockSpec((FFN_TF, D_MODEL), lambda i, j: (j, 0)),
            pl.BlockSpec((1, D_MODEL), lambda i, j: (0, 0)),
        ],
        out_specs=pl.BlockSpec((tm, D_MODEL), lambda i, j: (i, 0)),
        out_shape=jax.ShapeDtypeStruct((t, D_MODEL), F32),
        scratch_shapes=[pltpu.VMEM((tm, D_MODEL), BF16)],
        compiler_params=_params("parallel", "arbitrary"),
        name="ffn",
    )(x, norm_g, wg, wu, wd, gf)


def _inproj_kernel(x_ref, g_ref, w_ref, o_ref):
    h = _rms(x_ref[...], g_ref[...]).astype(BF16)
    o_ref[...] = jnp.dot(h, w_ref[...], preferred_element_type=F32)


def _in_proj(x, norm_g, w):
    t = x.shape[0]
    tm = min(PROJ_TM, t)
    return pl.pallas_call(
        _inproj_kernel,
        grid=(t // tm,),
        in_specs=[
            pl.BlockSpec((tm, D_MODEL), lambda i: (i, 0)),
            pl.BlockSpec((1, D_MODEL), lambda i: (0, 0)),
            pl.BlockSpec((D_MODEL, PROJ_COLS), lambda i: (0, 0), pipeline_mode=pl.Buffered(1)),
        ],
        out_specs=pl.BlockSpec((tm, PROJ_COLS), lambda i: (i, 0)),
        out_shape=jax.ShapeDtypeStruct((t, PROJ_COLS), F32),
        compiler_params=_params("parallel"),
        name="in_proj",
    )(x, norm_g, w)


def _expand_heads(v):
    q = v.shape[0]
    lane = lax.broadcasted_iota(jnp.int32, (q, LANES), 1)
    blocks = []
    for k in range(SSD_WIDTH // LANES):
        a = jnp.broadcast_to(v[:, 2 * k:2 * k + 1], (q, LANES))
        b = jnp.broadcast_to(v[:, 2 * k + 1:2 * k + 2], (q, LANES))
        blocks.append(jnp.where(lane < SSD_HEADDIM, a, b))
    return jnp.concatenate(blocks, axis=1)


def _slab(s):
    return slice(s * LANES, (s + 1) * LANES)


def _ssd_chunk_pre(dtr_ref, cw_ref, cb_ref, dtb_ref, alog_ref, d_ref, pad_ref, zs_ref, *, base, valid):
    q = CHUNK
    hq = q // 2
    gw = SSD_WIDTH // SSD_GROUPS
    slab = _slab
    row_time = lambda r: ((r & (hq - 1)) << 1) | (r >> (hq.bit_length() - 1))

    def split_rows(ref3, s, start):
        return [ref3[s, pl.ds(start + e, hq, stride=2), :] for e in range(2)]

    cw = cw_ref[...]
    cb = cb_ref[...]
    cols = []
    for s in range(CONV_DIM // LANES):
        taps = [split_rows(pad_ref, s, base + 5 + k) for k in range(CONV_W)]
        halves = []
        for e in range(2):
            acc = cb[:, slab(s)] + taps[0][e] * cw[0:1, slab(s)]
            for k in range(1, CONV_W):
                acc = acc + taps[k][e] * cw[k:k + 1, slab(s)]
            halves.append(acc)
        cols.append(jnp.concatenate(halves, axis=0))
    act = _silu(jnp.concatenate(cols, axis=1))
    xs = act[:, :SSD_WIDTH]
    bm = act[:, SSD_WIDTH:SSD_WIDTH + SSD_GROUPS * SSD_STATE].astype(BF16)
    cm = act[:, SSD_WIDTH + SSD_GROUPS * SSD_STATE:].astype(BF16)

    dt_raw = jnp.concatenate([dtr_ref[0, pl.ds(base + e, hq, stride=2), :] for e in range(2)], axis=0)
    dtc = jax.nn.softplus(dt_raw + dtb_ref[...])
    if valid < q:
        dtc = jnp.where(row_time(lax.broadcasted_iota(jnp.int32, dtc.shape, 0)) < valid, dtc, 0.0)
    da = dtc * (-jnp.exp(alog_ref[...]))
    tril = (row_time(lax.broadcasted_iota(jnp.int32, (q, q), 1))
            <= row_time(lax.broadcasted_iota(jnp.int32, (q, q), 0))).astype(F32)
    acs_c = jnp.dot(tril, da, precision=lax.Precision.HIGHEST, preferred_element_type=F32)
    cbts = []
    for g in range(SSD_GROUPS):
        bg = bm[:, g * SSD_STATE:(g + 1) * SSD_STATE]
        cg = cm[:, g * SSD_STATE:(g + 1) * SSD_STATE]
        b4 = jnp.concatenate([bg] * (SSD_HEADS // SSD_GROUPS), axis=0)
        cbts.append(lax.dot_general(cg, b4, (((1,), (1,)), ((), ())), preferred_element_type=F32))
    zcols = [jnp.concatenate(split_rows(zs_ref, s, base), axis=0) for s in range(SSD_WIDTH // LANES)]
    gate = _silu(jnp.concatenate(zcols, axis=1))
    return dict(xs=xs, bm=bm, cm=cm, dtc=dtc, acs_c=acs_c, cbts=cbts, gate=gate, skip=d_ref[...] * xs)


def _ssd_chunk_post(pre):
    q = CHUNK
    hq = q // 2
    gw = SSD_WIDTH // SSD_GROUPS
    row_time = lambda r: ((r & (hq - 1)) << 1) | (r >> (hq.bit_length() - 1))
    xs, bm, cm = pre["xs"], pre["bm"], pre["cm"]
    dt_e = _expand_heads(pre["dtc"])
    acs_e = _expand_heads(pre["acs_c"])
    row = lax.broadcasted_iota(jnp.int32, (q, SSD_WIDTH), 0)
    jj = lax.broadcasted_iota(jnp.int32, (q, SSD_WIDTH), 1) & (SSD_HEADDIM - 1)
    acs_row = jnp.sum(jnp.where(jj == row, acs_e, 0.0), axis=0, keepdims=True)
    causal = row_time(jj) <= row_time(row)
    lmat = jnp.exp(jnp.where(causal, acs_e - acs_row, -jnp.inf))
    last = acs_e[q - 1:q, :]
    eacs = jnp.exp(acs_e)
    dec_end = jnp.exp(last - acs_e)
    cdec = jnp.exp(last)
    xdt = xs * dt_e
    xdt_b = xdt.astype(BF16)
    xd_b = (xdt * dec_end).astype(BF16)
    head_of_lane = lax.broadcasted_iota(jnp.int32, (q, gw), 1) >> 6

    ydiags, cgs, sts = [], [], []
    for g in range(SSD_GROUPS):
        sl = slice(g * gw, (g + 1) * gw)
        bg = bm[:, g * SSD_STATE:(g + 1) * SSD_STATE]
        cg = cm[:, g * SSD_STATE:(g + 1) * SSD_STATE]
        sg = (pre["cbts"][g] * lmat[:, sl]).astype(BF16)
        xg = xdt_b[:, sl]
        xbd = jnp.concatenate(
            [jnp.where(head_of_lane == hh, xg, jnp.zeros_like(xg)) for hh in range(SSD_HEADS // SSD_GROUPS)],
            axis=0)
        ydiags.append(jnp.dot(sg, xbd, preferred_element_type=F32))
        cgs.append(cg)
        sts.append(lax.dot_general(bg, xd_b[:, sl], (((0,), (0,)), ((), ())), preferred_element_type=F32))
    return dict(ydiags=ydiags, cgs=cgs, sts=sts, eacs=eacs, cdec=cdec, skip=pre["skip"], gate=pre["gate"])


def _ssd_chunk_state(loc, ht_ref):
    gw = SSD_WIDTH // SSD_GROUPS
    ys = []
    for g in range(SSD_GROUPS):
        sl = slice(g * gw, (g + 1) * gw)
        htg = ht_ref[0, :, sl]
        yoff = jnp.dot(loc["cgs"][g], htg.astype(BF16), preferred_element_type=F32)
        ys.append(loc["ydiags"][g] + loc["eacs"][:, sl] * yoff)
        ht_ref[0, :, sl] = loc["cdec"][:, sl] * htg + loc["sts"][g]
    return (jnp.concatenate(ys, axis=1) + loc["skip"]) * loc["gate"]


def _ssd_chunk_out(y, nrm_ref, y_ref, *, base):
    q = CHUNK
    hq = q // 2
    row_time = lambda r: ((r & (hq - 1)) << 1) | (r >> (hq.bit_length() - 1))
    yn = _rms(y, nrm_ref[...]).astype(BF16)
    unperm = (row_time(lax.broadcasted_iota(jnp.int32, (q, q), 1))
              == lax.broadcasted_iota(jnp.int32, (q, q), 0)).astype(BF16)
    y_ref[0, base:base + q, :] = jnp.dot(unperm, yn, preferred_element_type=F32).astype(BF16)


def _ssd_kernel(xbc_ref, z_ref, dtr_ref, cw_ref, cb_ref, dtb_ref, alog_ref, d_ref, nrm_ref,
                conv0_ref, h0_ref, y_ref, convo_ref, ht_ref, pad_ref, zs_ref, *, valid, nsub):
    c = pl.program_id(1)
    rows = nsub * CHUNK
    last = rows - CHUNK + valid

    @pl.when(c == 0)
    def _():
        for s in range(CONV_DIM // LANES):
            pad_ref[s, 5:8, :] = conv0_ref[0, :, _slab(s)]
        ht_ref[0] = h0_ref[0]

    for s in range(CONV_DIM // LANES):
        pad_ref[s, 8:8 + rows, :] = xbc_ref[0, :, _slab(s)]
    for s in range(SSD_WIDTH // LANES):
        zs_ref[s] = z_ref[0, :, _slab(s)]
    pres = [_ssd_chunk_pre(dtr_ref, cw_ref, cb_ref, dtb_ref, alog_ref, d_ref, pad_ref, zs_ref,
                           base=i * CHUNK, valid=valid if i == nsub - 1 else CHUNK) for i in range(nsub)]
    locs = [_ssd_chunk_post(pre) for pre in pres]
    ys = [_ssd_chunk_state(loc, ht_ref) for loc in locs]
    for i in range(nsub):
        _ssd_chunk_out(ys[i], nrm_ref, y_ref, base=i * CHUNK)
    for s in range(CONV_DIM // LANES):
        pad_ref[s, 5:8, :] = pad_ref[s, 8 + last - 3:8 + last, :]

    @pl.when(c == pl.num_programs(1) - 1)
    def _():
        for s in range(CONV_DIM // LANES):
            convo_ref[0, :, _slab(s)] = pad_ref[s, 5:8, :]


def _ssd(proj3, valid, cw, cb, dtb, alog, d_e, nrm, conv0, ht0):
    bsz, lp, _ = proj3.shape
    nsub = min(SSD_NSUB, lp // CHUNK)
    q = nsub * CHUNK
    kern = functools.partial(_ssd_kernel, valid=valid, nsub=nsub)
    row = lambda n: pl.BlockSpec((1, n), lambda b, c: (0, 0))
    return pl.pallas_call(
        kern,
        grid=(bsz, lp // q),
        in_specs=[
            pl.BlockSpec((1, q, CONV_DIM), lambda b, c: (b, c, 0)),
            pl.BlockSpec((1, q, SSD_WIDTH), lambda b, c: (b, c, CONV_DIM // SSD_WIDTH)),
            pl.BlockSpec((1, q, DT_PAD), lambda b, c: (b, c, (PROJ_COLS - DT_PAD) // DT_PAD)),
            pl.BlockSpec((CONV_W, CONV_DIM), lambda b, c: (0, 0)),
            row(CONV_DIM), row(DT_PAD), row(DT_PAD), row(SSD_WIDTH), row(SSD_WIDTH),
            pl.BlockSpec((1, CONV_W - 1, CONV_DIM), lambda b, c: (b, 0, 0)),
            pl.BlockSpec((1, SSD_STATE, SSD_WIDTH), lambda b, c: (b, 0, 0)),
        ],
        out_specs=[
            pl.BlockSpec((1, q, SSD_WIDTH), lambda b, c: (b, c, 0)),
            pl.BlockSpec((1, CONV_W - 1, CONV_DIM), lambda b, c: (b, 0, 0)),
            pl.BlockSpec((1, SSD_STATE, SSD_WIDTH), lambda b, c: (b, 0, 0)),
        ],
        out_shape=[
            jax.ShapeDtypeStruct((bsz, lp, SSD_WIDTH), BF16),
            jax.ShapeDtypeStruct((bsz, CONV_W - 1, CONV_DIM), F32),
            jax.ShapeDtypeStruct((bsz, SSD_STATE, SSD_WIDTH), F32),
        ],
        scratch_shapes=[pltpu.VMEM((CONV_DIM // LANES, 8 + q, LANES), F32),
                        pltpu.VMEM((SSD_WIDTH // LANES, q, LANES), F32)],
        compiler_params=_params("parallel", "arbitrary"),
        name="ssd",
    )(proj3, proj3, proj3, cw, cb, dtb, alog, d_e, nrm, conv0, ht0)


def _mix_in_kernel(x_ref, g_ref, w_ref, cw_ref, cb_ref, dtb_ref, alog_ref, d_ref, nrm_ref, conv0_ref, h0_ref,
                   u5_ref, y_ref, convo_ref, ht_ref, pad_ref, zs_ref, dts_ref, *, tiles_per_seq):
    i = pl.program_id(0)
    nsub = MIX_TM // CHUNK
    in_seq = lax.rem(jnp.maximum(i - 1, 0), tiles_per_seq)

    @pl.when(i == 0)
    def _():
        pad_ref[...] = jnp.zeros_like(pad_ref)
        zs_ref[...] = jnp.zeros_like(zs_ref)
        dts_ref[...] = jnp.zeros_like(dts_ref)
        ht_ref[...] = jnp.zeros_like(ht_ref)

    @pl.when(jnp.logical_and(i >= 1, in_seq == 0))
    def _():
        for s in range(CONV_DIM // LANES):
            pad_ref[s, 5:8, :] = conv0_ref[0, :, _slab(s)]
        ht_ref[0] = h0_ref[0]

    h = _rms(x_ref[...], g_ref[...]).astype(BF16)
    bounds = list(range(0, PROJ_COLS, PROJ_PIECE)) + [PROJ_COLS]
    npieces = len(bounds) - 1
    per_chunk = npieces // nsub
    pieces = []

    def project(n):
        pieces.append(jnp.dot(h, w_ref[:, bounds[n]:bounds[n + 1]], preferred_element_type=F32))

    pres, locs = [], []
    for c in range(nsub + 1):
        if c < nsub:
            pres.append(_ssd_chunk_pre(dts_ref, cw_ref, cb_ref, dtb_ref, alog_ref, d_ref, pad_ref, zs_ref,
                                       base=c * CHUNK, valid=CHUNK))
        if c >= 1:
            locs.append(_ssd_chunk_post(pres[c - 1]))
        for n in range(c * per_chunk, min((c + 1) * per_chunk, nsub * per_chunk)):
            project(n)
    ys = [_ssd_chunk_state(loc, ht_ref) for loc in locs]
    for n in range(nsub * per_chunk, npieces):
        project(n)
    for c in range(nsub):
        _ssd_chunk_out(ys[c], nrm_ref, y_ref, base=c * CHUNK)

    for s in range(CONV_DIM // LANES):
        pad_ref[s, 5:8, :] = pad_ref[s, 8 + MIX_TM - 3:8 + MIX_TM, :]

    def column_tile(c0):
        return pieces[c0 // PROJ_PIECE][:, c0 % PROJ_PIECE:c0 % PROJ_PIECE + LANES]

    for s in range(CONV_DIM // LANES):
        pad_ref[s, 8:8 + MIX_TM, :] = column_tile(s * LANES)
    for s in range(SSD_WIDTH // LANES):
        zs_ref[s] = column_tile(CONV_DIM + s * LANES)
        u5_ref[:, _slab(s)] = column_tile(CONV_DIM + SSD_WIDTH + s * LANES)
    dts_ref[0] = column_tile(PROJ_COLS - DT_PAD)

    @pl.when(jnp.logical_and(i >= 1, in_seq == tiles_per_seq - 1))
    def _():
        for s in range(CONV_DIM // LANES):
            convo_ref[0, :, _slab(s)] = pad_ref[s, 5:8, :]


def _mix_in(x1, l, w, conv0, ht0):
    t = x1.shape[0]
    bsz, nt, tps = t // l, t // MIX_TM, l // MIX_TM
    seq = lambda i: jnp.maximum(i - 1, 0) // tps
    row = lambda n: pl.BlockSpec((1, n), lambda i: (0, 0))
    u5, y, conv_new, ht_new = pl.pallas_call(
        functools.partial(_mix_in_kernel, tiles_per_seq=tps),
        grid=(nt + 1,),
        in_specs=[
            pl.BlockSpec((MIX_TM, D_MODEL), lambda i: (jnp.minimum(i, nt - 1), 0)),
            row(D_MODEL),
            pl.BlockSpec((D_MODEL, PROJ_COLS), lambda i: (0, 0), pipeline_mode=pl.Buffered(1)),
            pl.BlockSpec((CONV_W, CONV_DIM), lambda i: (0, 0)),
            row(CONV_DIM), row(DT_PAD), row(DT_PAD), row(SSD_WIDTH), row(SSD_WIDTH),
            pl.BlockSpec((1, CONV_W - 1, CONV_DIM), lambda i: (seq(i), 0, 0)),
            pl.BlockSpec((1, SSD_STATE, SSD_WIDTH), lambda i: (seq(i), 0, 0)),
        ],
        out_specs=[
            pl.BlockSpec((MIX_TM, S5_WIDTH), lambda i: (jnp.minimum(i, nt - 1), 0)),
            pl.BlockSpec((1, MIX_TM, SSD_WIDTH), lambda i: (jnp.maximum(i - 1, 0), 0, 0)),
            pl.BlockSpec((1, CONV_W - 1, CONV_DIM), lambda i: (seq(i), 0, 0)),
            pl.BlockSpec((1, SSD_STATE, SSD_WIDTH), lambda i: (seq(i), 0, 0)),
        ],
        out_shape=[
            jax.ShapeDtypeStruct((t, S5_WIDTH), F32),
            jax.ShapeDtypeStruct((nt, MIX_TM, SSD_WIDTH), BF16),
            jax.ShapeDtypeStruct((bsz, CONV_W - 1, CONV_DIM), F32),
            jax.ShapeDtypeStruct((bsz, SSD_STATE, SSD_WIDTH), F32),
        ],
        scratch_shapes=[pltpu.VMEM((CONV_DIM // LANES, 8 + MIX_TM, LANES), F32),
                        pltpu.VMEM((SSD_WIDTH // LANES, MIX_TM, LANES), F32),
                        pltpu.VMEM((1, MIX_TM, DT_PAD), F32)],
        compiler_params=_params("arbitrary", vmem_limit=MIX_VMEM_LIMIT),
        name="mix_in",
    )(x1, w["norm_mix"], w["w_in"], w["conv_w"], w["conv_b"], w["dt_bias"], w["a_log"], w["d_ssd"],
      w["norm_ssd"], conv0, ht0)
    return u5, y.reshape(t, SSD_WIDTH), conv_new, ht_new


def _s5_prep_kernel(lre_ref, lim_ref, lstep_ref, bre_ref, bim_ref, lbr_ref, lbi_ref, bbr_ref, bbi_ref):
    lam_re = lre_ref[...]
    lam_im = lim_ref[...]
    step = jnp.exp(lstep_ref[...])
    mag = jnp.exp(lam_re * step)
    ang = lam_im * step
    lb_re = mag * jnp.cos(ang)
    lb_im = mag * jnp.sin(ang)
    den = lam_re * lam_re + lam_im * lam_im
    nr = lb_re - 1.0
    q_re = (nr * lam_re + lb_im * lam_im) / den
    q_im = (lb_im * lam_re - nr * lam_im) / den
    lbr_ref[...] = lb_re
    lbi_ref[...] = lb_im
    b_re = bre_ref[...]
    b_im = bim_ref[...]
    bbr_ref[...] = q_re[None] * b_re - q_im[None] * b_im
    bbi_ref[...] = q_re[None] * b_im + q_im[None] * b_re


def _s5_prep(lam_re, lam_im, log_step_e, b_re_t, b_im_t):
    gn = jax.ShapeDtypeStruct((S5_GROUPS, S5_STATE), F32)
    hgn = jax.ShapeDtypeStruct((S5_GROUP_CH, S5_GROUPS, S5_STATE), F32)
    return pl.pallas_call(_s5_prep_kernel, out_shape=[gn, gn, hgn, hgn], name="s5_prep")(
        lam_re, lam_im, log_step_e, b_re_t, b_im_t)


def _s5_kernel(u_ref, bdr_ref, bdi_ref, cdr_ref, cdi_ref, lbr_ref, lbi_ref, d_ref, h0r_ref, h0i_ref,
               y_ref, hor_ref, hoi_ref, up_ref, yp_ref, *slab_refs, lc, pitch):
    c = pl.program_id(1)

    @pl.when(c == 0)
    def _():
        hor_ref[...] = h0r_ref[...]
        hoi_ref[...] = h0i_ref[...]
        up_ref[...] = jnp.zeros_like(up_ref)

    for b in range(S5_NB):
        up_ref[b * pitch:b * pitch + lc, :] = u_ref[b]
    u = up_ref[...]
    ub = u.astype(BF16)

    spb = S5_SLABS // S5_QB
    hr_refs, hi_refs = slab_refs[:S5_QB], slab_refs[S5_QB:]

    def input_stage(qb):
        uq = ub[:, qb * LANES:(qb + 1) * LANES]
        bu_r = jnp.dot(uq, bdr_ref[qb], preferred_element_type=F32)
        bu_i = jnp.dot(uq, bdi_ref[qb], preferred_element_type=F32)
        for k in range(spb):
            hr_refs[qb][k] = bu_r[:, k * LANES:(k + 1) * LANES]
            hi_refs[qb][k] = bu_i[:, k * LANES:(k + 1) * LANES]

    def scan_stage(qb):
        hr_ref, hi_ref = hr_refs[qb], hi_refs[qb]
        for k in range(spb):
            sl = slice((qb * spb + k) * LANES, (qb * spb + k + 1) * LANES)
            lr = jnp.broadcast_to(lbr_ref[:, sl], (S5_NB, LANES))
            li = jnp.broadcast_to(lbi_ref[:, sl], (S5_NB, LANES))
            h_r, h_i = hor_ref[:, sl], hoi_ref[:, sl]
            for t in range(lc):
                rows = pl.ds(t, S5_NB, stride=pitch)
                n_r = lr * h_r - li * h_i + hr_ref[k, rows, :]
                n_i = lr * h_i + li * h_r + hi_ref[k, rows, :]
                hr_ref[k, rows, :] = n_r
                hi_ref[k, rows, :] = n_i
                h_r, h_i = n_r, n_i
            hor_ref[:, sl] = h_r
            hoi_ref[:, sl] = h_i

    def output_stage(qb):
        ln = slice(qb * LANES, (qb + 1) * LANES)
        h_re = jnp.concatenate([hr_refs[qb][k] for k in range(spb)], axis=1).astype(BF16)
        h_im = jnp.concatenate([hi_refs[qb][k] for k in range(spb)], axis=1).astype(BF16)
        yq = (jnp.dot(h_re, cdr_ref[qb], preferred_element_type=F32)
              - jnp.dot(h_im, cdi_ref[qb], preferred_element_type=F32))
        yp_ref[:, ln] = yq + d_ref[:, ln] * u[:, ln]

    for step in range(S5_QB + 2):
        if step < S5_QB:
            input_stage(step)
        if 0 <= step - 1 < S5_QB:
            scan_stage(step - 1)
        if 0 <= step - 2 < S5_QB:
            output_stage(step - 2)

    for b in range(S5_NB):
        y_ref[b] = yp_ref[b * pitch:b * pitch + lc, :]


def _s5(proj3, col, bdr, bdi, cdr, cdi, lbr, lbi, d5, h0r, h0i):
    bsz, l, _ = proj3.shape
    lc = min(S5_LC, l)
    pitch = lc + 4
    mp = S5_NB * pitch
    kern = functools.partial(_s5_kernel, lc=lc, pitch=pitch)
    ns = S5_GROUPS * S5_STATE
    full = lambda *shape: pl.BlockSpec(shape, lambda b, c: (0,) * len(shape))
    state = pl.BlockSpec((S5_NB, ns), lambda b, c: (b, 0))
    return pl.pallas_call(
        kern,
        grid=(bsz // S5_NB, l // lc),
        in_specs=[
            pl.BlockSpec((S5_NB, lc, S5_WIDTH), lambda b, c: (b, c, col)),
            full(S5_QB, LANES, 4 * LANES), full(S5_QB, LANES, 4 * LANES),
            full(S5_QB, 4 * LANES, LANES), full(S5_QB, 4 * LANES, LANES),
            full(1, ns), full(1, ns), full(1, S5_WIDTH),
            state, state,
        ],
        out_specs=[pl.BlockSpec((S5_NB, lc, S5_WIDTH), lambda b, c: (b, c, 0)), state, state],
        out_shape=[
            jax.ShapeDtypeStruct((bsz, l, S5_WIDTH), F32),
            jax.ShapeDtypeStruct((bsz, ns), F32),
            jax.ShapeDtypeStruct((bsz, ns), F32),
        ],
        scratch_shapes=[
            pltpu.VMEM((mp, S5_WIDTH), F32),
            pltpu.VMEM((mp, S5_WIDTH), F32),
        ] + [pltpu.VMEM((S5_SLABS // S5_QB, mp, LANES), F32)] * (2 * S5_QB),
        compiler_params=_params("parallel", "arbitrary"),
        name="s5",
    )(proj3, bdr, bdi, cdr, cdi, lbr, lbi, d5, h0r, h0i)


def _outproj_kernel(x_ref, ya_ref, y5_ref, wglu_ref, bglu_ref, nrm_ref, w_ref, o_ref):
    y5 = y5_ref[...]
    g = 0.5 * y5 * (1.0 + lax.erf(y5 * (0.5 ** 0.5)))
    gl = jnp.dot(g.astype(BF16), wglu_ref[...], preferred_element_type=F32) + bglu_ref[...]
    yb = _rms(g * jax.nn.sigmoid(gl), nrm_ref[...]).astype(BF16)
    o_ref[...] = (x_ref[...]
                  + jnp.dot(ya_ref[...], w_ref[:SSD_WIDTH, :], preferred_element_type=F32)
                  + jnp.dot(yb, w_ref[SSD_WIDTH:, :], preferred_element_type=F32))


def _out_proj(x, ya, y5, wglu, bglu, nrm, w):
    t = x.shape[0]
    tm = min(OUT_TM, t)
    const = lambda *shape: pl.BlockSpec(shape, lambda i: (0,) * len(shape), pipeline_mode=pl.Buffered(1))
    return pl.pallas_call(
        _outproj_kernel,
        grid=(t // tm,),
        in_specs=[
            pl.BlockSpec((tm, D_MODEL), lambda i: (i, 0)),
            pl.BlockSpec((tm, SSD_WIDTH), lambda i: (i, 0)),
            pl.BlockSpec((tm, S5_WIDTH), lambda i: (i, 0)),
            const(S5_WIDTH, S5_WIDTH), const(1, S5_WIDTH), const(1, S5_WIDTH),
            const(D_MODEL, D_MODEL),
        ],
        out_specs=pl.BlockSpec((tm, D_MODEL), lambda i: (i, 0)),
        out_shape=jax.ShapeDtypeStruct((t, D_MODEL), F32),
        compiler_params=_params("parallel"),
        name="out_proj",
    )(x, ya, y5, wglu, bglu, nrm, w)


def _castpad_kernel(x_ref, o_ref, *, nvalid):
    j = pl.program_id(0)

    @pl.when(j < nvalid)
    def _():
        o_ref[...] = x_ref[...].astype(BF16)

    @pl.when(j >= nvalid)
    def _():
        o_ref[...] = jnp.zeros_like(o_ref)


def _cast_pad(w, axis):
    nvalid = D_FF // LANES
    other = w.shape[1 - axis]
    if axis == 1:
        block, out_shape = (other, LANES), (other, D_FF_PAD)
        in_map, out_map = (lambda j: (0, jnp.minimum(j, nvalid - 1))), (lambda j: (0, j))
    else:
        block, out_shape = (LANES, other), (D_FF_PAD, other)
        in_map, out_map = (lambda j: (jnp.minimum(j, nvalid - 1), 0)), (lambda j: (j, 0))
    return pl.pallas_call(
        functools.partial(_castpad_kernel, nvalid=nvalid),
        grid=(D_FF_PAD // LANES,),
        in_specs=[pl.BlockSpec(block, in_map)],
        out_specs=pl.BlockSpec(block, out_map),
        out_shape=jax.ShapeDtypeStruct(out_shape, BF16),
        compiler_params=_params("parallel"),
        name="cast_pad",
    )(w)


def _prepare(p):
    w = {}
    for n in ("1", "2"):
        w["norm_ffn" + n] = p["norm_ffn" + n]
        w["wg" + n] = _cast_pad(p["w_ffn%s_gate" % n][0], 1)
        w["wu" + n] = _cast_pad(p["w_ffn%s_up" % n][0], 1)
        w["wd" + n] = _cast_pad(p["w_ffn%s_down" % n][0], 0)
    w_in = p["w_in"][0]
    c0, c1, c2 = SSD_WIDTH, SSD_WIDTH + CONV_DIM, SSD_WIDTH + CONV_DIM + SSD_HEADS
    w["w_in"] = jnp.concatenate(
        [w_in[:, c0:c1], w_in[:, :c0], w_in[:, c2:],
         jnp.pad(w_in[:, c1:c2], ((0, 0), (0, DT_PAD - SSD_HEADS)))], axis=1).astype(BF16)
    w["norm_mix"] = p["norm_mix"]
    w["conv_w"] = p["conv_w"][0]
    w["conv_b"] = p["conv_b"]
    w["dt_bias"] = jnp.pad(p["dt_bias"], ((0, 0), (0, DT_PAD - SSD_HEADS)))
    w["a_log"] = jnp.pad(p["a_log"], ((0, 0), (0, DT_PAD - SSD_HEADS)))
    w["d_ssd"] = jnp.repeat(p["d_ssd"][0], SSD_HEADDIM)[None]
    w["norm_ssd"] = p["norm_ssd"]

    lbr, lbi, bbr, bbi = _s5_prep(
        p["s5_lambda_re"][0], p["s5_lambda_im"][0],
        jnp.broadcast_to(p["s5_log_step"][0][:, None], (S5_GROUPS, S5_STATE)),
        jnp.transpose(p["s5_b_re"][0], (2, 0, 1)), jnp.transpose(p["s5_b_im"][0], (2, 0, 1)))
    eye = jnp.eye(S5_QB, dtype=F32)
    gl = S5_GROUPS // S5_QB

    def bd(bb):
        t = jnp.einsum("hqgn,gk->qghkn", bb.reshape(S5_GROUP_CH, S5_QB, gl, S5_STATE), eye)
        return t.reshape(S5_QB, LANES, gl * S5_STATE).astype(BF16)

    def cd(cc):
        t = jnp.einsum("qghn,gk->qgnkh", cc.reshape(S5_QB, gl, S5_GROUP_CH, S5_STATE), eye)
        return t.reshape(S5_QB, gl * S5_STATE, LANES).astype(BF16)

    w["bdr"], w["bdi"] = bd(bbr), bd(bbi)
    w["cdr"], w["cdi"] = cd(p["s5_c_re"][0]), cd(p["s5_c_im"][0])
    w["lbr"] = lbr.reshape(1, -1)
    w["lbi"] = lbi.reshape(1, -1)
    w["s5_d"] = p["s5_d"][0].reshape(1, -1)
    w["w_glu"] = p["w_glu"][0].astype(BF16)
    w["b_glu"] = p["b_glu"]
    w["norm_s5"] = p["norm_s5"]
    w["w_out"] = p["w_out"][0].astype(BF16)
    w["norm_final"] = p["norm_final"][None]
    return w


def _trunk(x, conv0, ssd0, s5r0, s5i0, w):
    bsz, l, _ = x.shape
    t = bsz * l
    x1 = _ffn(x.reshape(t, D_MODEL), w["norm_ffn1"], w["wg1"], w["wu1"], w["wd1"])
    ht0 = jnp.swapaxes(ssd0.reshape(bsz, SSD_WIDTH, SSD_STATE), 1, 2)
    if l % MIX_TM == 0:
        u5, y_ssd, conv_new, ht_new = _mix_in(x1, l, w, conv0, ht0)
        s5_in, s5_col = u5.reshape(bsz, l, S5_WIDTH), 0
    else:
        proj = _in_proj(x1, w["norm_mix"], w["w_in"]).reshape(bsz, l, PROJ_COLS)
        if l % CHUNK == 0:
            proj_ssd, valid = proj, CHUNK
        else:
            assert l < CHUNK
            proj_ssd, valid = jnp.pad(proj, ((0, 0), (0, CHUNK - l), (0, 0))), l
        y_ssd, conv_new, ht_new = _ssd(proj_ssd, valid, w["conv_w"], w["conv_b"], w["dt_bias"], w["a_log"],
                                       w["d_ssd"], w["norm_ssd"], conv0, ht0)
        y_ssd = y_ssd[:, :l].reshape(t, SSD_WIDTH)
        s5_in, s5_col = proj, (CONV_DIM + SSD_WIDTH) // S5_WIDTH
    ssd_new = jnp.swapaxes(ht_new, 1, 2).reshape(bsz, SSD_HEADS, SSD_HEADDIM, SSD_STATE)

    y_s5, s5r_new, s5i_new = _s5(s5_in, s5_col, w["bdr"], w["bdi"], w["cdr"], w["cdi"], w["lbr"], w["lbi"],
                                 w["s5_d"], s5r0.reshape(bsz, -1), s5i0.reshape(bsz, -1))

    x2 = _out_proj(x1, y_ssd, y_s5.reshape(t, S5_WIDTH),
                   w["w_glu"], w["b_glu"], w["norm_s5"], w["w_out"])
    y = _ffn(x2, w["norm_ffn2"], w["wg2"], w["wu2"], w["wd2"], final_g=w["norm_final"])
    return (y.reshape(bsz, l, D_MODEL), conv_new[None], ssd_new[None],
            s5r_new.reshape(1, bsz, S5_GROUPS, S5_STATE), s5i_new.reshape(1, bsz, S5_GROUPS, S5_STATE))


def kernel(x_prompt, x_sample, cache_conv, state_ssd, state_s5_re, state_s5_im, norm_ffn1, w_ffn1_gate, w_ffn1_up, w_ffn1_down, norm_mix, w_in, conv_w, conv_b, dt_bias, a_log, d_ssd, norm_ssd, s5_lambda_re, s5_lambda_im, s5_log_step, s5_b_re, s5_b_im, s5_c_re, s5_c_im, s5_d, w_glu, b_glu, norm_s5, w_out, norm_ffn2, w_ffn2_gate, w_ffn2_up, w_ffn2_down, norm_final):
    p = dict(norm_ffn1=norm_ffn1, w_ffn1_gate=w_ffn1_gate, w_ffn1_up=w_ffn1_up, w_ffn1_down=w_ffn1_down,
             norm_mix=norm_mix, w_in=w_in, conv_w=conv_w, conv_b=conv_b, dt_bias=dt_bias, a_log=a_log,
             d_ssd=d_ssd, norm_ssd=norm_ssd, s5_lambda_re=s5_lambda_re, s5_lambda_im=s5_lambda_im,
             s5_log_step=s5_log_step, s5_b_re=s5_b_re, s5_b_im=s5_b_im, s5_c_re=s5_c_re, s5_c_im=s5_c_im,
             s5_d=s5_d, w_glu=w_glu, b_glu=b_glu, norm_s5=norm_s5, w_out=w_out, norm_ffn2=norm_ffn2,
             w_ffn2_gate=w_ffn2_gate, w_ffn2_up=w_ffn2_up, w_ffn2_down=w_ffn2_down, norm_final=norm_final)
    w = _prepare(p)
    bsz = x_prompt.shape[0]
    zeros = lambda *s: jnp.zeros(s, F32)
    out_p = _trunk(x_prompt, zeros(bsz, CONV_W - 1, CONV_DIM),
                   zeros(bsz, SSD_HEADS, SSD_HEADDIM, SSD_STATE),
                   zeros(bsz, S5_GROUPS, S5_STATE), zeros(bsz, S5_GROUPS, S5_STATE), w)
    out_s = _trunk(x_sample, cache_conv[0], state_ssd[0], state_s5_re[0], state_s5_im[0], w)
    return (out_p[0], out_s[0]) + out_p[1:] + out_s[1:]
```

```python
import functools

import jax
import jax.numpy as jnp
from jax import lax
from jax.experimental import pallas as pl
from jax.experimental.pallas import tpu as pltpu

F32 = jnp.float32
BF16 = jnp.bfloat16
EPS = 1e-6

D_MODEL = 2048
D_FF = 5504
SSD_WIDTH = 1024
SSD_HEADDIM = 64
SSD_HEADS = 16
SSD_GROUPS = 4
SSD_STATE = 128
CONV_W = 4
CONV_DIM = SSD_WIDTH + 2 * SSD_GROUPS * SSD_STATE
S5_WIDTH = 1024
S5_GROUP_CH = 16
S5_GROUPS = 64
S5_STATE = 64
CHUNK = 64
SSD_NSUB = 4

LANES = 128
D_FF_PAD = 5632
FFN_TM = 1024
FFN_TF = 512
DT_PAD = LANES
PROJ_COLS = CONV_DIM + SSD_WIDTH + S5_WIDTH + DT_PAD
PROJ_TM = 512
MIX_TM = 512
PROJ_PIECE = 256
OUT_TM = 512
S5_NB = 8
S5_LC = 64
S5_SLABS = S5_GROUPS * S5_STATE // LANES
S5_QB = S5_WIDTH // LANES
S5_OUT_LAG = 2
CASTPAD_BLOCK = 256
VMEM_LIMIT = 56 * 1024 * 1024
MIX_VMEM_LIMIT = 60 * 1024 * 1024


def _rms(x, g):
    ms = jnp.mean(x * x, axis=-1, keepdims=True)
    return x * lax.rsqrt(ms + EPS) * g


def _silu(x):
    return x * jax.nn.sigmoid(x)


def _params(*sem, vmem_limit=VMEM_LIMIT):
    return pltpu.CompilerParams(dimension_semantics=sem, vmem_limit_bytes=vmem_limit)


def _ffn_kernel(x_ref, g_ref, wg_ref, wu_ref, wd_ref, gf_ref, o_ref, h_ref, *, final_norm):
    j = pl.program_id(1)

    @pl.when(j == 0)
    def _():
        x = x_ref[...]
        h_ref[...] = _rms(x, g_ref[...]).astype(BF16)
        o_ref[...] = x

    h = h_ref[...]
    gate = jnp.dot(h, wg_ref[...], preferred_element_type=F32)
    up = jnp.dot(h, wu_ref[...], preferred_element_type=F32)
    a = (0.5 * (_silu(gate) * up)).astype(BF16)
    o_ref[...] += jnp.dot(a, wd_ref[...], preferred_element_type=F32)

    if final_norm:
        @pl.when(j == pl.num_programs(1) - 1)
        def _():
            o_ref[...] = _rms(o_ref[...], gf_ref[...])


def _ffn(x, norm_g, wg, wu, wd, final_g=None):
    t = x.shape[0]
    tm = min(FFN_TM, t)
    kern = functools.partial(_ffn_kernel, final_norm=final_g is not None)
    gf = norm_g if final_g is None else final_g
    return pl.pallas_call(
        kern,
        grid=(t // tm, D_FF_PAD // FFN_TF),
        in_specs=[
            pl.BlockSpec((tm, D_MODEL), lambda i, j: (i, 0)),
            pl.BlockSpec((1, D_MODEL), lambda i, j: (0, 0)),
            pl.BlockSpec((D_MODEL, FFN_TF), lambda i, j: (0, j)),
            pl.BlockSpec((D_MODEL, FFN_TF), lambda i, j: (0, j)),
            pl.BlockSpec((FFN_TF, D_MODEL), lambda i, j: (j, 0)),
            pl.BlockSpec((1, D_MODEL), lambda i, j: (0, 0)),
        ],
        out_specs=pl.BlockSpec((tm, D_MODEL), lambda i, j: (i, 0)),
        out_shape=jax.ShapeDtypeStruct((t, D_MODEL), F32),
        scratch_shapes=[pltpu.VMEM((tm, D_MODEL), BF16)],
        compiler_params=_params("parallel", "arbitrary"),
        name="ffn",
    )(x, norm_g, wg, wu, wd, gf)


def _inproj_kernel(x_ref, g_ref, w_ref, o_ref):
    h = _rms(x_ref[...], g_ref[...]).astype(BF16)
    o_ref[...] = jnp.dot(h, w_ref[...], preferred_element_type=F32)


def _in_proj(x, norm_g, w):
    t = x.shape[0]
    tm = min(PROJ_TM, t)
    return pl.pallas_call(
        _inproj_kernel,
        grid=(t // tm,),
        in_specs=[
            pl.BlockSpec((tm, D_MODEL), lambda i: (i, 0)),
            pl.BlockSpec((1, D_MODEL), lambda i: (0, 0)),
            pl.BlockSpec((D_MODEL, PROJ_COLS), lambda i: (0, 0), pipeline_mode=pl.Buffered(1)),
        ],
        out_specs=pl.BlockSpec((tm, PROJ_COLS), lambda i: (i, 0)),
        out_shape=jax.ShapeDtypeStruct((t, PROJ_COLS), F32),
        compiler_params=_params("parallel"),
        name="in_proj",
    )(x, norm_g, w)


def _expand_heads(v):
    q = v.shape[0]
    lane = lax.broadcasted_iota(jnp.int32, (q, LANES), 1)
    blocks = []
    for k in range(SSD_WIDTH // LANES):
        a = jnp.broadcast_to(v[:, 2 * k:2 * k + 1], (q, LANES))
        b = jnp.broadcast_to(v[:, 2 * k + 1:2 * k + 2], (q, LANES))
        blocks.append(jnp.where(lane < SSD_HEADDIM, a, b))
    return jnp.concatenate(blocks, axis=1)


def _slab(s):
    return slice(s * LANES, (s + 1) * LANES)


def _ssd_chunk_pre(dtr_ref, cw_ref, cb_ref, dtb_ref, alog_ref, d_ref, pad_ref, zs_ref, *, base, valid):
    q = CHUNK
    hq = q // 2
    gw = SSD_WIDTH // SSD_GROUPS
    slab = _slab
    row_time = lambda r: ((r & (hq - 1)) << 1) | (r >> (hq.bit_length() - 1))

    def split_rows(ref3, s, start):
        return [ref3[s, pl.ds(start + e, hq, stride=2), :] for e in range(2)]

    cw = cw_ref[...]
    cb = cb_ref[...]
    cols = []
    for s in range(CONV_DIM // LANES):
        taps = [split_rows(pad_ref, s, base + 5 + k) for k in range(CONV_W)]
        halves = []
        for e in range(2):
            acc = cb[:, slab(s)] + taps[0][e] * cw[0:1, slab(s)]
            for k in range(1, CONV_W):
                acc = acc + taps[k][e] * cw[k:k + 1, slab(s)]
            halves.append(acc)
        cols.append(jnp.concatenate(halves, axis=0))
    act = _silu(jnp.concatenate(cols, axis=1))
    xs = act[:, :SSD_WIDTH]
    bm = act[:, SSD_WIDTH:SSD_WIDTH + SSD_GROUPS * SSD_STATE].astype(BF16)
    cm = act[:, SSD_WIDTH + SSD_GROUPS * SSD_STATE:].astype(BF16)

    dt_raw = jnp.concatenate([dtr_ref[0, pl.ds(base + e, hq, stride=2), :] for e in range(2)], axis=0)
    dtc = jax.nn.softplus(dt_raw + dtb_ref[...])
    if valid < q:
        dtc = jnp.where(row_time(lax.broadcasted_iota(jnp.int32, dtc.shape, 0)) < valid, dtc, 0.0)
    da = dtc * (-jnp.exp(alog_ref[...]))
    tril = (row_time(lax.broadcasted_iota(jnp.int32, (q, q), 1))
            <= row_time(lax.broadcasted_iota(jnp.int32, (q, q), 0))).astype(F32)
    acs_c = jnp.dot(tril, da, precision=lax.Precision.HIGHEST, preferred_element_type=F32)
    cbts = []
    for g in range(SSD_GROUPS):
        bg = bm[:, g * SSD_STATE:(g + 1) * SSD_STATE]
        cg = cm[:, g * SSD_STATE:(g + 1) * SSD_STATE]
        b4 = jnp.concatenate([bg] * (SSD_HEADS // SSD_GROUPS), axis=0)
        cbts.append(lax.dot_general(cg, b4, (((1,), (1,)), ((), ())), preferred_element_type=F32))
    zcols = [jnp.concatenate(split_rows(zs_ref, s, base), axis=0) for s in range(SSD_WIDTH // LANES)]
    gate = _silu(jnp.concatenate(zcols, axis=1))
    return dict(xs=xs, bm=bm, cm=cm, dtc=dtc, acs_c=acs_c, cbts=cbts, gate=gate, skip=d_ref[...] * xs)


def _ssd_chunk_post(pre):
    q = CHUNK
    hq = q // 2
    gw = SSD_WIDTH // SSD_GROUPS
    row_time = lambda r: ((r & (hq - 1)) << 1) | (r >> (hq.bit_length() - 1))
    xs, bm, cm = pre["xs"], pre["bm"], pre["cm"]
    dt_e = _expand_heads(pre["dtc"])
    acs_e = _expand_heads(pre["acs_c"])
    row = lax.broadcasted_iota(jnp.int32, (q, SSD_WIDTH), 0)
    jj = lax.broadcasted_iota(jnp.int32, (q, SSD_WIDTH), 1) & (SSD_HEADDIM - 1)
    acs_row = jnp.sum(jnp.where(jj == row, acs_e, 0.0), axis=0, keepdims=True)
    causal = row_time(jj) <= row_time(row)
    lmat = jnp.exp(jnp.where(causal, acs_e - acs_row, -jnp.inf))
    last = acs_e[q - 1:q, :]
    eacs = jnp.exp(acs_e)
    dec_end = jnp.exp(last - acs_e)
    cdec = jnp.exp(last)
    xdt = xs * dt_e
    xdt_b = xdt.astype(BF16)
    xd_b = (xdt * dec_end).astype(BF16)
    head_of_lane = lax.broadcasted_iota(jnp.int32, (q, gw), 1) >> 6

    ydiags, cgs, sts = [], [], []
    for g in range(SSD_GROUPS):
        sl = slice(g * gw, (g + 1) * gw)
        bg = bm[:, g * SSD_STATE:(g + 1) * SSD_STATE]
        cg = cm[:, g * SSD_STATE:(g + 1) * SSD_STATE]
        sg = (pre["cbts"][g] * lmat[:, sl]).astype(BF16)
        xg = xdt_b[:, sl]
        xbd = jnp.concatenate(
            [jnp.where(head_of_lane == hh, xg, jnp.zeros_like(xg)) for hh in range(SSD_HEADS // SSD_GROUPS)],
            axis=0)
        ydiags.append(jnp.dot(sg, xbd, preferred_element_type=F32))
        cgs.append(cg)
        sts.append(lax.dot_general(bg, xd_b[:, sl], (((0,), (0,)), ((), ())), preferred_element_type=F32))
    return dict(ydiags=ydiags, cgs=cgs, sts=sts, eacs=eacs, cdec=cdec, skip=pre["skip"], gate=pre["gate"])


def _ssd_chunk_state(loc, ht_ref):
    gw = SSD_WIDTH // SSD_GROUPS
    ys = []
    for g in range(SSD_GROUPS):
        sl = slice(g * gw, (g + 1) * gw)
        htg = ht_ref[0, :, sl]
        yoff = jnp.dot(loc["cgs"][g], htg.astype(BF16), preferred_element_type=F32)
        ys.append(loc["ydiags"][g] + loc["eacs"][:, sl] * yoff)
        ht_ref[0, :, sl] = loc["cdec"][:, sl] * htg + loc["sts"][g]
    return (jnp.concatenate(ys, axis=1) + loc["skip"]) * loc["gate"]


def _ssd_chunk_out(y, nrm_ref, y_ref, *, base):
    q = CHUNK
    hq = q // 2
    row_time = lambda r: ((r & (hq - 1)) << 1) | (r >> (hq.bit_length() - 1))
    yn = _rms(y, nrm_ref[...]).astype(BF16)
    unperm = (row_time(lax.broadcasted_iota(jnp.int32, (q, q), 1))
              == lax.broadcasted_iota(jnp.int32, (q, q), 0)).astype(BF16)
    y_ref[0, base:base + q, :] = jnp.dot(unperm, yn, preferred_element_type=F32).astype(BF16)


def _ssd_kernel(xbc_ref, z_ref, dtr_ref, cw_ref, cb_ref, dtb_ref, alog_ref, d_ref, nrm_ref,
                conv0_ref, h0_ref, y_ref, convo_ref, ht_ref, pad_ref, zs_ref, *, valid, nsub):
    c = pl.program_id(1)
    rows = nsub * CHUNK
    last = rows - CHUNK + valid

    @pl.when(c == 0)
    def _():
        for s in range(CONV_DIM // LANES):
            pad_ref[s, 5:8, :] = conv0_ref[0, :, _slab(s)]
        ht_ref[0] = h0_ref[0]

    for s in range(CONV_DIM // LANES):
        pad_ref[s, 8:8 + rows, :] = xbc_ref[0, :, _slab(s)]
    for s in range(SSD_WIDTH // LANES):
        zs_ref[s] = z_ref[0, :, _slab(s)]
    pres = [_ssd_chunk_pre(dtr_ref, cw_ref, cb_ref, dtb_ref, alog_ref, d_ref, pad_ref, zs_ref,
                           base=i * CHUNK, valid=valid if i == nsub - 1 else CHUNK) for i in range(nsub)]
    locs = [_ssd_chunk_post(pre) for pre in pres]
    ys = [_ssd_chunk_state(loc, ht_ref) for loc in locs]
    for i in range(nsub):
        _ssd_chunk_out(ys[i], nrm_ref, y_ref, base=i * CHUNK)
    for s in range(CONV_DIM // LANES):
        pad_ref[s, 5:8, :] = pad_ref[s, 8 + last - 3:8 + last, :]

    @pl.when(c == pl.num_programs(1) - 1)
    def _():
        for s in range(CONV_DIM // LANES):
            convo_ref[0, :, _slab(s)] = pad_ref[s, 5:8, :]


def _ssd(proj3, valid, cw, cb, dtb, alog, d_e, nrm, conv0, ht0):
    bsz, lp, _ = proj3.shape
    nsub = min(SSD_NSUB, lp // CHUNK)
    q = nsub * CHUNK
    kern = functools.partial(_ssd_kernel, valid=valid, nsub=nsub)
    row = lambda n: pl.BlockSpec((1, n), lambda b, c: (0, 0))
    return pl.pallas_call(
        kern,
        grid=(bsz, lp // q),
        in_specs=[
            pl.BlockSpec((1, q, CONV_DIM), lambda b, c: (b, c, 0)),
            pl.BlockSpec((1, q, SSD_WIDTH), lambda b, c: (b, c, CONV_DIM // SSD_WIDTH)),
            pl.BlockSpec((1, q, DT_PAD), lambda b, c: (b, c, (PROJ_COLS - DT_PAD) // DT_PAD)),
            pl.BlockSpec((CONV_W, CONV_DIM), lambda b, c: (0, 0)),
            row(CONV_DIM), row(DT_PAD), row(DT_PAD), row(SSD_WIDTH), row(SSD_WIDTH),
            pl.BlockSpec((1, CONV_W - 1, CONV_DIM), lambda b, c: (b, 0, 0)),
            pl.BlockSpec((1, SSD_STATE, SSD_WIDTH), lambda b, c: (b, 0, 0)),
        ],
        out_specs=[
            pl.BlockSpec((1, q, SSD_WIDTH), lambda b, c: (b, c, 0)),
            pl.BlockSpec((1, CONV_W - 1, CONV_DIM), lambda b, c: (b, 0, 0)),
            pl.BlockSpec((1, SSD_STATE, SSD_WIDTH), lambda b, c: (b, 0, 0)),
        ],
        out_shape=[
            jax.ShapeDtypeStruct((bsz, lp, SSD_WIDTH), BF16),
            jax.ShapeDtypeStruct((bsz, CONV_W - 1, CONV_DIM), F32),
            jax.ShapeDtypeStruct((bsz, SSD_STATE, SSD_WIDTH), F32),
        ],
        scratch_shapes=[pltpu.VMEM((CONV_DIM // LANES, 8 + q, LANES), F32),
                        pltpu.VMEM((SSD_WIDTH // LANES, q, LANES), F32)],
        compiler_params=_params("parallel", "arbitrary"),
        name="ssd",
    )(proj3, proj3, proj3, cw, cb, dtb, alog, d_e, nrm, conv0, ht0)


def _mix_in_kernel(x_ref, g_ref, w_ref, cw_ref, cb_ref, dtb_ref, alog_ref, d_ref, nrm_ref, conv0_ref, h0_ref,
                   u5_ref, y_ref, convo_ref, ht_ref, pad_ref, zs_ref, dts_ref, *, tiles_per_seq):
    i = pl.program_id(0)
    nsub = MIX_TM // CHUNK
    in_seq = lax.rem(jnp.maximum(i - 1, 0), tiles_per_seq)

    @pl.when(i == 0)
    def _():
        pad_ref[...] = jnp.zeros_like(pad_ref)
        zs_ref[...] = jnp.zeros_like(zs_ref)
        dts_ref[...] = jnp.zeros_like(dts_ref)
        ht_ref[...] = jnp.zeros_like(ht_ref)

    @pl.when(jnp.logical_and(i >= 1, in_seq == 0))
    def _():
        for s in range(CONV_DIM // LANES):
            pad_ref[s, 5:8, :] = conv0_ref[0, :, _slab(s)]
        ht_ref[0] = h0_ref[0]

    h = _rms(x_ref[...], g_ref[...]).astype(BF16)
    bounds = list(range(0, PROJ_COLS, PROJ_PIECE)) + [PROJ_COLS]
    npieces = len(bounds) - 1
    per_chunk = npieces // nsub
    pieces = []

    def project(n):
        pieces.append(jnp.dot(h, w_ref[:, bounds[n]:bounds[n + 1]], preferred_element_type=F32))

    pres, locs = [], []
    for c in range(nsub + 1):
        if c < nsub:
            pres.append(_ssd_chunk_pre(dts_ref, cw_ref, cb_ref, dtb_ref, alog_ref, d_ref, pad_ref, zs_ref,
                                       base=c * CHUNK, valid=CHUNK))
        if c >= 1:
            locs.append(_ssd_chunk_post(pres[c - 1]))
        for n in range(c * per_chunk, min((c + 1) * per_chunk, nsub * per_chunk)):
            project(n)
    ys = [_ssd_chunk_state(loc, ht_ref) for loc in locs]
    for n in range(nsub * per_chunk, npieces):
        project(n)
    for c in range(nsub):
        _ssd_chunk_out(ys[c], nrm_ref, y_ref, base=c * CHUNK)

    for s in range(CONV_DIM // LANES):
        pad_ref[s, 5:8, :] = pad_ref[s, 8 + MIX_TM - 3:8 + MIX_TM, :]

    def column_tile(c0):
        return pieces[c0 // PROJ_PIECE][:, c0 % PROJ_PIECE:c0 % PROJ_PIECE + LANES]

    for s in range(CONV_DIM // LANES):
        pad_ref[s, 8:8 + MIX_TM, :] = column_tile(s * LANES)
    for s in range(SSD_WIDTH // LANES):
        zs_ref[s] = column_tile(CONV_DIM + s * LANES)
        u5_ref[:, _slab(s)] = column_tile(CONV_DIM + SSD_WIDTH + s * LANES)
    dts_ref[0] = column_tile(PROJ_COLS - DT_PAD)

    @pl.when(jnp.logical_and(i >= 1, in_seq == tiles_per_seq - 1))
    def _():
        for s in range(CONV_DIM // LANES):
            convo_ref[0, :, _slab(s)] = pad_ref[s, 5:8, :]


def _mix_in(x1, l, w, conv0, ht0):
    t = x1.shape[0]
    bsz, nt, tps = t // l, t // MIX_TM, l // MIX_TM
    seq = lambda i: jnp.maximum(i - 1, 0) // tps
    row = lambda n: pl.BlockSpec((1, n), lambda i: (0, 0))
    u5, y, conv_new, ht_new = pl.pallas_call(
        functools.partial(_mix_in_kernel, tiles_per_seq=tps),
        grid=(nt + 1,),
        in_specs=[
            pl.BlockSpec((MIX_TM, D_MODEL), lambda i: (jnp.minimum(i, nt - 1), 0)),
            row(D_MODEL),
            pl.BlockSpec((D_MODEL, PROJ_COLS), lambda i: (0, 0), pipeline_mode=pl.Buffered(1)),
            pl.BlockSpec((CONV_W, CONV_DIM), lambda i: (0, 0)),
            row(CONV_DIM), row(DT_PAD), row(DT_PAD), row(SSD_WIDTH), row(SSD_WIDTH),
            pl.BlockSpec((1, CONV_W - 1, CONV_DIM), lambda i: (seq(i), 0, 0)),
            pl.BlockSpec((1, SSD_STATE, SSD_WIDTH), lambda i: (seq(i), 0, 0)),
        ],
        out_specs=[
            pl.BlockSpec((MIX_TM, S5_WIDTH), lambda i: (jnp.minimum(i, nt - 1), 0)),
            pl.BlockSpec((1, MIX_TM, SSD_WIDTH), lambda i: (jnp.maximum(i - 1, 0), 0, 0)),
            pl.BlockSpec((1, CONV_W - 1, CONV_DIM), lambda i: (seq(i), 0, 0)),
            pl.BlockSpec((1, SSD_STATE, SSD_WIDTH), lambda i: (seq(i), 0, 0)),
        ],
        out_shape=[
            jax.ShapeDtypeStruct((t, S5_WIDTH), F32),
            jax.ShapeDtypeStruct((nt, MIX_TM, SSD_WIDTH), BF16),
            jax.ShapeDtypeStruct((bsz, CONV_W - 1, CONV_DIM), F32),
            jax.ShapeDtypeStruct((bsz, SSD_STATE, SSD_WIDTH), F32),
        ],
        scratch_shapes=[pltpu.VMEM((CONV_DIM // LANES, 8 + MIX_TM, LANES), F32),
                        pltpu.VMEM((SSD_WIDTH // LANES, MIX_TM, LANES), F32),
                        pltpu.VMEM((1, MIX_TM, DT_PAD), F32)],
        compiler_params=_params("arbitrary", vmem_limit=MIX_VMEM_LIMIT),
        name="mix_in",
    )(x1, w["norm_mix"], w["w_in"], w["conv_w"], w["conv_b"], w["dt_bias"], w["a_log"], w["d_ssd"],
      w["norm_ssd"], conv0, ht0)
    return u5, y.reshape(t, SSD_WIDTH), conv_new, ht_new


def _s5_prep_kernel(lre_ref, lim_ref, lstep_ref, bre_ref, bim_ref, lbr_ref, lbi_ref, bbr_ref, bbi_ref):
    lam_re = lre_ref[...]
    lam_im = lim_ref[...]
    step = jnp.exp(lstep_ref[...])
    mag = jnp.exp(lam_re * step)
    ang = lam_im * step
    lb_re = mag * jnp.cos(ang)
    lb_im = mag * jnp.sin(ang)
    den = lam_re * lam_re + lam_im * lam_im
    nr = lb_re - 1.0
    q_re = (nr * lam_re + lb_im * lam_im) / den
    q_im = (lb_im * lam_re - nr * lam_im) / den
    lbr_ref[...] = lb_re
    lbi_ref[...] = lb_im
    b_re = bre_ref[...]
    b_im = bim_ref[...]
    bbr_ref[...] = q_re[None] * b_re - q_im[None] * b_im
    bbi_ref[...] = q_re[None] * b_im + q_im[None] * b_re


def _s5_prep(lam_re, lam_im, log_step_e, b_re_t, b_im_t):
    gn = jax.ShapeDtypeStruct((S5_GROUPS, S5_STATE), F32)
    hgn = jax.ShapeDtypeStruct((S5_GROUP_CH, S5_GROUPS, S5_STATE), F32)
    return pl.pallas_call(_s5_prep_kernel, out_shape=[gn, gn, hgn, hgn], name="s5_prep")(
        lam_re, lam_im, log_step_e, b_re_t, b_im_t)


def _s5_kernel(u_ref, bdr_ref, bdi_ref, cdr_ref, cdi_ref, lbr_ref, lbi_ref, d_ref, h0r_ref, h0i_ref,
               y_ref, hor_ref, hoi_ref, up_ref, yp_ref, *slab_refs, lc, pitch):
    c = pl.program_id(1)

    @pl.when(c == 0)
    def _():
        hor_ref[...] = h0r_ref[...]
        hoi_ref[...] = h0i_ref[...]
        up_ref[...] = jnp.zeros_like(up_ref)

    for b in range(S5_NB):
        up_ref[b * pitch:b * pitch + lc, :] = u_ref[b]
    u = up_ref[...]
    ub = u.astype(BF16)

    spb = S5_SLABS // S5_QB
    hr_refs, hi_refs = slab_refs[:S5_QB], slab_refs[S5_QB:]

    def input_stage(qb):
        uq = ub[:, qb * LANES:(qb + 1) * LANES]
        bu_r = jnp.dot(uq, bdr_ref[qb], preferred_element_type=F32)
        bu_i = jnp.dot(uq, bdi_ref[qb], preferred_element_type=F32)
        for k in range(spb):
            hr_refs[qb][k] = bu_r[:, k * LANES:(k + 1) * LANES]
            hi_refs[qb][k] = bu_i[:, k * LANES:(k + 1) * LANES]

    def scan_stage(qb):
        hr_ref, hi_ref = hr_refs[qb], hi_refs[qb]
        for k in range(spb):
            sl = slice((qb * spb + k) * LANES, (qb * spb + k + 1) * LANES)
            lr = jnp.broadcast_to(lbr_ref[:, sl], (S5_NB, LANES))
            li = jnp.broadcast_to(lbi_ref[:, sl], (S5_NB, LANES))
            h_r, h_i = hor_ref[:, sl], hoi_ref[:, sl]
            for t in range(lc):
                rows = pl.ds(t, S5_NB, stride=pitch)
                n_r = lr * h_r - li * h_i + hr_ref[k, rows, :]
                n_i = lr * h_i + li * h_r + hi_ref[k, rows, :]
                hr_ref[k, rows, :] = n_r
                hi_ref[k, rows, :] = n_i
                h_r, h_i = n_r, n_i
            hor_ref[:, sl] = h_r
            hoi_ref[:, sl] = h_i

    def output_stage(qb):
        ln = slice(qb * LANES, (qb + 1) * LANES)
        h_re = jnp.concatenate([hr_refs[qb][k] for k in range(spb)], axis=1).astype(BF16)
        h_im = jnp.concatenate([hi_refs[qb][k] for k in range(spb)], axis=1).astype(BF16)
        yq = (jnp.dot(h_re, cdr_ref[qb], preferred_element_type=F32)
              - jnp.dot(h_im, cdi_ref[qb], preferred_element_type=F32))
        yp_ref[:, ln] = yq + d_ref[:, ln] * u[:, ln]

    for step in range(S5_QB + S5_OUT_LAG):
        if step < S5_QB:
            input_stage(step)
        if 0 <= step - 1 < S5_QB:
            scan_stage(step - 1)
        if 0 <= step - S5_OUT_LAG < S5_QB:
            output_stage(step - S5_OUT_LAG)

    for b in range(S5_NB):
        y_ref[b] = yp_ref[b * pitch:b * pitch + lc, :]


def _s5(proj3, col, bdr, bdi, cdr, cdi, lbr, lbi, d5, h0r, h0i):
    bsz, l, _ = proj3.shape
    lc = min(S5_LC, l)
    pitch = lc + 4
    mp = S5_NB * pitch
    kern = functools.partial(_s5_kernel, lc=lc, pitch=pitch)
    ns = S5_GROUPS * S5_STATE
    full = lambda *shape: pl.BlockSpec(shape, lambda b, c: (0,) * len(shape))
    state = pl.BlockSpec((S5_NB, ns), lambda b, c: (b, 0))
    return pl.pallas_call(
        kern,
        grid=(bsz // S5_NB, l // lc),
        in_specs=[
            pl.BlockSpec((S5_NB, lc, S5_WIDTH), lambda b, c: (b, c, col)),
            full(S5_QB, LANES, 4 * LANES), full(S5_QB, LANES, 4 * LANES),
            full(S5_QB, 4 * LANES, LANES), full(S5_QB, 4 * LANES, LANES),
            full(1, ns), full(1, ns), full(1, S5_WIDTH),
            state, state,
        ],
        out_specs=[pl.BlockSpec((S5_NB, lc, S5_WIDTH), lambda b, c: (b, c, 0)), state, state],
        out_shape=[
            jax.ShapeDtypeStruct((bsz, l, S5_WIDTH), F32),
            jax.ShapeDtypeStruct((bsz, ns), F32),
            jax.ShapeDtypeStruct((bsz, ns), F32),
        ],
        scratch_shapes=[
            pltpu.VMEM((mp, S5_WIDTH), F32),
            pltpu.VMEM((mp, S5_WIDTH), F32),
        ] + [pltpu.VMEM((S5_SLABS // S5_QB, mp, LANES), F32)] * (2 * S5_QB),
        compiler_params=_params("parallel", "arbitrary"),
        name="s5",
    )(proj3, bdr, bdi, cdr, cdi, lbr, lbi, d5, h0r, h0i)


def _outproj_kernel(x_ref, ya_ref, y5_ref, wglu_ref, bglu_ref, nrm_ref, w_ref, o_ref):
    y5 = y5_ref[...]
    g = 0.5 * y5 * (1.0 + lax.erf(y5 * (0.5 ** 0.5)))
    gl = jnp.dot(g.astype(BF16), wglu_ref[...], preferred_element_type=F32) + bglu_ref[...]
    yb = _rms(g * jax.nn.sigmoid(gl), nrm_ref[...]).astype(BF16)
    o_ref[...] = (x_ref[...]
                  + jnp.dot(ya_ref[...], w_ref[:SSD_WIDTH, :], preferred_element_type=F32)
                  + jnp.dot(yb, w_ref[SSD_WIDTH:, :], preferred_element_type=F32))


def _out_proj(x, ya, y5, wglu, bglu, nrm, w):
    t = x.shape[0]
    tm = min(OUT_TM, t)
    const = lambda *shape: pl.BlockSpec(shape, lambda i: (0,) * len(shape), pipeline_mode=pl.Buffered(1))
    return pl.pallas_call(
        _outproj_kernel,
        grid=(t // tm,),
        in_specs=[
            pl.BlockSpec((tm, D_MODEL), lambda i: (i, 0)),
            pl.BlockSpec((tm, SSD_WIDTH), lambda i: (i, 0)),
            pl.BlockSpec((tm, S5_WIDTH), lambda i: (i, 0)),
            const(S5_WIDTH, S5_WIDTH), const(1, S5_WIDTH), const(1, S5_WIDTH),
            const(D_MODEL, D_MODEL),
        ],
        out_specs=pl.BlockSpec((tm, D_MODEL), lambda i: (i, 0)),
        out_shape=jax.ShapeDtypeStruct((t, D_MODEL), F32),
        compiler_params=_params("parallel"),
        name="out_proj",
    )(x, ya, y5, wglu, bglu, nrm, w)


def _castpad_kernel(x_ref, o_ref, *, axis):
    if axis == 1:
        o_ref[:, :D_FF] = x_ref[...].astype(BF16)
        o_ref[:, D_FF:] = jnp.zeros((o_ref.shape[0], D_FF_PAD - D_FF), BF16)
    else:
        o_ref[:D_FF, :] = x_ref[...].astype(BF16)
        o_ref[D_FF:, :] = jnp.zeros((D_FF_PAD - D_FF, o_ref.shape[1]), BF16)


def _cast_pad(w, axis):
    other = w.shape[1 - axis]
    if axis == 1:
        in_block, out_block, out_shape = (CASTPAD_BLOCK, D_FF), (CASTPAD_BLOCK, D_FF_PAD), (other, D_FF_PAD)
        index = lambda j: (j, 0)
    else:
        in_block, out_block, out_shape = (D_FF, CASTPAD_BLOCK), (D_FF_PAD, CASTPAD_BLOCK), (D_FF_PAD, other)
        index = lambda j: (0, j)
    return pl.pallas_call(
        functools.partial(_castpad_kernel, axis=axis),
        grid=(other // CASTPAD_BLOCK,),
        in_specs=[pl.BlockSpec(in_block, index)],
        out_specs=pl.BlockSpec(out_block, index),
        out_shape=jax.ShapeDtypeStruct(out_shape, BF16),
        compiler_params=_params("parallel"),
        name="cast_pad",
    )(w)


def _winprep_kernel(x_ref, o_ref):
    c0, c1, c2 = SSD_WIDTH, SSD_WIDTH + CONV_DIM, SSD_WIDTH + CONV_DIM + SSD_HEADS
    rows = o_ref.shape[0]
    o_ref[:, :CONV_DIM] = x_ref[:, c0:c1].astype(BF16)
    o_ref[:, CONV_DIM:CONV_DIM + SSD_WIDTH] = x_ref[:, :c0].astype(BF16)
    o_ref[:, CONV_DIM + SSD_WIDTH:PROJ_COLS - DT_PAD] = x_ref[:, c2:].astype(BF16)
    dt = jnp.concatenate([x_ref[:, c1:c2], jnp.zeros((rows, DT_PAD - SSD_HEADS), F32)], axis=1)
    o_ref[:, PROJ_COLS - DT_PAD:] = dt.astype(BF16)


def _prep_w_in(w_in):
    k, n = w_in.shape
    return pl.pallas_call(
        _winprep_kernel,
        grid=(k // CASTPAD_BLOCK,),
        in_specs=[pl.BlockSpec((CASTPAD_BLOCK, n), lambda j: (j, 0))],
        out_specs=pl.BlockSpec((CASTPAD_BLOCK, PROJ_COLS), lambda j: (j, 0)),
        out_shape=jax.ShapeDtypeStruct((k, PROJ_COLS), BF16),
        compiler_params=_params("parallel"),
        name="prep_w_in",
    )(w_in)


def _prepare(p):
    w = {}
    for n in ("1", "2"):
        w["norm_ffn" + n] = p["norm_ffn" + n]
        w["wg" + n] = _cast_pad(p["w_ffn%s_gate" % n][0], 1)
        w["wu" + n] = _cast_pad(p["w_ffn%s_up" % n][0], 1)
        w["wd" + n] = _cast_pad(p["w_ffn%s_down" % n][0], 0)
    w["w_in"] = _prep_w_in(p["w_in"][0])
    w["norm_mix"] = p["norm_mix"]
    w["conv_w"] = p["conv_w"][0]
    w["conv_b"] = p["conv_b"]
    w["dt_bias"] = jnp.pad(p["dt_bias"], ((0, 0), (0, DT_PAD - SSD_HEADS)))
    w["a_log"] = jnp.pad(p["a_log"], ((0, 0), (0, DT_PAD - SSD_HEADS)))
    w["d_ssd"] = jnp.repeat(p["d_ssd"][0], SSD_HEADDIM)[None]
    w["norm_ssd"] = p["norm_ssd"]

    lbr, lbi, bbr, bbi = _s5_prep(
        p["s5_lambda_re"][0], p["s5_lambda_im"][0],
        jnp.broadcast_to(p["s5_log_step"][0][:, None], (S5_GROUPS, S5_STATE)),
        jnp.transpose(p["s5_b_re"][0], (2, 0, 1)), jnp.transpose(p["s5_b_im"][0], (2, 0, 1)))
    eye = jnp.eye(S5_QB, dtype=F32)
    gl = S5_GROUPS // S5_QB

    def bd(bb):
        t = jnp.einsum("hqgn,gk->qghkn", bb.reshape(S5_GROUP_CH, S5_QB, gl, S5_STATE), eye)
        return t.reshape(S5_QB, LANES, gl * S5_STATE).astype(BF16)

    def cd(cc):
        t = jnp.einsum("qghn,gk->qgnkh", cc.reshape(S5_QB, gl, S5_GROUP_CH, S5_STATE), eye)
        return t.reshape(S5_QB, gl * S5_STATE, LANES).astype(BF16)

    w["bdr"], w["bdi"] = bd(bbr), bd(bbi)
    w["cdr"], w["cdi"] = cd(p["s5_c_re"][0]), cd(p["s5_c_im"][0])
    w["lbr"] = lbr.reshape(1, -1)
    w["lbi"] = lbi.reshape(1, -1)
    w["s5_d"] = p["s5_d"][0].reshape(1, -1)
    w["w_glu"] = p["w_glu"][0].astype(BF16)
    w["b_glu"] = p["b_glu"]
    w["norm_s5"] = p["norm_s5"]
    w["w_out"] = p["w_out"][0].astype(BF16)
    w["norm_final"] = p["norm_final"][None]
    return w


def _trunk(x, conv0, ssd0, s5r0, s5i0, w):
    bsz, l, _ = x.shape
    t = bsz * l
    x1 = _ffn(x.reshape(t, D_MODEL), w["norm_ffn1"], w["wg1"], w["wu1"], w["wd1"])
    ht0 = jnp.swapaxes(ssd0.reshape(bsz, SSD_WIDTH, SSD_STATE), 1, 2)
    if l % MIX_TM == 0:
        u5, y_ssd, conv_new, ht_new = _mix_in(x1, l, w, conv0, ht0)
        s5_in, s5_col = u5.reshape(bsz, l, S5_WIDTH), 0
    else:
        proj = _in_proj(x1, w["norm_mix"], w["w_in"]).reshape(bsz, l, PROJ_COLS)
        if l % CHUNK == 0:
            proj_ssd, valid = proj, CHUNK
        else:
            assert l < CHUNK
            proj_ssd, valid = jnp.pad(proj, ((0, 0), (0, CHUNK - l), (0, 0))), l
        y_ssd, conv_new, ht_new = _ssd(proj_ssd, valid, w["conv_w"], w["conv_b"], w["dt_bias"], w["a_log"],
                                       w["d_ssd"], w["norm_ssd"], conv0, ht0)
        y_ssd = y_ssd[:, :l].reshape(t, SSD_WIDTH)
        s5_in, s5_col = proj, (CONV_DIM + SSD_WIDTH) // S5_WIDTH
    ssd_new = jnp.swapaxes(ht_new, 1, 2).reshape(bsz, SSD_HEADS, SSD_HEADDIM, SSD_STATE)

    y_s5, s5r_new, s5i_new = _s5(s5_in, s5_col, w["bdr"], w["bdi"], w["cdr"], w["cdi"], w["lbr"], w["lbi"],
                                 w["s5_d"], s5r0.reshape(bsz, -1), s5i0.reshape(bsz, -1))

    x2 = _out_proj(x1, y_ssd, y_s5.reshape(t, S5_WIDTH),
                   w["w_glu"], w["b_glu"], w["norm_s5"], w["w_out"])
    y = _ffn(x2, w["norm_ffn2"], w["wg2"], w["wu2"], w["wd2"], final_g=w["norm_final"])
    return (y.reshape(bsz, l, D_MODEL), conv_new[None], ssd_new[None],
            s5r_new.reshape(1, bsz, S5_GROUPS, S5_STATE), s5i_new.reshape(1, bsz, S5_GROUPS, S5_STATE))


def kernel(x_prompt, x_sample, cache_conv, state_ssd, state_s5_re, state_s5_im, norm_ffn1, w_ffn1_gate, w_ffn1_up, w_ffn1_down, norm_mix, w_in, conv_w, conv_b, dt_bias, a_log, d_ssd, norm_ssd, s5_lambda_re, s5_lambda_im, s5_log_step, s5_b_re, s5_b_im, s5_c_re, s5_c_im, s5_d, w_glu, b_glu, norm_s5, w_out, norm_ffn2, w_ffn2_gate, w_ffn2_up, w_ffn2_down, norm_final):
    p = dict(norm_ffn1=norm_ffn1, w_ffn1_gate=w_ffn1_gate, w_ffn1_up=w_ffn1_up, w_ffn1_down=w_ffn1_down,
             norm_mix=norm_mix, w_in=w_in, conv_w=conv_w, conv_b=conv_b, dt_bias=dt_bias, a_log=a_log,
             d_ssd=d_ssd, norm_ssd=norm_ssd, s5_lambda_re=s5_lambda_re, s5_lambda_im=s5_lambda_im,
             s5_log_step=s5_log_step, s5_b_re=s5_b_re, s5_b_im=s5_b_im, s5_c_re=s5_c_re, s5_c_im=s5_c_im,
             s5_d=s5_d, w_glu=w_glu, b_glu=b_glu, norm_s5=norm_s5, w_out=w_out, norm_ffn2=norm_ffn2,
             w_ffn2_gate=w_ffn2_gate, w_ffn2_up=w_ffn2_up, w_ffn2_down=w_ffn2_down, norm_final=norm_final)
    w = _prepare(p)
    bsz = x_prompt.shape[0]
    zeros = lambda *s: jnp.zeros(s, F32)
    out_p = _trunk(x_prompt, zeros(bsz, CONV_W - 1, CONV_DIM),
                   zeros(bsz, SSD_HEADS, SSD_HEADDIM, SSD_STATE),
                   zeros(bsz, S5_GROUPS, S5_STATE), zeros(bsz, S5_GROUPS, S5_STATE), w)
    out_s = _trunk(x_sample, cache_conv[0], state_ssd[0], state_s5_re[0], state_s5_im[0], w)
    return (out_p[0], out_s[0]) + out_p[1:] + out_s[1:]
```

```python
import functools

import jax
import jax.numpy as jnp
from jax import lax
from jax.experimental import pallas as pl
from jax.experimental.pallas import tpu as pltpu

F32 = jnp.float32
BF16 = jnp.bfloat16
EPS = 1e-6

D_MODEL = 2048
D_FF = 5504
SSD_WIDTH = 1024
SSD_HEADDIM = 64
SSD_HEADS = 16
SSD_GROUPS = 4
SSD_STATE = 128
CONV_W = 4
CONV_DIM = SSD_WIDTH + 2 * SSD_GROUPS * SSD_STATE
S5_WIDTH = 1024
S5_GROUP_CH = 16
S5_GROUPS = 64
S5_STATE = 64
CHUNK = 64
SSD_NSUB = 4

LANES = 128
SUBLANES = 8
CONV_TAIL = CONV_W - 1
PAD_HEAD = SUBLANES
PAD_TAIL = slice(PAD_HEAD - CONV_TAIL, PAD_HEAD)
D_FF_PAD = 5632
FFN_TM = 1024
FFN_TF = 512
FFN_TF_SMALL = D_FF_PAD // 4
DT_PAD = LANES
PROJ_COLS = CONV_DIM + SSD_WIDTH + S5_WIDTH + DT_PAD
PROJ_TM = 512
MIX_TM = 512
PROJ_PIECE = 256
OUT_TM = 512
S5_NB = 8
S5_LC = 64
S5_SLABS = S5_GROUPS * S5_STATE // LANES
S5_QB = S5_WIDTH // LANES
S5_OUT_LAG = 2
CASTPAD_BLOCK = 256
VMEM_LIMIT = 56 * 1024 * 1024
MIX_VMEM_LIMIT = 60 * 1024 * 1024


def _rms(x, g):
    ms = jnp.mean(x * x, axis=-1, keepdims=True)
    return x * lax.rsqrt(ms + EPS) * g


def _silu(x):
    return x * jax.nn.sigmoid(x)


def _params(*sem, vmem_limit=VMEM_LIMIT):
    return pltpu.CompilerParams(dimension_semantics=sem, vmem_limit_bytes=vmem_limit)


def _ffn_kernel(x_ref, g_ref, wg_ref, wu_ref, wd_ref, gf_ref, o_ref, h_ref, *, final_norm):
    j = pl.program_id(1)

    @pl.when(j == 0)
    def _():
        x = x_ref[...]
        h_ref[...] = _rms(x, g_ref[...]).astype(BF16)
        o_ref[...] = x

    h = h_ref[...]
    gate = jnp.dot(h, wg_ref[...], preferred_element_type=F32)
    up = jnp.dot(h, wu_ref[...], preferred_element_type=F32)
    a = (0.5 * (_silu(gate) * up)).astype(BF16)
    o_ref[...] += jnp.dot(a, wd_ref[...], preferred_element_type=F32)

    if final_norm:
        @pl.when(j == pl.num_programs(1) - 1)
        def _():
            o_ref[...] = _rms(o_ref[...], gf_ref[...])


def _ffn(x, norm_g, wg, wu, wd, final_g=None):
    t = x.shape[0]
    tm = min(FFN_TM, t)
    tf = FFN_TF if t >= FFN_TM else FFN_TF_SMALL
    kern = functools.partial(_ffn_kernel, final_norm=final_g is not None)
    gf = norm_g if final_g is None else final_g
    return pl.pallas_call(
        kern,
        grid=(t // tm, D_FF_PAD // tf),
        in_specs=[
            pl.BlockSpec((tm, D_MODEL), lambda i, j: (i, 0)),
            pl.BlockSpec((1, D_MODEL), lambda i, j: (0, 0)),
            pl.BlockSpec((D_MODEL, tf), lambda i, j: (0, j)),
            pl.BlockSpec((D_MODEL, tf), lambda i, j: (0, j)),
            pl.BlockSpec((tf, D_MODEL), lambda i, j: (j, 0)),
            pl.BlockSpec((1, D_MODEL), lambda i, j: (0, 0)),
        ],
        out_specs=pl.BlockSpec((tm, D_MODEL), lambda i, j: (i, 0)),
        out_shape=jax.ShapeDtypeStruct((t, D_MODEL), F32),
        scratch_shapes=[pltpu.VMEM((tm, D_MODEL), BF16)],
        compiler_params=_params("parallel", "arbitrary"),
        name="ffn",
    )(x, norm_g, wg, wu, wd, gf)


def _inproj_kernel(x_ref, g_ref, w_ref, o_ref):
    h = _rms(x_ref[...], g_ref[...]).astype(BF16)
    o_ref[...] = jnp.dot(h, w_ref[...], preferred_element_type=F32)


def _in_proj(x, norm_g, w):
    t = x.shape[0]
    tm = min(PROJ_TM, t)
    return pl.pallas_call(
        _inproj_kernel,
        grid=(t // tm,),
        in_specs=[
            pl.BlockSpec((tm, D_MODEL), lambda i: (i, 0)),
            pl.BlockSpec((1, D_MODEL), lambda i: (0, 0)),
            pl.BlockSpec((D_MODEL, PROJ_COLS), lambda i: (0, 0), pipeline_mode=pl.Buffered(1)),
        ],
        out_specs=pl.BlockSpec((tm, PROJ_COLS), lambda i: (i, 0)),
        out_shape=jax.ShapeDtypeStruct((t, PROJ_COLS), F32),
        compiler_params=_params("parallel"),
        name="in_proj",
    )(x, norm_g, w)


def _expand_heads(v):
    q = v.shape[0]
    lane = lax.broadcasted_iota(jnp.int32, (q, LANES), 1)
    blocks = []
    for k in range(SSD_WIDTH // LANES):
        a = jnp.broadcast_to(v[:, 2 * k:2 * k + 1], (q, LANES))
        b = jnp.broadcast_to(v[:, 2 * k + 1:2 * k + 2], (q, LANES))
        blocks.append(jnp.where(lane < SSD_HEADDIM, a, b))
    return jnp.concatenate(blocks, axis=1)


def _slab(s):
    return slice(s * LANES, (s + 1) * LANES)


def _ssd_chunk_pre(dtr_ref, cw_ref, cb_ref, dtb_ref, alog_ref, d_ref, pad_ref, zs_ref, *, base, valid):
    q = CHUNK
    hq = q // 2
    gw = SSD_WIDTH // SSD_GROUPS
    slab = _slab
    row_time = lambda r: ((r & (hq - 1)) << 1) | (r >> (hq.bit_length() - 1))

    def split_rows(ref3, s, start):
        return [ref3[s, pl.ds(start + e, hq, stride=2), :] for e in range(2)]

    cw = cw_ref[...]
    cb = cb_ref[...]
    cols = []
    for s in range(CONV_DIM // LANES):
        taps = [split_rows(pad_ref, s, base + PAD_HEAD - CONV_TAIL + k) for k in range(CONV_W)]
        halves = []
        for e in range(2):
            acc = cb[:, slab(s)] + taps[0][e] * cw[0:1, slab(s)]
            for k in range(1, CONV_W):
                acc = acc + taps[k][e] * cw[k:k + 1, slab(s)]
            halves.append(acc)
        cols.append(jnp.concatenate(halves, axis=0))
    act = _silu(jnp.concatenate(cols, axis=1))
    xs = act[:, :SSD_WIDTH]
    bm = act[:, SSD_WIDTH:SSD_WIDTH + SSD_GROUPS * SSD_STATE].astype(BF16)
    cm = act[:, SSD_WIDTH + SSD_GROUPS * SSD_STATE:].astype(BF16)

    dt_raw = jnp.concatenate([dtr_ref[0, pl.ds(base + e, hq, stride=2), :] for e in range(2)], axis=0)
    dtc = jax.nn.softplus(dt_raw + dtb_ref[...])
    if valid < q:
        dtc = jnp.where(row_time(lax.broadcasted_iota(jnp.int32, dtc.shape, 0)) < valid, dtc, 0.0)
    da = dtc * (-jnp.exp(alog_ref[...]))
    tril = (row_time(lax.broadcasted_iota(jnp.int32, (q, q), 1))
            <= row_time(lax.broadcasted_iota(jnp.int32, (q, q), 0))).astype(F32)
    acs_c = jnp.dot(tril, da, precision=lax.Precision.HIGHEST, preferred_element_type=F32)
    cbts = []
    for g in range(SSD_GROUPS):
        bg = bm[:, g * SSD_STATE:(g + 1) * SSD_STATE]
        cg = cm[:, g * SSD_STATE:(g + 1) * SSD_STATE]
        b4 = jnp.concatenate([bg] * (SSD_HEADS // SSD_GROUPS), axis=0)
        cbts.append(lax.dot_general(cg, b4, (((1,), (1,)), ((), ())), preferred_element_type=F32))
    zcols = [jnp.concatenate(split_rows(zs_ref, s, base), axis=0) for s in range(SSD_WIDTH // LANES)]
    gate = _silu(jnp.concatenate(zcols, axis=1))
    return dict(xs=xs, bm=bm, cm=cm, dtc=dtc, acs_c=acs_c, cbts=cbts, gate=gate, skip=d_ref[...] * xs)


def _ssd_chunk_post(pre):
    q = CHUNK
    hq = q // 2
    gw = SSD_WIDTH // SSD_GROUPS
    row_time = lambda r: ((r & (hq - 1)) << 1) | (r >> (hq.bit_length() - 1))
    xs, bm, cm = pre["xs"], pre["bm"], pre["cm"]
    dt_e = _expand_heads(pre["dtc"])
    acs_e = _expand_heads(pre["acs_c"])
    row = lax.broadcasted_iota(jnp.int32, (q, SSD_WIDTH), 0)
    jj = lax.broadcasted_iota(jnp.int32, (q, SSD_WIDTH), 1) & (SSD_HEADDIM - 1)
    acs_row = jnp.sum(jnp.where(jj == row, acs_e, 0.0), axis=0, keepdims=True)
    causal = row_time(jj) <= row_time(row)
    lmat = jnp.exp(jnp.where(causal, acs_e - acs_row, -jnp.inf))
    last = acs_e[q - 1:q, :]
    eacs = jnp.exp(acs_e)
    dec_end = jnp.exp(last - acs_e)
    cdec = jnp.exp(last)
    xdt = xs * dt_e
    xdt_b = xdt.astype(BF16)
    xd_b = (xdt * dec_end).astype(BF16)
    head_of_lane = lax.broadcasted_iota(jnp.int32, (q, gw), 1) // SSD_HEADDIM

    ydiags, cgs, sts = [], [], []
    for g in range(SSD_GROUPS):
        sl = slice(g * gw, (g + 1) * gw)
        bg = bm[:, g * SSD_STATE:(g + 1) * SSD_STATE]
        cg = cm[:, g * SSD_STATE:(g + 1) * SSD_STATE]
        sg = (pre["cbts"][g] * lmat[:, sl]).astype(BF16)
        xg = xdt_b[:, sl]
        xbd = jnp.concatenate(
            [jnp.where(head_of_lane == hh, xg, jnp.zeros_like(xg)) for hh in range(SSD_HEADS // SSD_GROUPS)],
            axis=0)
        ydiags.append(jnp.dot(sg, xbd, preferred_element_type=F32))
        cgs.append(cg)
        sts.append(lax.dot_general(bg, xd_b[:, sl], (((0,), (0,)), ((), ())), preferred_element_type=F32))
    return dict(ydiags=ydiags, cgs=cgs, sts=sts, eacs=eacs, cdec=cdec, skip=pre["skip"], gate=pre["gate"])


def _ssd_chunk_state(loc, ht_ref):
    gw = SSD_WIDTH // SSD_GROUPS
    ys = []
    for g in range(SSD_GROUPS):
        sl = slice(g * gw, (g + 1) * gw)
        htg = ht_ref[0, :, sl]
        yoff = jnp.dot(loc["cgs"][g], htg.astype(BF16), preferred_element_type=F32)
        ys.append(loc["ydiags"][g] + loc["eacs"][:, sl] * yoff)
        ht_ref[0, :, sl] = loc["cdec"][:, sl] * htg + loc["sts"][g]
    return (jnp.concatenate(ys, axis=1) + loc["skip"]) * loc["gate"]


def _ssd_chunk_out(y, nrm_ref, y_ref, *, base):
    q = CHUNK
    hq = q // 2
    row_time = lambda r: ((r & (hq - 1)) << 1) | (r >> (hq.bit_length() - 1))
    yn = _rms(y, nrm_ref[...]).astype(BF16)
    unperm = (row_time(lax.broadcasted_iota(jnp.int32, (q, q), 1))
              == lax.broadcasted_iota(jnp.int32, (q, q), 0)).astype(BF16)
    y_ref[0, base:base + q, :] = jnp.dot(unperm, yn, preferred_element_type=F32).astype(BF16)


def _ssd_kernel(xbc_ref, z_ref, dtr_ref, cw_ref, cb_ref, dtb_ref, alog_ref, d_ref, nrm_ref,
                conv0_ref, h0_ref, y_ref, convo_ref, ht_ref, pad_ref, zs_ref, *, valid, nsub):
    c = pl.program_id(1)
    rows = nsub * CHUNK
    last = rows - CHUNK + valid

    @pl.when(c == 0)
    def _():
        for s in range(CONV_DIM // LANES):
            pad_ref[s, PAD_TAIL, :] = conv0_ref[0, :, _slab(s)]
        ht_ref[0] = h0_ref[0]

    for s in range(CONV_DIM // LANES):
        pad_ref[s, PAD_HEAD:PAD_HEAD + rows, :] = xbc_ref[0, :, _slab(s)]
    for s in range(SSD_WIDTH // LANES):
        zs_ref[s] = z_ref[0, :, _slab(s)]
    pres = [_ssd_chunk_pre(dtr_ref, cw_ref, cb_ref, dtb_ref, alog_ref, d_ref, pad_ref, zs_ref,
                           base=i * CHUNK, valid=valid if i == nsub - 1 else CHUNK) for i in range(nsub)]
    locs = [_ssd_chunk_post(pre) for pre in pres]
    ys = [_ssd_chunk_state(loc, ht_ref) for loc in locs]
    for i in range(nsub):
        _ssd_chunk_out(ys[i], nrm_ref, y_ref, base=i * CHUNK)
    for s in range(CONV_DIM // LANES):
        pad_ref[s, PAD_TAIL, :] = pad_ref[s, PAD_HEAD + last - CONV_TAIL:PAD_HEAD + last, :]

    @pl.when(c == pl.num_programs(1) - 1)
    def _():
        for s in range(CONV_DIM // LANES):
            convo_ref[0, :, _slab(s)] = pad_ref[s, PAD_TAIL, :]


def _ssd(proj3, valid, cw, cb, dtb, alog, d_e, nrm, conv0, ht0):
    bsz, lp, _ = proj3.shape
    nsub = min(SSD_NSUB, lp // CHUNK)
    q = nsub * CHUNK
    kern = functools.partial(_ssd_kernel, valid=valid, nsub=nsub)
    row = lambda n: pl.BlockSpec((1, n), lambda b, c: (0, 0))
    return pl.pallas_call(
        kern,
        grid=(bsz, lp // q),
        in_specs=[
            pl.BlockSpec((1, q, CONV_DIM), lambda b, c: (b, c, 0)),
            pl.BlockSpec((1, q, SSD_WIDTH), lambda b, c: (b, c, CONV_DIM // SSD_WIDTH)),
            pl.BlockSpec((1, q, DT_PAD), lambda b, c: (b, c, (PROJ_COLS - DT_PAD) // DT_PAD)),
            pl.BlockSpec((CONV_W, CONV_DIM), lambda b, c: (0, 0)),
            row(CONV_DIM), row(DT_PAD), row(DT_PAD), row(SSD_WIDTH), row(SSD_WIDTH),
            pl.BlockSpec((1, CONV_W - 1, CONV_DIM), lambda b, c: (b, 0, 0)),
            pl.BlockSpec((1, SSD_STATE, SSD_WIDTH), lambda b, c: (b, 0, 0)),
        ],
        out_specs=[
            pl.BlockSpec((1, q, SSD_WIDTH), lambda b, c: (b, c, 0)),
            pl.BlockSpec((1, CONV_W - 1, CONV_DIM), lambda b, c: (b, 0, 0)),
            pl.BlockSpec((1, SSD_STATE, SSD_WIDTH), lambda b, c: (b, 0, 0)),
        ],
        out_shape=[
            jax.ShapeDtypeStruct((bsz, lp, SSD_WIDTH), BF16),
            jax.ShapeDtypeStruct((bsz, CONV_W - 1, CONV_DIM), F32),
            jax.ShapeDtypeStruct((bsz, SSD_STATE, SSD_WIDTH), F32),
        ],
        scratch_shapes=[pltpu.VMEM((CONV_DIM // LANES, PAD_HEAD + q, LANES), F32),
                        pltpu.VMEM((SSD_WIDTH // LANES, q, LANES), F32)],
        compiler_params=_params("parallel", "arbitrary"),
        name="ssd",
    )(proj3, proj3, proj3, cw, cb, dtb, alog, d_e, nrm, conv0, ht0)


def _mix_in_kernel(x_ref, g_ref, w_ref, cw_ref, cb_ref, dtb_ref, alog_ref, d_ref, nrm_ref, conv0_ref, h0_ref,
                   u5_ref, y_ref, convo_ref, ht_ref, pad_ref, zs_ref, dts_ref, *, tiles_per_seq):
    i = pl.program_id(0)
    nsub = MIX_TM // CHUNK
    in_seq = lax.rem(jnp.maximum(i - 1, 0), tiles_per_seq)

    @pl.when(i == 0)
    def _():
        pad_ref[...] = jnp.zeros_like(pad_ref)
        zs_ref[...] = jnp.zeros_like(zs_ref)
        dts_ref[...] = jnp.zeros_like(dts_ref)
        ht_ref[...] = jnp.zeros_like(ht_ref)

    @pl.when(jnp.logical_and(i >= 1, in_seq == 0))
    def _():
        for s in range(CONV_DIM // LANES):
            pad_ref[s, PAD_TAIL, :] = conv0_ref[0, :, _slab(s)]
        ht_ref[0] = h0_ref[0]

    h = _rms(x_ref[...], g_ref[...]).astype(BF16)
    bounds = list(range(0, PROJ_COLS, PROJ_PIECE)) + [PROJ_COLS]
    npieces = len(bounds) - 1
    per_chunk = npieces // nsub
    pieces = []

    def project(n):
        pieces.append(jnp.dot(h, w_ref[:, bounds[n]:bounds[n + 1]], preferred_element_type=F32))

    pres, locs = [], []
    for c in range(nsub + 1):
        if c < nsub:
            pres.append(_ssd_chunk_pre(dts_ref, cw_ref, cb_ref, dtb_ref, alog_ref, d_ref, pad_ref, zs_ref,
                                       base=c * CHUNK, valid=CHUNK))
        if c >= 1:
            locs.append(_ssd_chunk_post(pres[c - 1]))
        for n in range(c * per_chunk, min((c + 1) * per_chunk, nsub * per_chunk)):
            project(n)
    ys = [_ssd_chunk_state(loc, ht_ref) for loc in locs]
    for n in range(nsub * per_chunk, npieces):
        project(n)
    for c in range(nsub):
        _ssd_chunk_out(ys[c], nrm_ref, y_ref, base=c * CHUNK)

    for s in range(CONV_DIM // LANES):
        pad_ref[s, PAD_TAIL, :] = pad_ref[s, PAD_HEAD + MIX_TM - CONV_TAIL:PAD_HEAD + MIX_TM, :]

    def column_tile(c0):
        return pieces[c0 // PROJ_PIECE][:, c0 % PROJ_PIECE:c0 % PROJ_PIECE + LANES]

    for s in range(CONV_DIM // LANES):
        pad_ref[s, PAD_HEAD:PAD_HEAD + MIX_TM, :] = column_tile(s * LANES)
    for s in range(SSD_WIDTH // LANES):
        zs_ref[s] = column_tile(CONV_DIM + s * LANES)
        u5_ref[:, _slab(s)] = column_tile(CONV_DIM + SSD_WIDTH + s * LANES)
    dts_ref[0] = column_tile(PROJ_COLS - DT_PAD)

    @pl.when(jnp.logical_and(i >= 1, in_seq == tiles_per_seq - 1))
    def _():
        for s in range(CONV_DIM // LANES):
            convo_ref[0, :, _slab(s)] = pad_ref[s, PAD_TAIL, :]


def _mix_in(x1, l, w, conv0, ht0):
    t = x1.shape[0]
    bsz, nt, tps = t // l, t // MIX_TM, l // MIX_TM
    seq = lambda i: jnp.maximum(i - 1, 0) // tps
    row = lambda n: pl.BlockSpec((1, n), lambda i: (0, 0))
    u5, y, conv_new, ht_new = pl.pallas_call(
        functools.partial(_mix_in_kernel, tiles_per_seq=tps),
        grid=(nt + 1,),
        in_specs=[
            pl.BlockSpec((MIX_TM, D_MODEL), lambda i: (jnp.minimum(i, nt - 1), 0)),
            row(D_MODEL),
            pl.BlockSpec((D_MODEL, PROJ_COLS), lambda i: (0, 0), pipeline_mode=pl.Buffered(1)),
            pl.BlockSpec((CONV_W, CONV_DIM), lambda i: (0, 0)),
            row(CONV_DIM), row(DT_PAD), row(DT_PAD), row(SSD_WIDTH), row(SSD_WIDTH),
            pl.BlockSpec((1, CONV_W - 1, CONV_DIM), lambda i: (seq(i), 0, 0)),
            pl.BlockSpec((1, SSD_STATE, SSD_WIDTH), lambda i: (seq(i), 0, 0)),
        ],
        out_specs=[
            pl.BlockSpec((MIX_TM, S5_WIDTH), lambda i: (jnp.minimum(i, nt - 1), 0)),
            pl.BlockSpec((1, MIX_TM, SSD_WIDTH), lambda i: (jnp.maximum(i - 1, 0), 0, 0)),
            pl.BlockSpec((1, CONV_W - 1, CONV_DIM), lambda i: (seq(i), 0, 0)),
            pl.BlockSpec((1, SSD_STATE, SSD_WIDTH), lambda i: (seq(i), 0, 0)),
        ],
        out_shape=[
            jax.ShapeDtypeStruct((t, S5_WIDTH), F32),
            jax.ShapeDtypeStruct((nt, MIX_TM, SSD_WIDTH), BF16),
            jax.ShapeDtypeStruct((bsz, CONV_W - 1, CONV_DIM), F32),
            jax.ShapeDtypeStruct((bsz, SSD_STATE, SSD_WIDTH), F32),
        ],
        scratch_shapes=[pltpu.VMEM((CONV_DIM // LANES, PAD_HEAD + MIX_TM, LANES), F32),
                        pltpu.VMEM((SSD_WIDTH // LANES, MIX_TM, LANES), F32),
                        pltpu.VMEM((1, MIX_TM, DT_PAD), F32)],
        compiler_params=_params("arbitrary", vmem_limit=MIX_VMEM_LIMIT),
        name="mix_in",
    )(x1, w["norm_mix"], w["w_in"], w["conv_w"], w["conv_b"], w["dt_bias"], w["a_log"], w["d_ssd"],
      w["norm_ssd"], conv0, ht0)
    return u5, y.reshape(t, SSD_WIDTH), conv_new, ht_new


def _s5_prep_kernel(lre_ref, lim_ref, lstep_ref, bre_ref, bim_ref, lbr_ref, lbi_ref, bbr_ref, bbi_ref):
    lam_re = lre_ref[...]
    lam_im = lim_ref[...]
    step = jnp.exp(lstep_ref[...])
    mag = jnp.exp(lam_re * step)
    ang = lam_im * step
    lb_re = mag * jnp.cos(ang)
    lb_im = mag * jnp.sin(ang)
    den = lam_re * lam_re + lam_im * lam_im
    nr = lb_re - 1.0
    q_re = (nr * lam_re + lb_im * lam_im) / den
    q_im = (lb_im * lam_re - nr * lam_im) / den
    lbr_ref[...] = lb_re
    lbi_ref[...] = lb_im
    b_re = bre_ref[...]
    b_im = bim_ref[...]
    bbr_ref[...] = q_re[None] * b_re - q_im[None] * b_im
    bbi_ref[...] = q_re[None] * b_im + q_im[None] * b_re


def _s5_prep(lam_re, lam_im, log_step_e, b_re_t, b_im_t):
    gn = jax.ShapeDtypeStruct((S5_GROUPS, S5_STATE), F32)
    hgn = jax.ShapeDtypeStruct((S5_GROUP_CH, S5_GROUPS, S5_STATE), F32)
    return pl.pallas_call(_s5_prep_kernel, out_shape=[gn, gn, hgn, hgn], name="s5_prep")(
        lam_re, lam_im, log_step_e, b_re_t, b_im_t)


def _s5_kernel(u_ref, bdr_ref, bdi_ref, cdr_ref, cdi_ref, lbr_ref, lbi_ref, d_ref, h0r_ref, h0i_ref,
               y_ref, hor_ref, hoi_ref, up_ref, yp_ref, *slab_refs, lc, pitch):
    c = pl.program_id(1)

    @pl.when(c == 0)
    def _():
        hor_ref[...] = h0r_ref[...]
        hoi_ref[...] = h0i_ref[...]
        up_ref[...] = jnp.zeros_like(up_ref)

    for b in range(S5_NB):
        up_ref[b * pitch:b * pitch + lc, :] = u_ref[b]
    u = up_ref[...]
    ub = u.astype(BF16)

    spb = S5_SLABS // S5_QB
    hr_refs, hi_refs = slab_refs[:S5_QB], slab_refs[S5_QB:]

    def input_stage(qb):
        uq = ub[:, qb * LANES:(qb + 1) * LANES]
        bu_r = jnp.dot(uq, bdr_ref[qb], preferred_element_type=F32)
        bu_i = jnp.dot(uq, bdi_ref[qb], preferred_element_type=F32)
        for k in range(spb):
            hr_refs[qb][k] = bu_r[:, k * LANES:(k + 1) * LANES]
            hi_refs[qb][k] = bu_i[:, k * LANES:(k + 1) * LANES]

    def scan_stage(qb):
        hr_ref, hi_ref = hr_refs[qb], hi_refs[qb]
        for k in range(spb):
            sl = slice((qb * spb + k) * LANES, (qb * spb + k + 1) * LANES)
            lr = jnp.broadcast_to(lbr_ref[:, sl], (S5_NB, LANES))
            li = jnp.broadcast_to(lbi_ref[:, sl], (S5_NB, LANES))
            h_r, h_i = hor_ref[:, sl], hoi_ref[:, sl]
            for t in range(lc):
                rows = pl.ds(t, S5_NB, stride=pitch)
                n_r = lr * h_r - li * h_i + hr_ref[k, rows, :]
                n_i = lr * h_i + li * h_r + hi_ref[k, rows, :]
                hr_ref[k, rows, :] = n_r
                hi_ref[k, rows, :] = n_i
                h_r, h_i = n_r, n_i
            hor_ref[:, sl] = h_r
            hoi_ref[:, sl] = h_i

    def output_stage(qb):
        ln = slice(qb * LANES, (qb + 1) * LANES)
        h_re = jnp.concatenate([hr_refs[qb][k] for k in range(spb)], axis=1).astype(BF16)
        h_im = jnp.concatenate([hi_refs[qb][k] for k in range(spb)], axis=1).astype(BF16)
        yq = (jnp.dot(h_re, cdr_ref[qb], preferred_element_type=F32)
              - jnp.dot(h_im, cdi_ref[qb], preferred_element_type=F32))
        yp_ref[:, ln] = yq + d_ref[:, ln] * u[:, ln]

    for step in range(S5_QB + S5_OUT_LAG):
        if step < S5_QB:
            input_stage(step)
        if 0 <= step - 1 < S5_QB:
            scan_stage(step - 1)
        if 0 <= step - S5_OUT_LAG < S5_QB:
            output_stage(step - S5_OUT_LAG)

    for b in range(S5_NB):
        y_ref[b] = yp_ref[b * pitch:b * pitch + lc, :]


def _s5(proj3, col, bdr, bdi, cdr, cdi, lbr, lbi, d5, h0r, h0i):
    bsz, l, _ = proj3.shape
    lc = min(S5_LC, l)
    pitch = lc + 4
    mp = S5_NB * pitch
    kern = functools.partial(_s5_kernel, lc=lc, pitch=pitch)
    ns = S5_GROUPS * S5_STATE
    full = lambda *shape: pl.BlockSpec(shape, lambda b, c: (0,) * len(shape))
    state = pl.BlockSpec((S5_NB, ns), lambda b, c: (b, 0))
    return pl.pallas_call(
        kern,
        grid=(bsz // S5_NB, l // lc),
        in_specs=[
            pl.BlockSpec((S5_NB, lc, S5_WIDTH), lambda b, c: (b, c, col)),
            full(S5_QB, LANES, 4 * LANES), full(S5_QB, LANES, 4 * LANES),
            full(S5_QB, 4 * LANES, LANES), full(S5_QB, 4 * LANES, LANES),
            full(1, ns), full(1, ns), full(1, S5_WIDTH),
            state, state,
        ],
        out_specs=[pl.BlockSpec((S5_NB, lc, S5_WIDTH), lambda b, c: (b, c, 0)), state, state],
        out_shape=[
            jax.ShapeDtypeStruct((bsz, l, S5_WIDTH), F32),
            jax.ShapeDtypeStruct((bsz, ns), F32),
            jax.ShapeDtypeStruct((bsz, ns), F32),
        ],
        scratch_shapes=[
            pltpu.VMEM((mp, S5_WIDTH), F32),
            pltpu.VMEM((mp, S5_WIDTH), F32),
        ] + [pltpu.VMEM((S5_SLABS // S5_QB, mp, LANES), F32)] * (2 * S5_QB),
        compiler_params=_params("parallel", "arbitrary"),
        name="s5",
    )(proj3, bdr, bdi, cdr, cdi, lbr, lbi, d5, h0r, h0i)


def _outproj_kernel(x_ref, ya_ref, y5_ref, wglu_ref, bglu_ref, nrm_ref, w_ref, o_ref):
    y5 = y5_ref[...]
    g = 0.5 * y5 * (1.0 + lax.erf(y5 * (0.5 ** 0.5)))
    gl = jnp.dot(g.astype(BF16), wglu_ref[...], preferred_element_type=F32) + bglu_ref[...]
    yb = _rms(g * jax.nn.sigmoid(gl), nrm_ref[...]).astype(BF16)
    o_ref[...] = (x_ref[...]
                  + jnp.dot(ya_ref[...], w_ref[:SSD_WIDTH, :], preferred_element_type=F32)
                  + jnp.dot(yb, w_ref[SSD_WIDTH:, :], preferred_element_type=F32))


def _out_proj(x, ya, y5, wglu, bglu, nrm, w):
    t = x.shape[0]
    tm = min(OUT_TM, t)
    const = lambda *shape: pl.BlockSpec(shape, lambda i: (0,) * len(shape), pipeline_mode=pl.Buffered(1))
    return pl.pallas_call(
        _outproj_kernel,
        grid=(t // tm,),
        in_specs=[
            pl.BlockSpec((tm, D_MODEL), lambda i: (i, 0)),
            pl.BlockSpec((tm, SSD_WIDTH), lambda i: (i, 0)),
            pl.BlockSpec((tm, S5_WIDTH), lambda i: (i, 0)),
            const(S5_WIDTH, S5_WIDTH), const(1, S5_WIDTH), const(1, S5_WIDTH),
            const(D_MODEL, D_MODEL),
        ],
        out_specs=pl.BlockSpec((tm, D_MODEL), lambda i: (i, 0)),
        out_shape=jax.ShapeDtypeStruct((t, D_MODEL), F32),
        compiler_params=_params("parallel"),
        name="out_proj",
    )(x, ya, y5, wglu, bglu, nrm, w)


def _castpad_kernel(x_ref, o_ref, *, axis):
    if axis == 1:
        o_ref[:, :D_FF] = x_ref[...].astype(BF16)
        o_ref[:, D_FF:] = jnp.zeros((o_ref.shape[0], D_FF_PAD - D_FF), BF16)
    else:
        o_ref[:D_FF, :] = x_ref[...].astype(BF16)
        o_ref[D_FF:, :] = jnp.zeros((D_FF_PAD - D_FF, o_ref.shape[1]), BF16)


def _cast_pad(w, axis):
    other = w.shape[1 - axis]
    if axis == 1:
        in_block, out_block, out_shape = (CASTPAD_BLOCK, D_FF), (CASTPAD_BLOCK, D_FF_PAD), (other, D_FF_PAD)
        index = lambda j: (j, 0)
    else:
        in_block, out_block, out_shape = (D_FF, CASTPAD_BLOCK), (D_FF_PAD, CASTPAD_BLOCK), (D_FF_PAD, other)
        index = lambda j: (0, j)
    return pl.pallas_call(
        functools.partial(_castpad_kernel, axis=axis),
        grid=(other // CASTPAD_BLOCK,),
        in_specs=[pl.BlockSpec(in_block, index)],
        out_specs=pl.BlockSpec(out_block, index),
        out_shape=jax.ShapeDtypeStruct(out_shape, BF16),
        compiler_params=_params("parallel"),
        name="cast_pad",
    )(w)


def _winprep_kernel(x_ref, o_ref):
    c0, c1, c2 = SSD_WIDTH, SSD_WIDTH + CONV_DIM, SSD_WIDTH + CONV_DIM + SSD_HEADS
    rows = o_ref.shape[0]
    o_ref[:, :CONV_DIM] = x_ref[:, c0:c1].astype(BF16)
    o_ref[:, CONV_DIM:CONV_DIM + SSD_WIDTH] = x_ref[:, :c0].astype(BF16)
    o_ref[:, CONV_DIM + SSD_WIDTH:PROJ_COLS - DT_PAD] = x_ref[:, c2:].astype(BF16)
    dt = jnp.concatenate([x_ref[:, c1:c2], jnp.zeros((rows, DT_PAD - SSD_HEADS), F32)], axis=1)
    o_ref[:, PROJ_COLS - DT_PAD:] = dt.astype(BF16)


def _prep_w_in(w_in):
    k, n = w_in.shape
    return pl.pallas_call(
        _winprep_kernel,
        grid=(k // CASTPAD_BLOCK,),
        in_specs=[pl.BlockSpec((CASTPAD_BLOCK, n), lambda j: (j, 0))],
        out_specs=pl.BlockSpec((CASTPAD_BLOCK, PROJ_COLS), lambda j: (j, 0)),
        out_shape=jax.ShapeDtypeStruct((k, PROJ_COLS), BF16),
        compiler_params=_params("parallel"),
        name="prep_w_in",
    )(w_in)


def _prepare(p):
    w = {}
    for n in ("1", "2"):
        w["norm_ffn" + n] = p["norm_ffn" + n]
        w["wg" + n] = _cast_pad(p["w_ffn%s_gate" % n][0], 1)
        w["wu" + n] = _cast_pad(p["w_ffn%s_up" % n][0], 1)
        w["wd" + n] = _cast_pad(p["w_ffn%s_down" % n][0], 0)
    w["w_in"] = _prep_w_in(p["w_in"][0])
    w["norm_mix"] = p["norm_mix"]
    w["conv_w"] = p["conv_w"][0]
    w["conv_b"] = p["conv_b"]
    w["dt_bias"] = jnp.pad(p["dt_bias"], ((0, 0), (0, DT_PAD - SSD_HEADS)))
    w["a_log"] = jnp.pad(p["a_log"], ((0, 0), (0, DT_PAD - SSD_HEADS)))
    w["d_ssd"] = jnp.repeat(p["d_ssd"][0], SSD_HEADDIM)[None]
    w["norm_ssd"] = p["norm_ssd"]

    lbr, lbi, bbr, bbi = _s5_prep(
        p["s5_lambda_re"][0], p["s5_lambda_im"][0],
        jnp.broadcast_to(p["s5_log_step"][0][:, None], (S5_GROUPS, S5_STATE)),
        jnp.transpose(p["s5_b_re"][0], (2, 0, 1)), jnp.transpose(p["s5_b_im"][0], (2, 0, 1)))
    eye = jnp.eye(S5_QB, dtype=F32)
    gl = S5_GROUPS // S5_QB

    def bd(bb):
        t = jnp.einsum("hqgn,gk->qghkn", bb.reshape(S5_GROUP_CH, S5_QB, gl, S5_STATE), eye)
        return t.reshape(S5_QB, LANES, gl * S5_STATE).astype(BF16)

    def cd(cc):
        t = jnp.einsum("qghn,gk->qgnkh", cc.reshape(S5_QB, gl, S5_GROUP_CH, S5_STATE), eye)
        return t.reshape(S5_QB, gl * S5_STATE, LANES).astype(BF16)

    w["bdr"], w["bdi"] = bd(bbr), bd(bbi)
    w["cdr"], w["cdi"] = cd(p["s5_c_re"][0]), cd(p["s5_c_im"][0])
    w["lbr"] = lbr.reshape(1, -1)
    w["lbi"] = lbi.reshape(1, -1)
    w["s5_d"] = p["s5_d"][0].reshape(1, -1)
    w["w_glu"] = p["w_glu"][0].astype(BF16)
    w["b_glu"] = p["b_glu"]
    w["norm_s5"] = p["norm_s5"]
    w["w_out"] = p["w_out"][0].astype(BF16)
    w["norm_final"] = p["norm_final"][None]
    return w


def _trunk(x, conv0, ssd0, s5r0, s5i0, w):
    bsz, l, _ = x.shape
    t = bsz * l
    x1 = _ffn(x.reshape(t, D_MODEL), w["norm_ffn1"], w["wg1"], w["wu1"], w["wd1"])
    ht0 = jnp.swapaxes(ssd0.reshape(bsz, SSD_WIDTH, SSD_STATE), 1, 2)
    if l % MIX_TM == 0:
        u5, y_ssd, conv_new, ht_new = _mix_in(x1, l, w, conv0, ht0)
        s5_in, s5_col = u5.reshape(bsz, l, S5_WIDTH), 0
    else:
        proj = _in_proj(x1, w["norm_mix"], w["w_in"]).reshape(bsz, l, PROJ_COLS)
        if l % CHUNK == 0:
            proj_ssd, valid = proj, CHUNK
        else:
            assert l < CHUNK
            proj_ssd, valid = jnp.pad(proj, ((0, 0), (0, CHUNK - l), (0, 0))), l
        y_ssd, conv_new, ht_new = _ssd(proj_ssd, valid, w["conv_w"], w["conv_b"], w["dt_bias"], w["a_log"],
                                       w["d_ssd"], w["norm_ssd"], conv0, ht0)
        y_ssd = y_ssd[:, :l].reshape(t, SSD_WIDTH)
        s5_in, s5_col = proj, (CONV_DIM + SSD_WIDTH) // S5_WIDTH
    ssd_new = jnp.swapaxes(ht_new, 1, 2).reshape(bsz, SSD_HEADS, SSD_HEADDIM, SSD_STATE)

    y_s5, s5r_new, s5i_new = _s5(s5_in, s5_col, w["bdr"], w["bdi"], w["cdr"], w["cdi"], w["lbr"], w["lbi"],
                                 w["s5_d"], s5r0.reshape(bsz, -1), s5i0.reshape(bsz, -1))

    x2 = _out_proj(x1, y_ssd, y_s5.reshape(t, S5_WIDTH),
                   w["w_glu"], w["b_glu"], w["norm_s5"], w["w_out"])
    y = _ffn(x2, w["norm_ffn2"], w["wg2"], w["wu2"], w["wd2"], final_g=w["norm_final"])
    return (y.reshape(bsz, l, D_MODEL), conv_new[None], ssd_new[None],
            s5r_new.reshape(1, bsz, S5_GROUPS, S5_STATE), s5i_new.reshape(1, bsz, S5_GROUPS, S5_STATE))


def kernel(x_prompt, x_sample, cache_conv, state_ssd, state_s5_re, state_s5_im, norm_ffn1, w_ffn1_gate, w_ffn1_up, w_ffn1_down, norm_mix, w_in, conv_w, conv_b, dt_bias, a_log, d_ssd, norm_ssd, s5_lambda_re, s5_lambda_im, s5_log_step, s5_b_re, s5_b_im, s5_c_re, s5_c_im, s5_d, w_glu, b_glu, norm_s5, w_out, norm_ffn2, w_ffn2_gate, w_ffn2_up, w_ffn2_down, norm_final):
    p = dict(norm_ffn1=norm_ffn1, w_ffn1_gate=w_ffn1_gate, w_ffn1_up=w_ffn1_up, w_ffn1_down=w_ffn1_down,
             norm_mix=norm_mix, w_in=w_in, conv_w=conv_w, conv_b=conv_b, dt_bias=dt_bias, a_log=a_log,
             d_ssd=d_ssd, norm_ssd=norm_ssd, s5_lambda_re=s5_lambda_re, s5_lambda_im=s5_lambda_im,
             s5_log_step=s5_log_step, s5_b_re=s5_b_re, s5_b_im=s5_b_im, s5_c_re=s5_c_re, s5_c_im=s5_c_im,
             s5_d=s5_d, w_glu=w_glu, b_glu=b_glu, norm_s5=norm_s5, w_out=w_out, norm_ffn2=norm_ffn2,
             w_ffn2_gate=w_ffn2_gate, w_ffn2_up=w_ffn2_up, w_ffn2_down=w_ffn2_down, norm_final=norm_final)
    w = _prepare(p)
    bsz = x_prompt.shape[0]
    zeros = lambda *s: jnp.zeros(s, F32)
    out_p = _trunk(x_prompt, zeros(bsz, CONV_W - 1, CONV_DIM),
                   zeros(bsz, SSD_HEADS, SSD_HEADDIM, SSD_STATE),
                   zeros(bsz, S5_GROUPS, S5_STATE), zeros(bsz, S5_GROUPS, S5_STATE), w)
    out_s = _trunk(x_sample, cache_conv[0], state_ssd[0], state_s5_re[0], state_s5_im[0], w)
    return (out_p[0], out_s[0]) + out_p[1:] + out_s[1:]
```

```python
import functools

import jax
import jax.numpy as jnp
from jax import lax
from jax.experimental import pallas as pl
from jax.experimental.pallas import tpu as pltpu

F32 = jnp.float32
BF16 = jnp.bfloat16
EPS = 1e-6

D_MODEL = 2048
D_FF = 5504
SSD_WIDTH = 1024
SSD_HEADDIM = 64
SSD_HEADS = 16
SSD_GROUPS = 4
SSD_STATE = 128
CONV_W = 4
CONV_DIM = SSD_WIDTH + 2 * SSD_GROUPS * SSD_STATE
S5_WIDTH = 1024
S5_GROUP_CH = 16
S5_GROUPS = 64
S5_STATE = 64
CHUNK = 64
SSD_NSUB = 4

LANES = 128
SUBLANES = 8
CONV_TAIL = CONV_W - 1
PAD_HEAD = SUBLANES
PAD_TAIL = slice(PAD_HEAD - CONV_TAIL, PAD_HEAD)
D_FF_PAD = 5632
FFN_TM = 1024
FFN_TF = 512
FFN_TF_SMALL = D_FF_PAD // 4
DT_PAD = LANES
PROJ_COLS = CONV_DIM + SSD_WIDTH + S5_WIDTH + DT_PAD
PROJ_TM = 512
MIX_TM = 512
PROJ_PIECE = 256
OUT_TM = 512
S5_NB = 8
S5_LC = 64
S5_SLABS = S5_GROUPS * S5_STATE // LANES
S5_QB = S5_WIDTH // LANES
S5_OUT_LAG = 2
CASTPAD_BLOCK = 256
VMEM_LIMIT = 56 * 1024 * 1024
MIX_VMEM_LIMIT = 60 * 1024 * 1024


def _rms(x, g):
    ms = jnp.mean(x * x, axis=-1, keepdims=True)
    return x * lax.rsqrt(ms + EPS) * g


def _silu(x):
    return x * jax.nn.sigmoid(x)


def _params(*sem, vmem_limit=VMEM_LIMIT):
    return pltpu.CompilerParams(dimension_semantics=sem, vmem_limit_bytes=vmem_limit)


def _ffn_kernel(x_ref, g_ref, wg_ref, wu_ref, wd_ref, gf_ref, o_ref, h_ref, *, final_norm):
    j = pl.program_id(1)

    @pl.when(j == 0)
    def _():
        x = x_ref[...]
        h_ref[...] = _rms(x, g_ref[...]).astype(BF16)
        o_ref[...] = x

    h = h_ref[...]
    gate = jnp.dot(h, wg_ref[...], preferred_element_type=F32)
    up = jnp.dot(h, wu_ref[...], preferred_element_type=F32)
    a = (0.5 * (_silu(gate) * up)).astype(BF16)
    o_ref[...] += jnp.dot(a, wd_ref[...], preferred_element_type=F32)

    if final_norm:
        @pl.when(j == pl.num_programs(1) - 1)
        def _():
            o_ref[...] = _rms(o_ref[...], gf_ref[...])


def _ffn(x, norm_g, wg, wu, wd, final_g=None):
    t = x.shape[0]
    tm = min(FFN_TM, t)
    tf = FFN_TF if t >= FFN_TM else FFN_TF_SMALL
    kern = functools.partial(_ffn_kernel, final_norm=final_g is not None)
    gf = norm_g if final_g is None else final_g
    return pl.pallas_call(
        kern,
        grid=(t // tm, D_FF_PAD // tf),
        in_specs=[
            pl.BlockSpec((tm, D_MODEL), lambda i, j: (i, 0)),
            pl.BlockSpec((1, D_MODEL), lambda i, j: (0, 0)),
            pl.BlockSpec((D_MODEL, tf), lambda i, j: (0, j)),
            pl.BlockSpec((D_MODEL, tf), lambda i, j: (0, j)),
            pl.BlockSpec((tf, D_MODEL), lambda i, j: (j, 0)),
            pl.BlockSpec((1, D_MODEL), lambda i, j: (0, 0)),
        ],
        out_specs=pl.BlockSpec((tm, D_MODEL), lambda i, j: (i, 0)),
        out_shape=jax.ShapeDtypeStruct((t, D_MODEL), F32),
        scratch_shapes=[pltpu.VMEM((tm, D_MODEL), BF16)],
        compiler_params=_params("parallel", "arbitrary"),
        name="ffn",
    )(x, norm_g, wg, wu, wd, gf)


def _inproj_kernel(x_ref, g_ref, w_ref, o_ref):
    h = _rms(x_ref[...], g_ref[...]).astype(BF16)
    o_ref[...] = jnp.dot(h, w_ref[...], preferred_element_type=F32)


def _in_proj(x, norm_g, w):
    t = x.shape[0]
    tm = min(PROJ_TM, t)
    return pl.pallas_call(
        _inproj_kernel,
        grid=(t // tm,),
        in_specs=[
            pl.BlockSpec((tm, D_MODEL), lambda i: (i, 0)),
            pl.BlockSpec((1, D_MODEL), lambda i: (0, 0)),
            pl.BlockSpec((D_MODEL, PROJ_COLS), lambda i: (0, 0), pipeline_mode=pl.Buffered(1)),
        ],
        out_specs=pl.BlockSpec((tm, PROJ_COLS), lambda i: (i, 0)),
        out_shape=jax.ShapeDtypeStruct((t, PROJ_COLS), F32),
        compiler_params=_params("parallel"),
        name="in_proj",
    )(x, norm_g, w)


def _shift_rows(x, k):
    row = lax.broadcasted_iota(jnp.int32, x.shape, 0)
    return jnp.where(row >= k, pltpu.roll(x, k, axis=0), 0.0)


def _cumsum_rows(x):
    k = 1
    while k < x.shape[0]:
        x = x + _shift_rows(x, k)
        k *= 2
    return x


def _chunk_cumsum(da):
    hq = da.shape[0] // 2
    even, odd = _cumsum_rows(da[:hq]), _cumsum_rows(da[hq:])
    return jnp.concatenate([even + _shift_rows(odd, 1), even + odd], axis=0)


def _expand_heads(v):
    q = v.shape[0]
    lane = lax.broadcasted_iota(jnp.int32, (q, LANES), 1)
    blocks = []
    for k in range(SSD_WIDTH // LANES):
        a = jnp.broadcast_to(v[:, 2 * k:2 * k + 1], (q, LANES))
        b = jnp.broadcast_to(v[:, 2 * k + 1:2 * k + 2], (q, LANES))
        blocks.append(jnp.where(lane < SSD_HEADDIM, a, b))
    return jnp.concatenate(blocks, axis=1)


def _slab(s):
    return slice(s * LANES, (s + 1) * LANES)


def _ssd_chunk_pre(dtr_ref, cw_ref, cb_ref, dtb_ref, alog_ref, d_ref, pad_ref, zs_ref, *, base, valid):
    q = CHUNK
    hq = q // 2
    gw = SSD_WIDTH // SSD_GROUPS
    slab = _slab
    row_time = lambda r: ((r & (hq - 1)) << 1) | (r >> (hq.bit_length() - 1))

    def split_rows(ref3, s, start):
        return [ref3[s, pl.ds(start + e, hq, stride=2), :] for e in range(2)]

    cw = cw_ref[...]
    cb = cb_ref[...]
    cols = []
    for s in range(CONV_DIM // LANES):
        taps = [split_rows(pad_ref, s, base + PAD_HEAD - CONV_TAIL + k) for k in range(CONV_W)]
        halves = []
        for e in range(2):
            acc = cb[:, slab(s)] + taps[0][e] * cw[0:1, slab(s)]
            for k in range(1, CONV_W):
                acc = acc + taps[k][e] * cw[k:k + 1, slab(s)]
            halves.append(acc)
        cols.append(jnp.concatenate(halves, axis=0))
    act = _silu(jnp.concatenate(cols, axis=1))
    xs = act[:, :SSD_WIDTH]
    bm = act[:, SSD_WIDTH:SSD_WIDTH + SSD_GROUPS * SSD_STATE].astype(BF16)
    cm = act[:, SSD_WIDTH + SSD_GROUPS * SSD_STATE:].astype(BF16)

    dt_raw = jnp.concatenate([dtr_ref[0, pl.ds(base + e, hq, stride=2), :] for e in range(2)], axis=0)
    dtc = jax.nn.softplus(dt_raw + dtb_ref[...])
    if valid < q:
        dtc = jnp.where(row_time(lax.broadcasted_iota(jnp.int32, dtc.shape, 0)) < valid, dtc, 0.0)
    da = dtc * (-jnp.exp(alog_ref[...]))
    acs_c = _chunk_cumsum(da)
    cbts = []
    for g in range(SSD_GROUPS):
        bg = bm[:, g * SSD_STATE:(g + 1) * SSD_STATE]
        cg = cm[:, g * SSD_STATE:(g + 1) * SSD_STATE]
        b4 = jnp.concatenate([bg] * (SSD_HEADS // SSD_GROUPS), axis=0)
        cbts.append(lax.dot_general(cg, b4, (((1,), (1,)), ((), ())), preferred_element_type=F32))
    zcols = [jnp.concatenate(split_rows(zs_ref, s, base), axis=0) for s in range(SSD_WIDTH // LANES)]
    gate = _silu(jnp.concatenate(zcols, axis=1))
    return dict(xs=xs, bm=bm, cm=cm, dtc=dtc, acs_c=acs_c, cbts=cbts, gate=gate, skip=d_ref[...] * xs)


def _ssd_chunk_post(pre):
    q = CHUNK
    hq = q // 2
    gw = SSD_WIDTH // SSD_GROUPS
    row_time = lambda r: ((r & (hq - 1)) << 1) | (r >> (hq.bit_length() - 1))
    xs, bm, cm = pre["xs"], pre["bm"], pre["cm"]
    dt_e = _expand_heads(pre["dtc"])
    acs_e = _expand_heads(pre["acs_c"])
    row = lax.broadcasted_iota(jnp.int32, (q, SSD_WIDTH), 0)
    jj = lax.broadcasted_iota(jnp.int32, (q, SSD_WIDTH), 1) & (SSD_HEADDIM - 1)
    acs_row = jnp.sum(jnp.where(jj == row, acs_e, 0.0), axis=0, keepdims=True)
    causal = row_time(jj) <= row_time(row)
    lmat = jnp.exp(jnp.where(causal, acs_e - acs_row, -jnp.inf))
    last = acs_e[q - 1:q, :]
    eacs = jnp.exp(acs_e)
    dec_end = jnp.exp(last - acs_e)
    cdec = jnp.exp(last)
    xdt = xs * dt_e
    xdt_b = xdt.astype(BF16)
    xd_b = (xdt * dec_end).astype(BF16)
    head_of_lane = lax.broadcasted_iota(jnp.int32, (q, gw), 1) // SSD_HEADDIM

    ydiags, cgs, sts = [], [], []
    for g in range(SSD_GROUPS):
        sl = slice(g * gw, (g + 1) * gw)
        bg = bm[:, g * SSD_STATE:(g + 1) * SSD_STATE]
        cg = cm[:, g * SSD_STATE:(g + 1) * SSD_STATE]
        sg = (pre["cbts"][g] * lmat[:, sl]).astype(BF16)
        xg = xdt_b[:, sl]
        xbd = jnp.concatenate(
            [jnp.where(head_of_lane == hh, xg, jnp.zeros_like(xg)) for hh in range(SSD_HEADS // SSD_GROUPS)],
            axis=0)
        ydiags.append(jnp.dot(sg, xbd, preferred_element_type=F32))
        cgs.append(cg)
        sts.append(lax.dot_general(bg, xd_b[:, sl], (((0,), (0,)), ((), ())), preferred_element_type=F32))
    return dict(ydiags=ydiags, cgs=cgs, sts=sts, eacs=eacs, cdec=cdec, skip=pre["skip"], gate=pre["gate"])


def _ssd_chunk_state(loc, ht_ref):
    gw = SSD_WIDTH // SSD_GROUPS
    ys = []
    for g in range(SSD_GROUPS):
        sl = slice(g * gw, (g + 1) * gw)
        htg = ht_ref[0, :, sl]
        yoff = jnp.dot(loc["cgs"][g], htg.astype(BF16), preferred_element_type=F32)
        ys.append(loc["ydiags"][g] + loc["eacs"][:, sl] * yoff)
        ht_ref[0, :, sl] = loc["cdec"][:, sl] * htg + loc["sts"][g]
    return (jnp.concatenate(ys, axis=1) + loc["skip"]) * loc["gate"]


def _ssd_chunk_out(y, nrm_ref, y_ref, *, base):
    q = CHUNK
    hq = q // 2
    row_time = lambda r: ((r & (hq - 1)) << 1) | (r >> (hq.bit_length() - 1))
    yn = _rms(y, nrm_ref[...]).astype(BF16)
    unperm = (row_time(lax.broadcasted_iota(jnp.int32, (q, q), 1))
              == lax.broadcasted_iota(jnp.int32, (q, q), 0)).astype(BF16)
    y_ref[0, base:base + q, :] = jnp.dot(unperm, yn, preferred_element_type=F32).astype(BF16)


def _ssd_kernel(xbc_ref, z_ref, dtr_ref, cw_ref, cb_ref, dtb_ref, alog_ref, d_ref, nrm_ref,
                conv0_ref, h0_ref, y_ref, convo_ref, ht_ref, pad_ref, zs_ref, *, valid, nsub):
    c = pl.program_id(1)
    rows = nsub * CHUNK
    last = rows - CHUNK + valid

    @pl.when(c == 0)
    def _():
        for s in range(CONV_DIM // LANES):
            pad_ref[s, PAD_TAIL, :] = conv0_ref[0, :, _slab(s)]
        ht_ref[0] = h0_ref[0]

    for s in range(CONV_DIM // LANES):
        pad_ref[s, PAD_HEAD:PAD_HEAD + rows, :] = xbc_ref[0, :, _slab(s)]
    for s in range(SSD_WIDTH // LANES):
        zs_ref[s] = z_ref[0, :, _slab(s)]
    pres = [_ssd_chunk_pre(dtr_ref, cw_ref, cb_ref, dtb_ref, alog_ref, d_ref, pad_ref, zs_ref,
                           base=i * CHUNK, valid=valid if i == nsub - 1 else CHUNK) for i in range(nsub)]
    locs = [_ssd_chunk_post(pre) for pre in pres]
    ys = [_ssd_chunk_state(loc, ht_ref) for loc in locs]
    for i in range(nsub):
        _ssd_chunk_out(ys[i], nrm_ref, y_ref, base=i * CHUNK)
    for s in range(CONV_DIM // LANES):
        pad_ref[s, PAD_TAIL, :] = pad_ref[s, PAD_HEAD + last - CONV_TAIL:PAD_HEAD + last, :]

    @pl.when(c == pl.num_programs(1) - 1)
    def _():
        for s in range(CONV_DIM // LANES):
            convo_ref[0, :, _slab(s)] = pad_ref[s, PAD_TAIL, :]


def _ssd(proj3, valid, cw, cb, dtb, alog, d_e, nrm, conv0, ht0):
    bsz, lp, _ = proj3.shape
    nsub = min(SSD_NSUB, lp // CHUNK)
    q = nsub * CHUNK
    kern = functools.partial(_ssd_kernel, valid=valid, nsub=nsub)
    row = lambda n: pl.BlockSpec((1, n), lambda b, c: (0, 0))
    return pl.pallas_call(
        kern,
        grid=(bsz, lp // q),
        in_specs=[
            pl.BlockSpec((1, q, CONV_DIM), lambda b, c: (b, c, 0)),
            pl.BlockSpec((1, q, SSD_WIDTH), lambda b, c: (b, c, CONV_DIM // SSD_WIDTH)),
            pl.BlockSpec((1, q, DT_PAD), lambda b, c: (b, c, (PROJ_COLS - DT_PAD) // DT_PAD)),
            pl.BlockSpec((CONV_W, CONV_DIM), lambda b, c: (0, 0)),
            row(CONV_DIM), row(DT_PAD), row(DT_PAD), row(SSD_WIDTH), row(SSD_WIDTH),
            pl.BlockSpec((1, CONV_W - 1, CONV_DIM), lambda b, c: (b, 0, 0)),
            pl.BlockSpec((1, SSD_STATE, SSD_WIDTH), lambda b, c: (b, 0, 0)),
        ],
        out_specs=[
            pl.BlockSpec((1, q, SSD_WIDTH), lambda b, c: (b, c, 0)),
            pl.BlockSpec((1, CONV_W - 1, CONV_DIM), lambda b, c: (b, 0, 0)),
            pl.BlockSpec((1, SSD_STATE, SSD_WIDTH), lambda b, c: (b, 0, 0)),
        ],
        out_shape=[
            jax.ShapeDtypeStruct((bsz, lp, SSD_WIDTH), BF16),
            jax.ShapeDtypeStruct((bsz, CONV_W - 1, CONV_DIM), F32),
            jax.ShapeDtypeStruct((bsz, SSD_STATE, SSD_WIDTH), F32),
        ],
        scratch_shapes=[pltpu.VMEM((CONV_DIM // LANES, PAD_HEAD + q, LANES), F32),
                        pltpu.VMEM((SSD_WIDTH // LANES, q, LANES), F32)],
        compiler_params=_params("parallel", "arbitrary"),
        name="ssd",
    )(proj3, proj3, proj3, cw, cb, dtb, alog, d_e, nrm, conv0, ht0)


def _mix_in_kernel(x_ref, g_ref, w_ref, cw_ref, cb_ref, dtb_ref, alog_ref, d_ref, nrm_ref, conv0_ref, h0_ref,
                   u5_ref, y_ref, convo_ref, ht_ref, pad_ref, zs_ref, dts_ref, *, tiles_per_seq):
    i = pl.program_id(0)
    nsub = MIX_TM // CHUNK
    in_seq = lax.rem(jnp.maximum(i - 1, 0), tiles_per_seq)

    @pl.when(i == 0)
    def _():
        pad_ref[...] = jnp.zeros_like(pad_ref)
        zs_ref[...] = jnp.zeros_like(zs_ref)
        dts_ref[...] = jnp.zeros_like(dts_ref)
        ht_ref[...] = jnp.zeros_like(ht_ref)

    @pl.when(jnp.logical_and(i >= 1, in_seq == 0))
    def _():
        for s in range(CONV_DIM // LANES):
            pad_ref[s, PAD_TAIL, :] = conv0_ref[0, :, _slab(s)]
        ht_ref[0] = h0_ref[0]

    h = _rms(x_ref[...], g_ref[...]).astype(BF16)
    bounds = list(range(0, PROJ_COLS, PROJ_PIECE)) + [PROJ_COLS]
    npieces = len(bounds) - 1
    per_chunk = npieces // nsub
    pieces = []

    def project(n):
        pieces.append(jnp.dot(h, w_ref[:, bounds[n]:bounds[n + 1]], preferred_element_type=F32))

    pres, locs = [], []
    for c in range(nsub + 1):
        if c < nsub:
            pres.append(_ssd_chunk_pre(dts_ref, cw_ref, cb_ref, dtb_ref, alog_ref, d_ref, pad_ref, zs_ref,
                                       base=c * CHUNK, valid=CHUNK))
        if c >= 1:
            locs.append(_ssd_chunk_post(pres[c - 1]))
        for n in range(c * per_chunk, min((c + 1) * per_chunk, nsub * per_chunk)):
            project(n)
    ys = [_ssd_chunk_state(loc, ht_ref) for loc in locs]
    for n in range(nsub * per_chunk, npieces):
        project(n)
    for c in range(nsub):
        _ssd_chunk_out(ys[c], nrm_ref, y_ref, base=c * CHUNK)

    for s in range(CONV_DIM // LANES):
        pad_ref[s, PAD_TAIL, :] = pad_ref[s, PAD_HEAD + MIX_TM - CONV_TAIL:PAD_HEAD + MIX_TM, :]

    def column_tile(c0):
        return pieces[c0 // PROJ_PIECE][:, c0 % PROJ_PIECE:c0 % PROJ_PIECE + LANES]

    for s in range(CONV_DIM // LANES):
        pad_ref[s, PAD_HEAD:PAD_HEAD + MIX_TM, :] = column_tile(s * LANES)
    for s in range(SSD_WIDTH // LANES):
        zs_ref[s] = column_tile(CONV_DIM + s * LANES)
        u5_ref[:, _slab(s)] = column_tile(CONV_DIM + SSD_WIDTH + s * LANES)
    dts_ref[0] = column_tile(PROJ_COLS - DT_PAD)

    @pl.when(jnp.logical_and(i >= 1, in_seq == tiles_per_seq - 1))
    def _():
        for s in range(CONV_DIM // LANES):
            convo_ref[0, :, _slab(s)] = pad_ref[s, PAD_TAIL, :]


def _mix_in(x1, l, w, conv0, ht0):
    t = x1.shape[0]
    bsz, nt, tps = t // l, t // MIX_TM, l // MIX_TM
    seq = lambda i: jnp.maximum(i - 1, 0) // tps
    row = lambda n: pl.BlockSpec((1, n), lambda i: (0, 0))
    u5, y, conv_new, ht_new = pl.pallas_call(
        functools.partial(_mix_in_kernel, tiles_per_seq=tps),
        grid=(nt + 1,),
        in_specs=[
            pl.BlockSpec((MIX_TM, D_MODEL), lambda i: (jnp.minimum(i, nt - 1), 0)),
            row(D_MODEL),
            pl.BlockSpec((D_MODEL, PROJ_COLS), lambda i: (0, 0), pipeline_mode=pl.Buffered(1)),
            pl.BlockSpec((CONV_W, CONV_DIM), lambda i: (0, 0)),
            row(CONV_DIM), row(DT_PAD), row(DT_PAD), row(SSD_WIDTH), row(SSD_WIDTH),
            pl.BlockSpec((1, CONV_W - 1, CONV_DIM), lambda i: (seq(i), 0, 0)),
            pl.BlockSpec((1, SSD_STATE, SSD_WIDTH), lambda i: (seq(i), 0, 0)),
        ],
        out_specs=[
            pl.BlockSpec((MIX_TM, S5_WIDTH), lambda i: (jnp.minimum(i, nt - 1), 0)),
            pl.BlockSpec((1, MIX_TM, SSD_WIDTH), lambda i: (jnp.maximum(i - 1, 0), 0, 0)),
            pl.BlockSpec((1, CONV_W - 1, CONV_DIM), lambda i: (seq(i), 0, 0)),
            pl.BlockSpec((1, SSD_STATE, SSD_WIDTH), lambda i: (seq(i), 0, 0)),
        ],
        out_shape=[
            jax.ShapeDtypeStruct((t, S5_WIDTH), F32),
            jax.ShapeDtypeStruct((nt, MIX_TM, SSD_WIDTH), BF16),
            jax.ShapeDtypeStruct((bsz, CONV_W - 1, CONV_DIM), F32),
            jax.ShapeDtypeStruct((bsz, SSD_STATE, SSD_WIDTH), F32),
        ],
        scratch_shapes=[pltpu.VMEM((CONV_DIM // LANES, PAD_HEAD + MIX_TM, LANES), F32),
                        pltpu.VMEM((SSD_WIDTH // LANES, MIX_TM, LANES), F32),
                        pltpu.VMEM((1, MIX_TM, DT_PAD), F32)],
        compiler_params=_params("arbitrary", vmem_limit=MIX_VMEM_LIMIT),
        name="mix_in",
    )(x1, w["norm_mix"], w["w_in"], w["conv_w"], w["conv_b"], w["dt_bias"], w["a_log"], w["d_ssd"],
      w["norm_ssd"], conv0, ht0)
    return u5, y.reshape(t, SSD_WIDTH), conv_new, ht_new


def _s5_prep_kernel(lre_ref, lim_ref, lstep_ref, bre_ref, bim_ref, lbr_ref, lbi_ref, bbr_ref, bbi_ref):
    lam_re = lre_ref[...]
    lam_im = lim_ref[...]
    step = jnp.exp(lstep_ref[...])
    mag = jnp.exp(lam_re * step)
    ang = lam_im * step
    lb_re = mag * jnp.cos(ang)
    lb_im = mag * jnp.sin(ang)
    den = lam_re * lam_re + lam_im * lam_im
    nr = lb_re - 1.0
    q_re = (nr * lam_re + lb_im * lam_im) / den
    q_im = (lb_im * lam_re - nr * lam_im) / den
    lbr_ref[...] = lb_re
    lbi_ref[...] = lb_im
    b_re = bre_ref[...]
    b_im = bim_ref[...]
    bbr_ref[...] = q_re[None] * b_re - q_im[None] * b_im
    bbi_ref[...] = q_re[None] * b_im + q_im[None] * b_re


def _s5_prep(lam_re, lam_im, log_step_e, b_re_t, b_im_t):
    gn = jax.ShapeDtypeStruct((S5_GROUPS, S5_STATE), F32)
    hgn = jax.ShapeDtypeStruct((S5_GROUP_CH, S5_GROUPS, S5_STATE), F32)
    return pl.pallas_call(_s5_prep_kernel, out_shape=[gn, gn, hgn, hgn], name="s5_prep")(
        lam_re, lam_im, log_step_e, b_re_t, b_im_t)


def _s5_kernel(u_ref, bdr_ref, bdi_ref, cdr_ref, cdi_ref, lbr_ref, lbi_ref, d_ref, h0r_ref, h0i_ref,
               y_ref, hor_ref, hoi_ref, up_ref, yp_ref, *slab_refs, lc, pitch):
    c = pl.program_id(1)

    @pl.when(c == 0)
    def _():
        hor_ref[...] = h0r_ref[...]
        hoi_ref[...] = h0i_ref[...]
        up_ref[...] = jnp.zeros_like(up_ref)

    for b in range(S5_NB):
        up_ref[b * pitch:b * pitch + lc, :] = u_ref[b]
    u = up_ref[...]
    ub = u.astype(BF16)

    spb = S5_SLABS // S5_QB
    hr_refs, hi_refs = slab_refs[:S5_QB], slab_refs[S5_QB:]

    def input_stage(qb):
        uq = ub[:, qb * LANES:(qb + 1) * LANES]
        bu_r = jnp.dot(uq, bdr_ref[qb], preferred_element_type=F32)
        bu_i = jnp.dot(uq, bdi_ref[qb], preferred_element_type=F32)
        for k in range(spb):
            hr_refs[qb][k] = bu_r[:, k * LANES:(k + 1) * LANES]
            hi_refs[qb][k] = bu_i[:, k * LANES:(k + 1) * LANES]

    def scan_stage(qb):
        hr_ref, hi_ref = hr_refs[qb], hi_refs[qb]
        for k in range(spb):
            sl = slice((qb * spb + k) * LANES, (qb * spb + k + 1) * LANES)
            lr = jnp.broadcast_to(lbr_ref[:, sl], (S5_NB, LANES))
            li = jnp.broadcast_to(lbi_ref[:, sl], (S5_NB, LANES))
            h_r, h_i = hor_ref[:, sl], hoi_ref[:, sl]
            for t in range(lc):
                rows = pl.ds(t, S5_NB, stride=pitch)
                n_r = lr * h_r - li * h_i + hr_ref[k, rows, :]
                n_i = lr * h_i + li * h_r + hi_ref[k, rows, :]
                hr_ref[k, rows, :] = n_r
                hi_ref[k, rows, :] = n_i
                h_r, h_i = n_r, n_i
            hor_ref[:, sl] = h_r
            hoi_ref[:, sl] = h_i

    def output_stage(qb):
        ln = slice(qb * LANES, (qb + 1) * LANES)
        h_re = jnp.concatenate([hr_refs[qb][k] for k in range(spb)], axis=1).astype(BF16)
        h_im = jnp.concatenate([hi_refs[qb][k] for k in range(spb)], axis=1).astype(BF16)
        yq = (jnp.dot(h_re, cdr_ref[qb], preferred_element_type=F32)
              - jnp.dot(h_im, cdi_ref[qb], preferred_element_type=F32))
        yp_ref[:, ln] = yq + d_ref[:, ln] * u[:, ln]

    for step in range(S5_QB + S5_OUT_LAG):
        if 0 <= step - 1 < S5_QB:
            scan_stage(step - 1)
        if 0 <= step - S5_OUT_LAG < S5_QB:
            output_stage(step - S5_OUT_LAG)
        if step < S5_QB:
            input_stage(step)

    for b in range(S5_NB):
        y_ref[b] = yp_ref[b * pitch:b * pitch + lc, :]


def _s5(proj3, col, bdr, bdi, cdr, cdi, lbr, lbi, d5, h0r, h0i):
    bsz, l, _ = proj3.shape
    lc = min(S5_LC, l)
    pitch = lc + 4
    mp = S5_NB * pitch
    kern = functools.partial(_s5_kernel, lc=lc, pitch=pitch)
    ns = S5_GROUPS * S5_STATE
    full = lambda *shape: pl.BlockSpec(shape, lambda b, c: (0,) * len(shape))
    state = pl.BlockSpec((S5_NB, ns), lambda b, c: (b, 0))
    return pl.pallas_call(
        kern,
        grid=(bsz // S5_NB, l // lc),
        in_specs=[
            pl.BlockSpec((S5_NB, lc, S5_WIDTH), lambda b, c: (b, c, col)),
            full(S5_QB, LANES, 4 * LANES), full(S5_QB, LANES, 4 * LANES),
            full(S5_QB, 4 * LANES, LANES), full(S5_QB, 4 * LANES, LANES),
            full(1, ns), full(1, ns), full(1, S5_WIDTH),
            state, state,
        ],
        out_specs=[pl.BlockSpec((S5_NB, lc, S5_WIDTH), lambda b, c: (b, c, 0)), state, state],
        out_shape=[
            jax.ShapeDtypeStruct((bsz, l, S5_WIDTH), F32),
            jax.ShapeDtypeStruct((bsz, ns), F32),
            jax.ShapeDtypeStruct((bsz, ns), F32),
        ],
        scratch_shapes=[
            pltpu.VMEM((mp, S5_WIDTH), F32),
            pltpu.VMEM((mp, S5_WIDTH), F32),
        ] + [pltpu.VMEM((S5_SLABS // S5_QB, mp, LANES), F32)] * (2 * S5_QB),
        compiler_params=_params("parallel", "arbitrary"),
        name="s5",
    )(proj3, bdr, bdi, cdr, cdi, lbr, lbi, d5, h0r, h0i)


def _outproj_kernel(x_ref, ya_ref, y5_ref, wglu_ref, bglu_ref, nrm_ref, w_ref, o_ref):
    y5 = y5_ref[...]
    g = 0.5 * y5 * (1.0 + lax.erf(y5 * (0.5 ** 0.5)))
    gl = jnp.dot(g.astype(BF16), wglu_ref[...], preferred_element_type=F32) + bglu_ref[...]
    yb = _rms(g * jax.nn.sigmoid(gl), nrm_ref[...]).astype(BF16)
    o_ref[...] = (x_ref[...]
                  + jnp.dot(ya_ref[...], w_ref[:SSD_WIDTH, :], preferred_element_type=F32)
                  + jnp.dot(yb, w_ref[SSD_WIDTH:, :], preferred_element_type=F32))


def _out_proj(x, ya, y5, wglu, bglu, nrm, w):
    t = x.shape[0]
    tm = min(OUT_TM, t)
    const = lambda *shape: pl.BlockSpec(shape, lambda i: (0,) * len(shape), pipeline_mode=pl.Buffered(1))
    return pl.pallas_call(
        _outproj_kernel,
        grid=(t // tm,),
        in_specs=[
            pl.BlockSpec((tm, D_MODEL), lambda i: (i, 0)),
            pl.BlockSpec((tm, SSD_WIDTH), lambda i: (i, 0)),
            pl.BlockSpec((tm, S5_WIDTH), lambda i: (i, 0)),
            const(S5_WIDTH, S5_WIDTH), const(1, S5_WIDTH), const(1, S5_WIDTH),
            const(D_MODEL, D_MODEL),
        ],
        out_specs=pl.BlockSpec((tm, D_MODEL), lambda i: (i, 0)),
        out_shape=jax.ShapeDtypeStruct((t, D_MODEL), F32),
        compiler_params=_params("parallel"),
        name="out_proj",
    )(x, ya, y5, wglu, bglu, nrm, w)


def _castpad_kernel(x_ref, o_ref, *, axis):
    if axis == 1:
        o_ref[:, :D_FF] = x_ref[...].astype(BF16)
        o_ref[:, D_FF:] = jnp.zeros((o_ref.shape[0], D_FF_PAD - D_FF), BF16)
    else:
        o_ref[:D_FF, :] = x_ref[...].astype(BF16)
        o_ref[D_FF:, :] = jnp.zeros((D_FF_PAD - D_FF, o_ref.shape[1]), BF16)


def _cast_pad(w, axis):
    other = w.shape[1 - axis]
    if axis == 1:
        in_block, out_block, out_shape = (CASTPAD_BLOCK, D_FF), (CASTPAD_BLOCK, D_FF_PAD), (other, D_FF_PAD)
        index = lambda j: (j, 0)
    else:
        in_block, out_block, out_shape = (D_FF, CASTPAD_BLOCK), (D_FF_PAD, CASTPAD_BLOCK), (D_FF_PAD, other)
        index = lambda j: (0, j)
    return pl.pallas_call(
        functools.partial(_castpad_kernel, axis=axis),
        grid=(other // CASTPAD_BLOCK,),
        in_specs=[pl.BlockSpec(in_block, index)],
        out_specs=pl.BlockSpec(out_block, index),
        out_shape=jax.ShapeDtypeStruct(out_shape, BF16),
        compiler_params=_params("parallel"),
        name="cast_pad",
    )(w)


def _winprep_kernel(x_ref, o_ref):
    c0, c1, c2 = SSD_WIDTH, SSD_WIDTH + CONV_DIM, SSD_WIDTH + CONV_DIM + SSD_HEADS
    rows = o_ref.shape[0]
    o_ref[:, :CONV_DIM] = x_ref[:, c0:c1].astype(BF16)
    o_ref[:, CONV_DIM:CONV_DIM + SSD_WIDTH] = x_ref[:, :c0].astype(BF16)
    o_ref[:, CONV_DIM + SSD_WIDTH:PROJ_COLS - DT_PAD] = x_ref[:, c2:].astype(BF16)
    dt = jnp.concatenate([x_ref[:, c1:c2], jnp.zeros((rows, DT_PAD - SSD_HEADS), F32)], axis=1)
    o_ref[:, PROJ_COLS - DT_PAD:] = dt.astype(BF16)


def _prep_w_in(w_in):
    k, n = w_in.shape
    return pl.pallas_call(
        _winprep_kernel,
        grid=(k // CASTPAD_BLOCK,),
        in_specs=[pl.BlockSpec((CASTPAD_BLOCK, n), lambda j: (j, 0))],
        out_specs=pl.BlockSpec((CASTPAD_BLOCK, PROJ_COLS), lambda j: (j, 0)),
        out_shape=jax.ShapeDtypeStruct((k, PROJ_COLS), BF16),
        compiler_params=_params("parallel"),
        name="prep_w_in",
    )(w_in)


def _prepare(p):
    w = {}
    for n in ("1", "2"):
        w["norm_ffn" + n] = p["norm_ffn" + n]
        w["wg" + n] = _cast_pad(p["w_ffn%s_gate" % n][0], 1)
        w["wu" + n] = _cast_pad(p["w_ffn%s_up" % n][0], 1)
        w["wd" + n] = _cast_pad(p["w_ffn%s_down" % n][0], 0)
    w["w_in"] = _prep_w_in(p["w_in"][0])
    w["norm_mix"] = p["norm_mix"]
    w["conv_w"] = p["conv_w"][0]
    w["conv_b"] = p["conv_b"]
    w["dt_bias"] = jnp.pad(p["dt_bias"], ((0, 0), (0, DT_PAD - SSD_HEADS)))
    w["a_log"] = jnp.pad(p["a_log"], ((0, 0), (0, DT_PAD - SSD_HEADS)))
    w["d_ssd"] = jnp.repeat(p["d_ssd"][0], SSD_HEADDIM)[None]
    w["norm_ssd"] = p["norm_ssd"]

    lbr, lbi, bbr, bbi = _s5_prep(
        p["s5_lambda_re"][0], p["s5_lambda_im"][0],
        jnp.broadcast_to(p["s5_log_step"][0][:, None], (S5_GROUPS, S5_STATE)),
        jnp.transpose(p["s5_b_re"][0], (2, 0, 1)), jnp.transpose(p["s5_b_im"][0], (2, 0, 1)))
    eye = jnp.eye(S5_QB, dtype=F32)
    gl = S5_GROUPS // S5_QB

    def bd(bb):
        t = jnp.einsum("hqgn,gk->qghkn", bb.reshape(S5_GROUP_CH, S5_QB, gl, S5_STATE), eye)
        return t.reshape(S5_QB, LANES, gl * S5_STATE).astype(BF16)

    def cd(cc):
        t = jnp.einsum("qghn,gk->qgnkh", cc.reshape(S5_QB, gl, S5_GROUP_CH, S5_STATE), eye)
        return t.reshape(S5_QB, gl * S5_STATE, LANES).astype(BF16)

    w["bdr"], w["bdi"] = bd(bbr), bd(bbi)
    w["cdr"], w["cdi"] = cd(p["s5_c_re"][0]), cd(p["s5_c_im"][0])
    w["lbr"] = lbr.reshape(1, -1)
    w["lbi"] = lbi.reshape(1, -1)
    w["s5_d"] = p["s5_d"][0].reshape(1, -1)
    w["w_glu"] = p["w_glu"][0].astype(BF16)
    w["b_glu"] = p["b_glu"]
    w["norm_s5"] = p["norm_s5"]
    w["w_out"] = p["w_out"][0].astype(BF16)
    w["norm_final"] = p["norm_final"][None]
    return w


def _trunk(x, conv0, ssd0, s5r0, s5i0, w):
    bsz, l, _ = x.shape
    t = bsz * l
    x1 = _ffn(x.reshape(t, D_MODEL), w["norm_ffn1"], w["wg1"], w["wu1"], w["wd1"])
    ht0 = jnp.swapaxes(ssd0.reshape(bsz, SSD_WIDTH, SSD_STATE), 1, 2)
    if l % MIX_TM == 0:
        u5, y_ssd, conv_new, ht_new = _mix_in(x1, l, w, conv0, ht0)
        s5_in, s5_col = u5.reshape(bsz, l, S5_WIDTH), 0
    else:
        proj = _in_proj(x1, w["norm_mix"], w["w_in"]).reshape(bsz, l, PROJ_COLS)
        if l % CHUNK == 0:
            proj_ssd, valid = proj, CHUNK
        else:
            assert l < CHUNK
            proj_ssd, valid = jnp.pad(proj, ((0, 0), (0, CHUNK - l), (0, 0))), l
        y_ssd, conv_new, ht_new = _ssd(proj_ssd, valid, w["conv_w"], w["conv_b"], w["dt_bias"], w["a_log"],
                                       w["d_ssd"], w["norm_ssd"], conv0, ht0)
        y_ssd = y_ssd[:, :l].reshape(t, SSD_WIDTH)
        s5_in, s5_col = proj, (CONV_DIM + SSD_WIDTH) // S5_WIDTH
    ssd_new = jnp.swapaxes(ht_new, 1, 2).reshape(bsz, SSD_HEADS, SSD_HEADDIM, SSD_STATE)

    y_s5, s5r_new, s5i_new = _s5(s5_in, s5_col, w["bdr"], w["bdi"], w["cdr"], w["cdi"], w["lbr"], w["lbi"],
                                 w["s5_d"], s5r0.reshape(bsz, -1), s5i0.reshape(bsz, -1))

    x2 = _out_proj(x1, y_ssd, y_s5.reshape(t, S5_WIDTH),
                   w["w_glu"], w["b_glu"], w["norm_s5"], w["w_out"])
    y = _ffn(x2, w["norm_ffn2"], w["wg2"], w["wu2"], w["wd2"], final_g=w["norm_final"])
    return (y.reshape(bsz, l, D_MODEL), conv_new[None], ssd_new[None],
            s5r_new.reshape(1, bsz, S5_GROUPS, S5_STATE), s5i_new.reshape(1, bsz, S5_GROUPS, S5_STATE))


def kernel(x_prompt, x_sample, cache_conv, state_ssd, state_s5_re, state_s5_im, norm_ffn1, w_ffn1_gate, w_ffn1_up, w_ffn1_down, norm_mix, w_in, conv_w, conv_b, dt_bias, a_log, d_ssd, norm_ssd, s5_lambda_re, s5_lambda_im, s5_log_step, s5_b_re, s5_b_im, s5_c_re, s5_c_im, s5_d, w_glu, b_glu, norm_s5, w_out, norm_ffn2, w_ffn2_gate, w_ffn2_up, w_ffn2_down, norm_final):
    p = dict(norm_ffn1=norm_ffn1, w_ffn1_gate=w_ffn1_gate, w_ffn1_up=w_ffn1_up, w_ffn1_down=w_ffn1_down,
             norm_mix=norm_mix, w_in=w_in, conv_w=conv_w, conv_b=conv_b, dt_bias=dt_bias, a_log=a_log,
             d_ssd=d_ssd, norm_ssd=norm_ssd, s5_lambda_re=s5_lambda_re, s5_lambda_im=s5_lambda_im,
             s5_log_step=s5_log_step, s5_b_re=s5_b_re, s5_b_im=s5_b_im, s5_c_re=s5_c_re, s5_c_im=s5_c_im,
             s5_d=s5_d, w_glu=w_glu, b_glu=b_glu, norm_s5=norm_s5, w_out=w_out, norm_ffn2=norm_ffn2,
             w_ffn2_gate=w_ffn2_gate, w_ffn2_up=w_ffn2_up, w_ffn2_down=w_ffn2_down, norm_final=norm_final)
    w = _prepare(p)
    bsz = x_prompt.shape[0]
    zeros = lambda *s: jnp.zeros(s, F32)
    out_p = _trunk(x_prompt, zeros(bsz, CONV_W - 1, CONV_DIM),
                   zeros(bsz, SSD_HEADS, SSD_HEADDIM, SSD_STATE),
                   zeros(bsz, S5_GROUPS, S5_STATE), zeros(bsz, S5_GROUPS, S5_STATE), w)
    out_s = _trunk(x_sample, cache_conv[0], state_ssd[0], state_s5_re[0], state_s5_im[0], w)
    return (out_p[0], out_s[0]) + out_p[1:] + out_s[1:]
```

```python
import functools

import jax
import jax.numpy as jnp
from jax import lax
from jax.experimental import pallas as pl
from jax.experimental.pallas import tpu as pltpu

F32 = jnp.float32
BF16 = jnp.bfloat16
EPS = 1e-6

D_MODEL = 2048
D_FF = 5504
SSD_WIDTH = 1024
SSD_HEADDIM = 64
SSD_HEADS = 16
SSD_GROUPS = 4
SSD_STATE = 128
CONV_W = 4
CONV_DIM = SSD_WIDTH + 2 * SSD_GROUPS * SSD_STATE
S5_WIDTH = 1024
S5_GROUP_CH = 16
S5_GROUPS = 64
S5_STATE = 64
CHUNK = 64
SSD_NSUB = 4

LANES = 128
SUBLANES = 8
CONV_TAIL = CONV_W - 1
PAD_HEAD = SUBLANES
PAD_TAIL = slice(PAD_HEAD - CONV_TAIL, PAD_HEAD)
FFN_TM = 1024
FFN_TF = 512
D_FF_PAD = -(-D_FF // FFN_TF) * FFN_TF
DT_PAD = LANES
PROJ_COLS = CONV_DIM + SSD_WIDTH + S5_WIDTH + DT_PAD
PROJ_TM = 512
MIX_TM = 512
PROJ_PIECE = 256
OUT_TM = 512
S5_NB = 8
S5_LC = 64
S5_SLABS = S5_GROUPS * S5_STATE // LANES
S5_QB = S5_WIDTH // LANES
S5_OUT_LAG = 2
CASTPAD_BLOCK = 256
VMEM_LIMIT = 56 * 1024 * 1024
MIX_VMEM_LIMIT = 60 * 1024 * 1024


def _rms(x, g):
    ms = jnp.mean(x * x, axis=-1, keepdims=True)
    return x * lax.rsqrt(ms + EPS) * g


def _silu(x):
    return x * jax.nn.sigmoid(x)


def _params(*sem, vmem_limit=VMEM_LIMIT):
    return pltpu.CompilerParams(dimension_semantics=sem, vmem_limit_bytes=vmem_limit)


def _ffn_kernel(x_ref, g_ref, wg_ref, wu_ref, wd_ref, gf_ref, o_ref, h_ref, *, final_norm):
    j = pl.program_id(1)

    @pl.when(j == 0)
    def _():
        x = x_ref[...]
        h_ref[...] = _rms(x, g_ref[...]).astype(BF16)
        o_ref[...] = x

    h = h_ref[...]
    gate = jnp.dot(h, wg_ref[...], preferred_element_type=F32)
    up = jnp.dot(h, wu_ref[...], preferred_element_type=F32)
    a = (0.5 * (_silu(gate) * up)).astype(BF16)
    o_ref[...] += jnp.dot(a, wd_ref[...], preferred_element_type=F32)

    if final_norm:
        @pl.when(j == pl.num_programs(1) - 1)
        def _():
            o_ref[...] = _rms(o_ref[...], gf_ref[...])


def _ffn(x, norm_g, wg, wu, wd, final_g=None):
    t = x.shape[0]
    tm = min(FFN_TM, t)
    tf = FFN_TF
    kern = functools.partial(_ffn_kernel, final_norm=final_g is not None)
    gf = norm_g if final_g is None else final_g
    return pl.pallas_call(
        kern,
        grid=(t // tm, D_FF_PAD // tf),
        in_specs=[
            pl.BlockSpec((tm, D_MODEL), lambda i, j: (i, 0)),
            pl.BlockSpec((1, D_MODEL), lambda i, j: (0, 0)),
            pl.BlockSpec((D_MODEL, tf), lambda i, j: (0, j)),
            pl.BlockSpec((D_MODEL, tf), lambda i, j: (0, j)),
            pl.BlockSpec((tf, D_MODEL), lambda i, j: (j, 0)),
            pl.BlockSpec((1, D_MODEL), lambda i, j: (0, 0)),
        ],
        out_specs=pl.BlockSpec((tm, D_MODEL), lambda i, j: (i, 0)),
        out_shape=jax.ShapeDtypeStruct((t, D_MODEL), F32),
        scratch_shapes=[pltpu.VMEM((tm, D_MODEL), BF16)],
        compiler_params=_params("parallel", "arbitrary"),
        name="ffn",
    )(x, norm_g, wg, wu, wd, gf)


def _inproj_kernel(x_ref, g_ref, w_ref, o_ref):
    h = _rms(x_ref[...], g_ref[...]).astype(BF16)
    o_ref[...] = jnp.dot(h, w_ref[...], preferred_element_type=F32)


def _in_proj(x, norm_g, w):
    t = x.shape[0]
    tm = min(PROJ_TM, t)
    return pl.pallas_call(
        _inproj_kernel,
        grid=(t // tm,),
        in_specs=[
            pl.BlockSpec((tm, D_MODEL), lambda i: (i, 0)),
            pl.BlockSpec((1, D_MODEL), lambda i: (0, 0)),
            pl.BlockSpec((D_MODEL, PROJ_COLS), lambda i: (0, 0), pipeline_mode=pl.Buffered(1)),
        ],
        out_specs=pl.BlockSpec((tm, PROJ_COLS), lambda i: (i, 0)),
        out_shape=jax.ShapeDtypeStruct((t, PROJ_COLS), F32),
        compiler_params=_params("parallel"),
        name="in_proj",
    )(x, norm_g, w)


def _shift_rows(x, k):
    row = lax.broadcasted_iota(jnp.int32, x.shape, 0)
    return jnp.where(row >= k, pltpu.roll(x, k, axis=0), 0.0)


def _cumsum_rows(x):
    k = 1
    while k < x.shape[0]:
        x = x + _shift_rows(x, k)
        k *= 2
    return x


def _chunk_cumsum(da):
    hq = da.shape[0] // 2
    even, odd = _cumsum_rows(da[:hq]), _cumsum_rows(da[hq:])
    return jnp.concatenate([even + _shift_rows(odd, 1), even + odd], axis=0)


def _expand_heads(v):
    q = v.shape[0]
    lane = lax.broadcasted_iota(jnp.int32, (q, LANES), 1)
    blocks = []
    for k in range(SSD_WIDTH // LANES):
        a = jnp.broadcast_to(v[:, 2 * k:2 * k + 1], (q, LANES))
        b = jnp.broadcast_to(v[:, 2 * k + 1:2 * k + 2], (q, LANES))
        blocks.append(jnp.where(lane < SSD_HEADDIM, a, b))
    return jnp.concatenate(blocks, axis=1)


def _slab(s):
    return slice(s * LANES, (s + 1) * LANES)


def _ssd_chunk_pre(dtr_ref, cw_ref, cb_ref, dtb_ref, alog_ref, d_ref, pad_ref, zs_ref, *, base, valid):
    q = CHUNK
    hq = q // 2
    gw = SSD_WIDTH // SSD_GROUPS
    slab = _slab
    row_time = lambda r: ((r & (hq - 1)) << 1) | (r >> (hq.bit_length() - 1))

    def split_rows(ref3, s, start):
        return [ref3[s, pl.ds(start + e, hq, stride=2), :] for e in range(2)]

    cw = cw_ref[...]
    cb = cb_ref[...]
    cols = []
    for s in range(CONV_DIM // LANES):
        taps = [split_rows(pad_ref, s, base + PAD_HEAD - CONV_TAIL + k) for k in range(CONV_W)]
        halves = []
        for e in range(2):
            acc = cb[:, slab(s)] + taps[0][e] * cw[0:1, slab(s)]
            for k in range(1, CONV_W):
                acc = acc + taps[k][e] * cw[k:k + 1, slab(s)]
            halves.append(acc)
        cols.append(jnp.concatenate(halves, axis=0))
    act = _silu(jnp.concatenate(cols, axis=1))
    xs = act[:, :SSD_WIDTH]
    bm = act[:, SSD_WIDTH:SSD_WIDTH + SSD_GROUPS * SSD_STATE].astype(BF16)
    cm = act[:, SSD_WIDTH + SSD_GROUPS * SSD_STATE:].astype(BF16)

    dt_raw = jnp.concatenate([dtr_ref[0, pl.ds(base + e, hq, stride=2), :] for e in range(2)], axis=0)
    dtc = jax.nn.softplus(dt_raw + dtb_ref[...])
    if valid < q:
        dtc = jnp.where(row_time(lax.broadcasted_iota(jnp.int32, dtc.shape, 0)) < valid, dtc, 0.0)
    da = dtc * (-jnp.exp(alog_ref[...]))
    acs_c = _chunk_cumsum(da)
    cbts = []
    for g in range(SSD_GROUPS):
        bg = bm[:, g * SSD_STATE:(g + 1) * SSD_STATE]
        cg = cm[:, g * SSD_STATE:(g + 1) * SSD_STATE]
        b4 = jnp.concatenate([bg] * (SSD_HEADS // SSD_GROUPS), axis=0)
        cbts.append(lax.dot_general(cg, b4, (((1,), (1,)), ((), ())), preferred_element_type=F32))
    zcols = [jnp.concatenate(split_rows(zs_ref, s, base), axis=0) for s in range(SSD_WIDTH // LANES)]
    gate = _silu(jnp.concatenate(zcols, axis=1))
    return dict(xs=xs, bm=bm, cm=cm, dtc=dtc, acs_c=acs_c, cbts=cbts, gate=gate, skip=d_ref[...] * xs)


def _ssd_chunk_post(pre):
    q = CHUNK
    hq = q // 2
    gw = SSD_WIDTH // SSD_GROUPS
    row_time = lambda r: ((r & (hq - 1)) << 1) | (r >> (hq.bit_length() - 1))
    xs, bm, cm = pre["xs"], pre["bm"], pre["cm"]
    dt_e = _expand_heads(pre["dtc"])
    acs_e = _expand_heads(pre["acs_c"])
    row = lax.broadcasted_iota(jnp.int32, (q, SSD_WIDTH), 0)
    jj = lax.broadcasted_iota(jnp.int32, (q, SSD_WIDTH), 1) & (SSD_HEADDIM - 1)
    acs_row = jnp.sum(jnp.where(jj == row, acs_e, 0.0), axis=0, keepdims=True)
    causal = row_time(jj) <= row_time(row)
    lmat = jnp.exp(jnp.where(causal, acs_e - acs_row, -jnp.inf))
    last = acs_e[q - 1:q, :]
    eacs = jnp.exp(acs_e)
    dec_end = jnp.exp(last - acs_e)
    cdec = jnp.exp(last)
    xdt = xs * dt_e
    xdt_b = xdt.astype(BF16)
    xd_b = (xdt * dec_end).astype(BF16)
    head_of_lane = lax.broadcasted_iota(jnp.int32, (q, gw), 1) // SSD_HEADDIM

    ydiags, cgs, sts = [], [], []
    for g in range(SSD_GROUPS):
        sl = slice(g * gw, (g + 1) * gw)
        bg = bm[:, g * SSD_STATE:(g + 1) * SSD_STATE]
        cg = cm[:, g * SSD_STATE:(g + 1) * SSD_STATE]
        sg = (pre["cbts"][g] * lmat[:, sl]).astype(BF16)
        xg = xdt_b[:, sl]
        xbd = jnp.concatenate(
            [jnp.where(head_of_lane == hh, xg, jnp.zeros_like(xg)) for hh in range(SSD_HEADS // SSD_GROUPS)],
            axis=0)
        ydiags.append(jnp.dot(sg, xbd, preferred_element_type=F32))
        cgs.append(cg)
        sts.append(lax.dot_general(bg, xd_b[:, sl], (((0,), (0,)), ((), ())), preferred_element_type=F32))
    return dict(ydiags=ydiags, cgs=cgs, sts=sts, eacs=eacs, cdec=cdec, skip=pre["skip"], gate=pre["gate"])


def _ssd_chunk_state(loc, ht_ref):
    gw = SSD_WIDTH // SSD_GROUPS
    ys = []
    for g in range(SSD_GROUPS):
        sl = slice(g * gw, (g + 1) * gw)
        htg = ht_ref[0, :, sl]
        yoff = jnp.dot(loc["cgs"][g], htg.astype(BF16), preferred_element_type=F32)
        ys.append(loc["ydiags"][g] + loc["eacs"][:, sl] * yoff)
        ht_ref[0, :, sl] = loc["cdec"][:, sl] * htg + loc["sts"][g]
    return (jnp.concatenate(ys, axis=1) + loc["skip"]) * loc["gate"]


def _ssd_chunk_out(y, nrm_ref, y_ref, *, base):
    q = CHUNK
    hq = q // 2
    row_time = lambda r: ((r & (hq - 1)) << 1) | (r >> (hq.bit_length() - 1))
    yn = _rms(y, nrm_ref[...]).astype(BF16)
    unperm = (row_time(lax.broadcasted_iota(jnp.int32, (q, q), 1))
              == lax.broadcasted_iota(jnp.int32, (q, q), 0)).astype(BF16)
    y_ref[0, base:base + q, :] = jnp.dot(unperm, yn, preferred_element_type=F32).astype(BF16)


def _ssd_kernel(xbc_ref, z_ref, dtr_ref, cw_ref, cb_ref, dtb_ref, alog_ref, d_ref, nrm_ref,
                conv0_ref, h0_ref, y_ref, convo_ref, ht_ref, pad_ref, zs_ref, *, valid, nsub):
    c = pl.program_id(1)
    rows = nsub * CHUNK
    last = rows - CHUNK + valid

    @pl.when(c == 0)
    def _():
        for s in range(CONV_DIM // LANES):
            pad_ref[s, PAD_TAIL, :] = conv0_ref[0, :, _slab(s)]
        ht_ref[0] = h0_ref[0]

    for s in range(CONV_DIM // LANES):
        pad_ref[s, PAD_HEAD:PAD_HEAD + rows, :] = xbc_ref[0, :, _slab(s)]
    for s in range(SSD_WIDTH // LANES):
        zs_ref[s] = z_ref[0, :, _slab(s)]
    pres = [_ssd_chunk_pre(dtr_ref, cw_ref, cb_ref, dtb_ref, alog_ref, d_ref, pad_ref, zs_ref,
                           base=i * CHUNK, valid=valid if i == nsub - 1 else CHUNK) for i in range(nsub)]
    locs = [_ssd_chunk_post(pre) for pre in pres]
    ys = [_ssd_chunk_state(loc, ht_ref) for loc in locs]
    for i in range(nsub):
        _ssd_chunk_out(ys[i], nrm_ref, y_ref, base=i * CHUNK)
    for s in range(CONV_DIM // LANES):
        pad_ref[s, PAD_TAIL, :] = pad_ref[s, PAD_HEAD + last - CONV_TAIL:PAD_HEAD + last, :]

    @pl.when(c == pl.num_programs(1) - 1)
    def _():
        for s in range(CONV_DIM // LANES):
            convo_ref[0, :, _slab(s)] = pad_ref[s, PAD_TAIL, :]


def _ssd(proj3, valid, cw, cb, dtb, alog, d_e, nrm, conv0, ht0):
    bsz, lp, _ = proj3.shape
    nsub = min(SSD_NSUB, lp // CHUNK)
    q = nsub * CHUNK
    kern = functools.partial(_ssd_kernel, valid=valid, nsub=nsub)
    row = lambda n: pl.BlockSpec((1, n), lambda b, c: (0, 0))
    return pl.pallas_call(
        kern,
        grid=(bsz, lp // q),
        in_specs=[
            pl.BlockSpec((1, q, CONV_DIM), lambda b, c: (b, c, 0)),
            pl.BlockSpec((1, q, SSD_WIDTH), lambda b, c: (b, c, CONV_DIM // SSD_WIDTH)),
            pl.BlockSpec((1, q, DT_PAD), lambda b, c: (b, c, (PROJ_COLS - DT_PAD) // DT_PAD)),
            pl.BlockSpec((CONV_W, CONV_DIM), lambda b, c: (0, 0)),
            row(CONV_DIM), row(DT_PAD), row(DT_PAD), row(SSD_WIDTH), row(SSD_WIDTH),
            pl.BlockSpec((1, CONV_W - 1, CONV_DIM), lambda b, c: (b, 0, 0)),
            pl.BlockSpec((1, SSD_STATE, SSD_WIDTH), lambda b, c: (b, 0, 0)),
        ],
        out_specs=[
            pl.BlockSpec((1, q, SSD_WIDTH), lambda b, c: (b, c, 0)),
            pl.BlockSpec((1, CONV_W - 1, CONV_DIM), lambda b, c: (b, 0, 0)),
            pl.BlockSpec((1, SSD_STATE, SSD_WIDTH), lambda b, c: (b, 0, 0)),
        ],
        out_shape=[
            jax.ShapeDtypeStruct((bsz, lp, SSD_WIDTH), BF16),
            jax.ShapeDtypeStruct((bsz, CONV_W - 1, CONV_DIM), F32),
            jax.ShapeDtypeStruct((bsz, SSD_STATE, SSD_WIDTH), F32),
        ],
        scratch_shapes=[pltpu.VMEM((CONV_DIM // LANES, PAD_HEAD + q, LANES), F32),
                        pltpu.VMEM((SSD_WIDTH // LANES, q, LANES), F32)],
        compiler_params=_params("parallel", "arbitrary"),
        name="ssd",
    )(proj3, proj3, proj3, cw, cb, dtb, alog, d_e, nrm, conv0, ht0)


def _mix_in_kernel(x_ref, g_ref, w_ref, cw_ref, cb_ref, dtb_ref, alog_ref, d_ref, nrm_ref, conv0_ref, h0_ref,
                   u5_ref, y_ref, convo_ref, ht_ref, pad_ref, zs_ref, dts_ref, *, tiles_per_seq):
    i = pl.program_id(0)
    nsub = MIX_TM // CHUNK
    in_seq = lax.rem(jnp.maximum(i - 1, 0), tiles_per_seq)

    @pl.when(i == 0)
    def _():
        pad_ref[...] = jnp.zeros_like(pad_ref)
        zs_ref[...] = jnp.zeros_like(zs_ref)
        dts_ref[...] = jnp.zeros_like(dts_ref)
        ht_ref[...] = jnp.zeros_like(ht_ref)

    @pl.when(jnp.logical_and(i >= 1, in_seq == 0))
    def _():
        for s in range(CONV_DIM // LANES):
            pad_ref[s, PAD_TAIL, :] = conv0_ref[0, :, _slab(s)]
        ht_ref[0] = h0_ref[0]

    h = _rms(x_ref[...], g_ref[...]).astype(BF16)
    bounds = list(range(0, PROJ_COLS, PROJ_PIECE)) + [PROJ_COLS]
    npieces = len(bounds) - 1
    per_chunk = npieces // nsub
    pieces = []

    def project(n):
        pieces.append(jnp.dot(h, w_ref[:, bounds[n]:bounds[n + 1]], preferred_element_type=F32))

    pres, locs = [], []
    for c in range(nsub + 1):
        if c < nsub:
            pres.append(_ssd_chunk_pre(dts_ref, cw_ref, cb_ref, dtb_ref, alog_ref, d_ref, pad_ref, zs_ref,
                                       base=c * CHUNK, valid=CHUNK))
        if c >= 1:
            locs.append(_ssd_chunk_post(pres[c - 1]))
        for n in range(c * per_chunk, min((c + 1) * per_chunk, nsub * per_chunk)):
            project(n)
    ys = [_ssd_chunk_state(loc, ht_ref) for loc in locs]
    for n in range(nsub * per_chunk, npieces):
        project(n)
    for c in range(nsub):
        _ssd_chunk_out(ys[c], nrm_ref, y_ref, base=c * CHUNK)

    for s in range(CONV_DIM // LANES):
        pad_ref[s, PAD_TAIL, :] = pad_ref[s, PAD_HEAD + MIX_TM - CONV_TAIL:PAD_HEAD + MIX_TM, :]

    def column_tile(c0):
        return pieces[c0 // PROJ_PIECE][:, c0 % PROJ_PIECE:c0 % PROJ_PIECE + LANES]

    for s in range(CONV_DIM // LANES):
        pad_ref[s, PAD_HEAD:PAD_HEAD + MIX_TM, :] = column_tile(s * LANES)
    for s in range(SSD_WIDTH // LANES):
        zs_ref[s] = column_tile(CONV_DIM + s * LANES)
        u5_ref[:, _slab(s)] = column_tile(CONV_DIM + SSD_WIDTH + s * LANES)
    dts_ref[0] = column_tile(PROJ_COLS - DT_PAD)

    @pl.when(jnp.logical_and(i >= 1, in_seq == tiles_per_seq - 1))
    def _():
        for s in range(CONV_DIM // LANES):
            convo_ref[0, :, _slab(s)] = pad_ref[s, PAD_TAIL, :]


def _mix_in(x1, l, w, conv0, ht0):
    t = x1.shape[0]
    bsz, nt, tps = t // l, t // MIX_TM, l // MIX_TM
    seq = lambda i: jnp.maximum(i - 1, 0) // tps
    row = lambda n: pl.BlockSpec((1, n), lambda i: (0, 0))
    u5, y, conv_new, ht_new = pl.pallas_call(
        functools.partial(_mix_in_kernel, tiles_per_seq=tps),
        grid=(nt + 1,),
        in_specs=[
            pl.BlockSpec((MIX_TM, D_MODEL), lambda i: (jnp.minimum(i, nt - 1), 0)),
            row(D_MODEL),
            pl.BlockSpec((D_MODEL, PROJ_COLS), lambda i: (0, 0), pipeline_mode=pl.Buffered(1)),
            pl.BlockSpec((CONV_W, CONV_DIM), lambda i: (0, 0)),
            row(CONV_DIM), row(DT_PAD), row(DT_PAD), row(SSD_WIDTH), row(SSD_WIDTH),
            pl.BlockSpec((1, CONV_W - 1, CONV_DIM), lambda i: (seq(i), 0, 0)),
            pl.BlockSpec((1, SSD_STATE, SSD_WIDTH), lambda i: (seq(i), 0, 0)),
        ],
        out_specs=[
            pl.BlockSpec((MIX_TM, S5_WIDTH), lambda i: (jnp.minimum(i, nt - 1), 0)),
            pl.BlockSpec((1, MIX_TM, SSD_WIDTH), lambda i: (jnp.maximum(i - 1, 0), 0, 0)),
            pl.BlockSpec((1, CONV_W - 1, CONV_DIM), lambda i: (seq(i), 0, 0)),
            pl.BlockSpec((1, SSD_STATE, SSD_WIDTH), lambda i: (seq(i), 0, 0)),
        ],
        out_shape=[
            jax.ShapeDtypeStruct((t, S5_WIDTH), F32),
            jax.ShapeDtypeStruct((nt, MIX_TM, SSD_WIDTH), BF16),
            jax.ShapeDtypeStruct((bsz, CONV_W - 1, CONV_DIM), F32),
            jax.ShapeDtypeStruct((bsz, SSD_STATE, SSD_WIDTH), F32),
        ],
        scratch_shapes=[pltpu.VMEM((CONV_DIM // LANES, PAD_HEAD + MIX_TM, LANES), F32),
                        pltpu.VMEM((SSD_WIDTH // LANES, MIX_TM, LANES), F32),
                        pltpu.VMEM((1, MIX_TM, DT_PAD), F32)],
        compiler_params=_params("arbitrary", vmem_limit=MIX_VMEM_LIMIT),
        name="mix_in",
    )(x1, w["norm_mix"], w["w_in"], w["conv_w"], w["conv_b"], w["dt_bias"], w["a_log"], w["d_ssd"],
      w["norm_ssd"], conv0, ht0)
    return u5, y.reshape(t, SSD_WIDTH), conv_new, ht_new


def _s5_prep_kernel(lre_ref, lim_ref, lstep_ref, bre_ref, bim_ref, lbr_ref, lbi_ref, bbr_ref, bbi_ref):
    lam_re = lre_ref[...]
    lam_im = lim_ref[...]
    step = jnp.exp(lstep_ref[...])
    mag = jnp.exp(lam_re * step)
    ang = lam_im * step
    lb_re = mag * jnp.cos(ang)
    lb_im = mag * jnp.sin(ang)
    den = lam_re * lam_re + lam_im * lam_im
    nr = lb_re - 1.0
    q_re = (nr * lam_re + lb_im * lam_im) / den
    q_im = (lb_im * lam_re - nr * lam_im) / den
    lbr_ref[...] = lb_re
    lbi_ref[...] = lb_im
    b_re = bre_ref[...]
    b_im = bim_ref[...]
    bbr_ref[...] = q_re[None] * b_re - q_im[None] * b_im
    bbi_ref[...] = q_re[None] * b_im + q_im[None] * b_re


def _s5_prep(lam_re, lam_im, log_step_e, b_re_t, b_im_t):
    gn = jax.ShapeDtypeStruct((S5_GROUPS, S5_STATE), F32)
    hgn = jax.ShapeDtypeStruct((S5_GROUP_CH, S5_GROUPS, S5_STATE), F32)
    return pl.pallas_call(_s5_prep_kernel, out_shape=[gn, gn, hgn, hgn], name="s5_prep")(
        lam_re, lam_im, log_step_e, b_re_t, b_im_t)


def _s5_kernel(u_ref, bdr_ref, bdi_ref, cdr_ref, cdi_ref, lbr_ref, lbi_ref, d_ref, h0r_ref, h0i_ref,
               y_ref, hor_ref, hoi_ref, up_ref, yp_ref, *slab_refs, lc, pitch):
    c = pl.program_id(1)

    @pl.when(c == 0)
    def _():
        hor_ref[...] = h0r_ref[...]
        hoi_ref[...] = h0i_ref[...]
        up_ref[...] = jnp.zeros_like(up_ref)

    for b in range(S5_NB):
        up_ref[b * pitch:b * pitch + lc, :] = u_ref[b]
    u = up_ref[...]
    ub = u.astype(BF16)

    spb = S5_SLABS // S5_QB
    hr_refs, hi_refs = slab_refs[:S5_QB], slab_refs[S5_QB:]

    def input_stage(qb):
        uq = ub[:, qb * LANES:(qb + 1) * LANES]
        bu_r = jnp.dot(uq, bdr_ref[qb], preferred_element_type=F32)
        bu_i = jnp.dot(uq, bdi_ref[qb], preferred_element_type=F32)
        for k in range(spb):
            hr_refs[qb][k] = bu_r[:, k * LANES:(k + 1) * LANES]
            hi_refs[qb][k] = bu_i[:, k * LANES:(k + 1) * LANES]

    def scan_stage(qb):
        hr_ref, hi_ref = hr_refs[qb], hi_refs[qb]
        for k in range(spb):
            sl = slice((qb * spb + k) * LANES, (qb * spb + k + 1) * LANES)
            lr = jnp.broadcast_to(lbr_ref[:, sl], (S5_NB, LANES))
            li = jnp.broadcast_to(lbi_ref[:, sl], (S5_NB, LANES))
            h_r, h_i = hor_ref[:, sl], hoi_ref[:, sl]
            for t in range(lc):
                rows = pl.ds(t, S5_NB, stride=pitch)
                n_r = lr * h_r - li * h_i + hr_ref[k, rows, :]
                n_i = lr * h_i + li * h_r + hi_ref[k, rows, :]
                hr_ref[k, rows, :] = n_r
                hi_ref[k, rows, :] = n_i
                h_r, h_i = n_r, n_i
            hor_ref[:, sl] = h_r
            hoi_ref[:, sl] = h_i

    def output_stage(qb):
        ln = slice(qb * LANES, (qb + 1) * LANES)
        h_re = jnp.concatenate([hr_refs[qb][k] for k in range(spb)], axis=1).astype(BF16)
        h_im = jnp.concatenate([hi_refs[qb][k] for k in range(spb)], axis=1).astype(BF16)
        yq = (jnp.dot(h_re, cdr_ref[qb], preferred_element_type=F32)
              - jnp.dot(h_im, cdi_ref[qb], preferred_element_type=F32))
        yp_ref[:, ln] = yq + d_ref[:, ln] * u[:, ln]

    for step in range(S5_QB + S5_OUT_LAG):
        if 0 <= step - 1 < S5_QB:
            scan_stage(step - 1)
        if 0 <= step - S5_OUT_LAG < S5_QB:
            output_stage(step - S5_OUT_LAG)
        if step < S5_QB:
            input_stage(step)

    for b in range(S5_NB):
        y_ref[b] = yp_ref[b * pitch:b * pitch + lc, :]


def _s5(proj3, col, bdr, bdi, cdr, cdi, lbr, lbi, d5, h0r, h0i):
    bsz, l, _ = proj3.shape
    lc = min(S5_LC, l)
    pitch = lc + 4
    mp = S5_NB * pitch
    kern = functools.partial(_s5_kernel, lc=lc, pitch=pitch)
    ns = S5_GROUPS * S5_STATE
    full = lambda *shape: pl.BlockSpec(shape, lambda b, c: (0,) * len(shape))
    state = pl.BlockSpec((S5_NB, ns), lambda b, c: (b, 0))
    return pl.pallas_call(
        kern,
        grid=(bsz // S5_NB, l // lc),
        in_specs=[
            pl.BlockSpec((S5_NB, lc, S5_WIDTH), lambda b, c: (b, c, col)),
            full(S5_QB, LANES, 4 * LANES), full(S5_QB, LANES, 4 * LANES),
            full(S5_QB, 4 * LANES, LANES), full(S5_QB, 4 * LANES, LANES),
            full(1, ns), full(1, ns), full(1, S5_WIDTH),
            state, state,
        ],
        out_specs=[pl.BlockSpec((S5_NB, lc, S5_WIDTH), lambda b, c: (b, c, 0)), state, state],
        out_shape=[
            jax.ShapeDtypeStruct((bsz, l, S5_WIDTH), F32),
            jax.ShapeDtypeStruct((bsz, ns), F32),
            jax.ShapeDtypeStruct((bsz, ns), F32),
        ],
        scratch_shapes=[
            pltpu.VMEM((mp, S5_WIDTH), F32),
            pltpu.VMEM((mp, S5_WIDTH), F32),
        ] + [pltpu.VMEM((S5_SLABS // S5_QB, mp, LANES), F32)] * (2 * S5_QB),
        compiler_params=_params("parallel", "arbitrary"),
        name="s5",
    )(proj3, bdr, bdi, cdr, cdi, lbr, lbi, d5, h0r, h0i)


def _outproj_kernel(x_ref, ya_ref, y5_ref, wglu_ref, bglu_ref, nrm_ref, w_ref, o_ref):
    y5 = y5_ref[...]
    g = 0.5 * y5 * (1.0 + lax.erf(y5 * (0.5 ** 0.5)))
    gl = jnp.dot(g.astype(BF16), wglu_ref[...], preferred_element_type=F32) + bglu_ref[...]
    yb = _rms(g * jax.nn.sigmoid(gl), nrm_ref[...]).astype(BF16)
    o_ref[...] = (x_ref[...]
                  + jnp.dot(ya_ref[...], w_ref[:SSD_WIDTH, :], preferred_element_type=F32)
                  + jnp.dot(yb, w_ref[SSD_WIDTH:, :], preferred_element_type=F32))


def _out_proj(x, ya, y5, wglu, bglu, nrm, w):
    t = x.shape[0]
    tm = min(OUT_TM, t)
    const = lambda *shape: pl.BlockSpec(shape, lambda i: (0,) * len(shape), pipeline_mode=pl.Buffered(1))
    return pl.pallas_call(
        _outproj_kernel,
        grid=(t // tm,),
        in_specs=[
            pl.BlockSpec((tm, D_MODEL), lambda i: (i, 0)),
            pl.BlockSpec((tm, SSD_WIDTH), lambda i: (i, 0)),
            pl.BlockSpec((tm, S5_WIDTH), lambda i: (i, 0)),
            const(S5_WIDTH, S5_WIDTH), const(1, S5_WIDTH), const(1, S5_WIDTH),
            const(D_MODEL, D_MODEL),
        ],
        out_specs=pl.BlockSpec((tm, D_MODEL), lambda i: (i, 0)),
        out_shape=jax.ShapeDtypeStruct((t, D_MODEL), F32),
        compiler_params=_params("parallel"),
        name="out_proj",
    )(x, ya, y5, wglu, bglu, nrm, w)


def _castpad_kernel(x_ref, o_ref, *, axis):
    if axis == 1:
        o_ref[:, :D_FF] = x_ref[...].astype(BF16)
        o_ref[:, D_FF:] = jnp.zeros((o_ref.shape[0], D_FF_PAD - D_FF), BF16)
    else:
        o_ref[:D_FF, :] = x_ref[...].astype(BF16)
        o_ref[D_FF:, :] = jnp.zeros((D_FF_PAD - D_FF, o_ref.shape[1]), BF16)


def _cast_pad(w, axis):
    other = w.shape[1 - axis]
    if axis == 1:
        in_block, out_block, out_shape = (CASTPAD_BLOCK, D_FF), (CASTPAD_BLOCK, D_FF_PAD), (other, D_FF_PAD)
        index = lambda j: (j, 0)
    else:
        in_block, out_block, out_shape = (D_FF, CASTPAD_BLOCK), (D_FF_PAD, CASTPAD_BLOCK), (D_FF_PAD, other)
        index = lambda j: (0, j)
    return pl.pallas_call(
        functools.partial(_castpad_kernel, axis=axis),
        grid=(other // CASTPAD_BLOCK,),
        in_specs=[pl.BlockSpec(in_block, index)],
        out_specs=pl.BlockSpec(out_block, index),
        out_shape=jax.ShapeDtypeStruct(out_shape, BF16),
        compiler_params=_params("parallel"),
        name="cast_pad",
    )(w)


def _winprep_kernel(x_ref, o_ref):
    c0, c1, c2 = SSD_WIDTH, SSD_WIDTH + CONV_DIM, SSD_WIDTH + CONV_DIM + SSD_HEADS
    rows = o_ref.shape[0]
    o_ref[:, :CONV_DIM] = x_ref[:, c0:c1].astype(BF16)
    o_ref[:, CONV_DIM:CONV_DIM + SSD_WIDTH] = x_ref[:, :c0].astype(BF16)
    o_ref[:, CONV_DIM + SSD_WIDTH:PROJ_COLS - DT_PAD] = x_ref[:, c2:].astype(BF16)
    dt = jnp.concatenate([x_ref[:, c1:c2], jnp.zeros((rows, DT_PAD - SSD_HEADS), F32)], axis=1)
    o_ref[:, PROJ_COLS - DT_PAD:] = dt.astype(BF16)


def _prep_w_in(w_in):
    k, n = w_in.shape
    return pl.pallas_call(
        _winprep_kernel,
        grid=(k // CASTPAD_BLOCK,),
        in_specs=[pl.BlockSpec((CASTPAD_BLOCK, n), lambda j: (j, 0))],
        out_specs=pl.BlockSpec((CASTPAD_BLOCK, PROJ_COLS), lambda j: (j, 0)),
        out_shape=jax.ShapeDtypeStruct((k, PROJ_COLS), BF16),
        compiler_params=_params("parallel"),
        name="prep_w_in",
    )(w_in)


def _prepare(p):
    w = {}
    for n in ("1", "2"):
        w["norm_ffn" + n] = p["norm_ffn" + n]
        w["wg" + n] = _cast_pad(p["w_ffn%s_gate" % n][0], 1)
        w["wu" + n] = _cast_pad(p["w_ffn%s_up" % n][0], 1)
        w["wd" + n] = _cast_pad(p["w_ffn%s_down" % n][0], 0)
    w["w_in"] = _prep_w_in(p["w_in"][0])
    w["norm_mix"] = p["norm_mix"]
    w["conv_w"] = p["conv_w"][0]
    w["conv_b"] = p["conv_b"]
    w["dt_bias"] = jnp.pad(p["dt_bias"], ((0, 0), (0, DT_PAD - SSD_HEADS)))
    w["a_log"] = jnp.pad(p["a_log"], ((0, 0), (0, DT_PAD - SSD_HEADS)))
    w["d_ssd"] = jnp.repeat(p["d_ssd"][0], SSD_HEADDIM)[None]
    w["norm_ssd"] = p["norm_ssd"]

    lbr, lbi, bbr, bbi = _s5_prep(
        p["s5_lambda_re"][0], p["s5_lambda_im"][0],
        jnp.broadcast_to(p["s5_log_step"][0][:, None], (S5_GROUPS, S5_STATE)),
        jnp.transpose(p["s5_b_re"][0], (2, 0, 1)), jnp.transpose(p["s5_b_im"][0], (2, 0, 1)))
    eye = jnp.eye(S5_QB, dtype=F32)
    gl = S5_GROUPS // S5_QB

    def bd(bb):
        t = jnp.einsum("hqgn,gk->qghkn", bb.reshape(S5_GROUP_CH, S5_QB, gl, S5_STATE), eye)
        return t.reshape(S5_QB, LANES, gl * S5_STATE).astype(BF16)

    def cd(cc):
        t = jnp.einsum("qghn,gk->qgnkh", cc.reshape(S5_QB, gl, S5_GROUP_CH, S5_STATE), eye)
        return t.reshape(S5_QB, gl * S5_STATE, LANES).astype(BF16)

    w["bdr"], w["bdi"] = bd(bbr), bd(bbi)
    w["cdr"], w["cdi"] = cd(p["s5_c_re"][0]), cd(p["s5_c_im"][0])
    w["lbr"] = lbr.reshape(1, -1)
    w["lbi"] = lbi.reshape(1, -1)
    w["s5_d"] = p["s5_d"][0].reshape(1, -1)
    w["w_glu"] = p["w_glu"][0].astype(BF16)
    w["b_glu"] = p["b_glu"]
    w["norm_s5"] = p["norm_s5"]
    w["w_out"] = p["w_out"][0].astype(BF16)
    w["norm_final"] = p["norm_final"][None]
    return w


def _trunk(x, conv0, ssd0, s5r0, s5i0, w):
    bsz, l, _ = x.shape
    t = bsz * l
    x1 = _ffn(x.reshape(t, D_MODEL), w["norm_ffn1"], w["wg1"], w["wu1"], w["wd1"])
    ht0 = jnp.swapaxes(ssd0.reshape(bsz, SSD_WIDTH, SSD_STATE), 1, 2)
    if l % MIX_TM == 0:
        u5, y_ssd, conv_new, ht_new = _mix_in(x1, l, w, conv0, ht0)
        s5_in, s5_col = u5.reshape(bsz, l, S5_WIDTH), 0
    else:
        proj = _in_proj(x1, w["norm_mix"], w["w_in"]).reshape(bsz, l, PROJ_COLS)
        if l % CHUNK == 0:
            proj_ssd, valid = proj, CHUNK
        else:
            assert l < CHUNK
            proj_ssd, valid = jnp.pad(proj, ((0, 0), (0, CHUNK - l), (0, 0))), l
        y_ssd, conv_new, ht_new = _ssd(proj_ssd, valid, w["conv_w"], w["conv_b"], w["dt_bias"], w["a_log"],
                                       w["d_ssd"], w["norm_ssd"], conv0, ht0)
        y_ssd = y_ssd[:, :l].reshape(t, SSD_WIDTH)
        s5_in, s5_col = proj, (CONV_DIM + SSD_WIDTH) // S5_WIDTH
    ssd_new = jnp.swapaxes(ht_new, 1, 2).reshape(bsz, SSD_HEADS, SSD_HEADDIM, SSD_STATE)

    y_s5, s5r_new, s5i_new = _s5(s5_in, s5_col, w["bdr"], w["bdi"], w["cdr"], w["cdi"], w["lbr"], w["lbi"],
                                 w["s5_d"], s5r0.reshape(bsz, -1), s5i0.reshape(bsz, -1))

    x2 = _out_proj(x1, y_ssd, y_s5.reshape(t, S5_WIDTH),
                   w["w_glu"], w["b_glu"], w["norm_s5"], w["w_out"])
    y = _ffn(x2, w["norm_ffn2"], w["wg2"], w["wu2"], w["wd2"], final_g=w["norm_final"])
    return (y.reshape(bsz, l, D_MODEL), conv_new[None], ssd_new[None],
            s5r_new.reshape(1, bsz, S5_GROUPS, S5_STATE), s5i_new.reshape(1, bsz, S5_GROUPS, S5_STATE))


def kernel(x_prompt, x_sample, cache_conv, state_ssd, state_s5_re, state_s5_im, norm_ffn1, w_ffn1_gate, w_ffn1_up, w_ffn1_down, norm_mix, w_in, conv_w, conv_b, dt_bias, a_log, d_ssd, norm_ssd, s5_lambda_re, s5_lambda_im, s5_log_step, s5_b_re, s5_b_im, s5_c_re, s5_c_im, s5_d, w_glu, b_glu, norm_s5, w_out, norm_ffn2, w_ffn2_gate, w_ffn2_up, w_ffn2_down, norm_final):
    p = dict(norm_ffn1=norm_ffn1, w_ffn1_gate=w_ffn1_gate, w_ffn1_up=w_ffn1_up, w_ffn1_down=w_ffn1_down,
             norm_mix=norm_mix, w_in=w_in, conv_w=conv_w, conv_b=conv_b, dt_bias=dt_bias, a_log=a_log,
             d_ssd=d_ssd, norm_ssd=norm_ssd, s5_lambda_re=s5_lambda_re, s5_lambda_im=s5_lambda_im,
             s5_log_step=s5_log_step, s5_b_re=s5_b_re, s5_b_im=s5_b_im, s5_c_re=s5_c_re, s5_c_im=s5_c_im,
             s5_d=s5_d, w_glu=w_glu, b_glu=b_glu, norm_s5=norm_s5, w_out=w_out, norm_ffn2=norm_ffn2,
             w_ffn2_gate=w_ffn2_gate, w_ffn2_up=w_ffn2_up, w_ffn2_down=w_ffn2_down, norm_final=norm_final)
    w = _prepare(p)
    bsz = x_prompt.shape[0]
    zeros = lambda *s: jnp.zeros(s, F32)
    out_p = _trunk(x_prompt, zeros(bsz, CONV_W - 1, CONV_DIM),
                   zeros(bsz, SSD_HEADS, SSD_HEADDIM, SSD_STATE),
                   zeros(bsz, S5_GROUPS, S5_STATE), zeros(bsz, S5_GROUPS, S5_STATE), w)
    out_s = _trunk(x_sample, cache_conv[0], state_ssd[0], state_s5_re[0], state_s5_im[0], w)
    return (out_p[0], out_s[0]) + out_p[1:] + out_s[1:]
```

```python
import functools

import jax
import jax.numpy as jnp
from jax import lax
from jax.experimental import pallas as pl
from jax.experimental.pallas import tpu as pltpu

F32 = jnp.float32
BF16 = jnp.bfloat16
EPS = 1e-6

D_MODEL = 2048
D_FF = 5504
SSD_WIDTH = 1024
SSD_HEADDIM = 64
SSD_HEADS = 16
SSD_GROUPS = 4
SSD_STATE = 128
CONV_W = 4
CONV_DIM = SSD_WIDTH + 2 * SSD_GROUPS * SSD_STATE
S5_WIDTH = 1024
S5_GROUP_CH = 16
S5_GROUPS = 64
S5_STATE = 64
CHUNK = 64
SSD_NSUB = 4

LANES = 128
SUBLANES = 8
CONV_TAIL = CONV_W - 1
PAD_HEAD = SUBLANES
PAD_TAIL = slice(PAD_HEAD - CONV_TAIL, PAD_HEAD)
FFN_TM = 1024
FFN_TF = 512
FFN_SPLIT = 2
D_FF_PAD = -(-D_FF // FFN_TF) * FFN_TF
DT_PAD = LANES
PROJ_COLS = CONV_DIM + SSD_WIDTH + S5_WIDTH + DT_PAD
PROJ_TM = 512
MIX_TM = 512
PROJ_PIECE = 256
OUT_TM = 512
S5_NB = 8
S5_LC = 64
S5_SLABS = S5_GROUPS * S5_STATE // LANES
S5_QB = S5_WIDTH // LANES
S5_OUT_LAG = 2
CASTPAD_BLOCK = 256
VMEM_LIMIT = 56 * 1024 * 1024
MIX_VMEM_LIMIT = 60 * 1024 * 1024


def _rms(x, g):
    ms = jnp.mean(x * x, axis=-1, keepdims=True)
    return x * lax.rsqrt(ms + EPS) * g


def _silu(x):
    return x * jax.nn.sigmoid(x)


def _params(*sem, vmem_limit=VMEM_LIMIT):
    return pltpu.CompilerParams(dimension_semantics=sem, vmem_limit_bytes=vmem_limit)


def _ffn_kernel(x_ref, g_ref, wg_ref, wu_ref, wd_ref, gf_ref, o_ref, h_ref, *, final_norm):
    j = pl.program_id(1)

    @pl.when(j == 0)
    def _():
        x = x_ref[...]
        h_ref[...] = _rms(x, g_ref[...]).astype(BF16)
        o_ref[...] = x

    h = h_ref[...]
    tf = wg_ref.shape[1]
    w = tf // FFN_SPLIT
    acts = []
    for k in range(FFN_SPLIT):
        gate = jnp.dot(h, wg_ref[:, k * w:(k + 1) * w], preferred_element_type=F32)
        up = jnp.dot(h, wu_ref[:, k * w:(k + 1) * w], preferred_element_type=F32)
        acts.append((0.5 * (_silu(gate) * up)).astype(BF16))
    down = jnp.dot(acts[0], wd_ref[:w, :], preferred_element_type=F32)
    for k in range(1, FFN_SPLIT):
        down = down + jnp.dot(acts[k], wd_ref[k * w:(k + 1) * w, :], preferred_element_type=F32)
    o_ref[...] += down

    if final_norm:
        @pl.when(j == pl.num_programs(1) - 1)
        def _():
            o_ref[...] = _rms(o_ref[...], gf_ref[...])


def _ffn(x, norm_g, wg, wu, wd, final_g=None):
    t = x.shape[0]
    tm = min(FFN_TM, t)
    tf = FFN_TF
    kern = functools.partial(_ffn_kernel, final_norm=final_g is not None)
    gf = norm_g if final_g is None else final_g
    return pl.pallas_call(
        kern,
        grid=(t // tm, D_FF_PAD // tf),
        in_specs=[
            pl.BlockSpec((tm, D_MODEL), lambda i, j: (i, 0)),
            pl.BlockSpec((1, D_MODEL), lambda i, j: (0, 0)),
            pl.BlockSpec((D_MODEL, tf), lambda i, j: (0, j)),
            pl.BlockSpec((D_MODEL, tf), lambda i, j: (0, j)),
            pl.BlockSpec((tf, D_MODEL), lambda i, j: (j, 0)),
            pl.BlockSpec((1, D_MODEL), lambda i, j: (0, 0)),
        ],
        out_specs=pl.BlockSpec((tm, D_MODEL), lambda i, j: (i, 0)),
        out_shape=jax.ShapeDtypeStruct((t, D_MODEL), F32),
        scratch_shapes=[pltpu.VMEM((tm, D_MODEL), BF16)],
        compiler_params=_params("parallel", "arbitrary"),
        name="ffn",
    )(x, norm_g, wg, wu, wd, gf)


def _inproj_kernel(x_ref, g_ref, w_ref, o_ref):
    h = _rms(x_ref[...], g_ref[...]).astype(BF16)
    o_ref[...] = jnp.dot(h, w_ref[...], preferred_element_type=F32)


def _in_proj(x, norm_g, w):
    t = x.shape[0]
    tm = min(PROJ_TM, t)
    return pl.pallas_call(
        _inproj_kernel,
        grid=(t // tm,),
        in_specs=[
            pl.BlockSpec((tm, D_MODEL), lambda i: (i, 0)),
            pl.BlockSpec((1, D_MODEL), lambda i: (0, 0)),
            pl.BlockSpec((D_MODEL, PROJ_COLS), lambda i: (0, 0), pipeline_mode=pl.Buffered(1)),
        ],
        out_specs=pl.BlockSpec((tm, PROJ_COLS), lambda i: (i, 0)),
        out_shape=jax.ShapeDtypeStruct((t, PROJ_COLS), F32),
        compiler_params=_params("parallel"),
        name="in_proj",
    )(x, norm_g, w)


def _shift_rows(x, k):
    row = lax.broadcasted_iota(jnp.int32, x.shape, 0)
    return jnp.where(row >= k, pltpu.roll(x, k, axis=0), 0.0)


def _cumsum_rows(x):
    k = 1
    while k < x.shape[0]:
        x = x + _shift_rows(x, k)
        k *= 2
    return x


def _chunk_cumsum(da):
    hq = da.shape[0] // 2
    even, odd = _cumsum_rows(da[:hq]), _cumsum_rows(da[hq:])
    return jnp.concatenate([even + _shift_rows(odd, 1), even + odd], axis=0)


def _expand_heads(v):
    q = v.shape[0]
    lane = lax.broadcasted_iota(jnp.int32, (q, LANES), 1)
    blocks = []
    for k in range(SSD_WIDTH // LANES):
        a = jnp.broadcast_to(v[:, 2 * k:2 * k + 1], (q, LANES))
        b = jnp.broadcast_to(v[:, 2 * k + 1:2 * k + 2], (q, LANES))
        blocks.append(jnp.where(lane < SSD_HEADDIM, a, b))
    return jnp.concatenate(blocks, axis=1)


def _slab(s):
    return slice(s * LANES, (s + 1) * LANES)


def _ssd_chunk_pre(dtr_ref, cw_ref, cb_ref, dtb_ref, alog_ref, d_ref, pad_ref, zs_ref, *, base, valid):
    q = CHUNK
    hq = q // 2
    gw = SSD_WIDTH // SSD_GROUPS
    slab = _slab
    row_time = lambda r: ((r & (hq - 1)) << 1) | (r >> (hq.bit_length() - 1))

    def split_rows(ref3, s, start):
        return [ref3[s, pl.ds(start + e, hq, stride=2), :] for e in range(2)]

    cw = cw_ref[...]
    cb = cb_ref[...]
    cols = []
    for s in range(CONV_DIM // LANES):
        taps = [split_rows(pad_ref, s, base + PAD_HEAD - CONV_TAIL + k) for k in range(CONV_W)]
        halves = []
        for e in range(2):
            acc = cb[:, slab(s)] + taps[0][e] * cw[0:1, slab(s)]
            for k in range(1, CONV_W):
                acc = acc + taps[k][e] * cw[k:k + 1, slab(s)]
            halves.append(acc)
        cols.append(jnp.concatenate(halves, axis=0))
    act = _silu(jnp.concatenate(cols, axis=1))
    xs = act[:, :SSD_WIDTH]
    bm = act[:, SSD_WIDTH:SSD_WIDTH + SSD_GROUPS * SSD_STATE].astype(BF16)
    cm = act[:, SSD_WIDTH + SSD_GROUPS * SSD_STATE:].astype(BF16)

    dt_raw = jnp.concatenate([dtr_ref[0, pl.ds(base + e, hq, stride=2), :] for e in range(2)], axis=0)
    dtc = jax.nn.softplus(dt_raw + dtb_ref[...])
    if valid < q:
        dtc = jnp.where(row_time(lax.broadcasted_iota(jnp.int32, dtc.shape, 0)) < valid, dtc, 0.0)
    da = dtc * (-jnp.exp(alog_ref[...]))
    acs_c = _chunk_cumsum(da)
    cbts = []
    for g in range(SSD_GROUPS):
        bg = bm[:, g * SSD_STATE:(g + 1) * SSD_STATE]
        cg = cm[:, g * SSD_STATE:(g + 1) * SSD_STATE]
        b4 = jnp.concatenate([bg] * (SSD_HEADS // SSD_GROUPS), axis=0)
        cbts.append(lax.dot_general(cg, b4, (((1,), (1,)), ((), ())), preferred_element_type=F32))
    zcols = [jnp.concatenate(split_rows(zs_ref, s, base), axis=0) for s in range(SSD_WIDTH // LANES)]
    gate = _silu(jnp.concatenate(zcols, axis=1))
    return dict(xs=xs, bm=bm, cm=cm, dtc=dtc, acs_c=acs_c, cbts=cbts, gate=gate, skip=d_ref[...] * xs)


def _ssd_chunk_post(pre):
    q = CHUNK
    hq = q // 2
    gw = SSD_WIDTH // SSD_GROUPS
    row_time = lambda r: ((r & (hq - 1)) << 1) | (r >> (hq.bit_length() - 1))
    xs, bm, cm = pre["xs"], pre["bm"], pre["cm"]
    dt_e = _expand_heads(pre["dtc"])
    acs_e = _expand_heads(pre["acs_c"])
    row = lax.broadcasted_iota(jnp.int32, (q, SSD_WIDTH), 0)
    jj = lax.broadcasted_iota(jnp.int32, (q, SSD_WIDTH), 1) & (SSD_HEADDIM - 1)
    acs_row = jnp.sum(jnp.where(jj == row, acs_e, 0.0), axis=0, keepdims=True)
    causal = row_time(jj) <= row_time(row)
    lmat = jnp.exp(jnp.where(causal, acs_e - acs_row, -jnp.inf))
    last = acs_e[q - 1:q, :]
    eacs = jnp.exp(acs_e)
    dec_end = jnp.exp(last - acs_e)
    cdec = jnp.exp(last)
    xdt = xs * dt_e
    xdt_b = xdt.astype(BF16)
    xd_b = (xdt * dec_end).astype(BF16)
    head_of_lane = lax.broadcasted_iota(jnp.int32, (q, gw), 1) // SSD_HEADDIM

    ydiags, cgs, sts = [], [], []
    for g in range(SSD_GROUPS):
        sl = slice(g * gw, (g + 1) * gw)
        bg = bm[:, g * SSD_STATE:(g + 1) * SSD_STATE]
        cg = cm[:, g * SSD_STATE:(g + 1) * SSD_STATE]
        sg = (pre["cbts"][g] * lmat[:, sl]).astype(BF16)
        xg = xdt_b[:, sl]
        xbd = jnp.concatenate(
            [jnp.where(head_of_lane == hh, xg, jnp.zeros_like(xg)) for hh in range(SSD_HEADS // SSD_GROUPS)],
            axis=0)
        ydiags.append(jnp.dot(sg, xbd, preferred_element_type=F32))
        cgs.append(cg)
        sts.append(lax.dot_general(bg, xd_b[:, sl], (((0,), (0,)), ((), ())), preferred_element_type=F32))
    return dict(ydiags=ydiags, cgs=cgs, sts=sts, eacs=eacs, cdec=cdec, skip=pre["skip"], gate=pre["gate"])


def _ssd_chunk_state(loc, ht_ref):
    gw = SSD_WIDTH // SSD_GROUPS
    ys = []
    for g in range(SSD_GROUPS):
        sl = slice(g * gw, (g + 1) * gw)
        htg = ht_ref[0, :, sl]
        yoff = jnp.dot(loc["cgs"][g], htg.astype(BF16), preferred_element_type=F32)
        ys.append(loc["ydiags"][g] + loc["eacs"][:, sl] * yoff)
        ht_ref[0, :, sl] = loc["cdec"][:, sl] * htg + loc["sts"][g]
    return (jnp.concatenate(ys, axis=1) + loc["skip"]) * loc["gate"]


def _ssd_chunk_out(y, nrm_ref, y_ref, *, base):
    q = CHUNK
    hq = q // 2
    row_time = lambda r: ((r & (hq - 1)) << 1) | (r >> (hq.bit_length() - 1))
    yn = _rms(y, nrm_ref[...]).astype(BF16)
    unperm = (row_time(lax.broadcasted_iota(jnp.int32, (q, q), 1))
              == lax.broadcasted_iota(jnp.int32, (q, q), 0)).astype(BF16)
    y_ref[0, base:base + q, :] = jnp.dot(unperm, yn, preferred_element_type=F32).astype(BF16)


def _ssd_kernel(xbc_ref, z_ref, dtr_ref, cw_ref, cb_ref, dtb_ref, alog_ref, d_ref, nrm_ref,
                conv0_ref, h0_ref, y_ref, convo_ref, ht_ref, pad_ref, zs_ref, *, valid, nsub):
    c = pl.program_id(1)
    rows = nsub * CHUNK
    last = rows - CHUNK + valid

    @pl.when(c == 0)
    def _():
        for s in range(CONV_DIM // LANES):
            pad_ref[s, PAD_TAIL, :] = conv0_ref[0, :, _slab(s)]
        ht_ref[0] = h0_ref[0]

    for s in range(CONV_DIM // LANES):
        pad_ref[s, PAD_HEAD:PAD_HEAD + rows, :] = xbc_ref[0, :, _slab(s)]
    for s in range(SSD_WIDTH // LANES):
        zs_ref[s] = z_ref[0, :, _slab(s)]
    pres = [_ssd_chunk_pre(dtr_ref, cw_ref, cb_ref, dtb_ref, alog_ref, d_ref, pad_ref, zs_ref,
                           base=i * CHUNK, valid=valid if i == nsub - 1 else CHUNK) for i in range(nsub)]
    locs = [_ssd_chunk_post(pre) for pre in pres]
    ys = [_ssd_chunk_state(loc, ht_ref) for loc in locs]
    for i in range(nsub):
        _ssd_chunk_out(ys[i], nrm_ref, y_ref, base=i * CHUNK)
    for s in range(CONV_DIM // LANES):
        pad_ref[s, PAD_TAIL, :] = pad_ref[s, PAD_HEAD + last - CONV_TAIL:PAD_HEAD + last, :]

    @pl.when(c == pl.num_programs(1) - 1)
    def _():
        for s in range(CONV_DIM // LANES):
            convo_ref[0, :, _slab(s)] = pad_ref[s, PAD_TAIL, :]


def _ssd(proj3, valid, cw, cb, dtb, alog, d_e, nrm, conv0, ht0):
    bsz, lp, _ = proj3.shape
    nsub = min(SSD_NSUB, lp // CHUNK)
    q = nsub * CHUNK
    kern = functools.partial(_ssd_kernel, valid=valid, nsub=nsub)
    row = lambda n: pl.BlockSpec((1, n), lambda b, c: (0, 0))
    return pl.pallas_call(
        kern,
        grid=(bsz, lp // q),
        in_specs=[
            pl.BlockSpec((1, q, CONV_DIM), lambda b, c: (b, c, 0)),
            pl.BlockSpec((1, q, SSD_WIDTH), lambda b, c: (b, c, CONV_DIM // SSD_WIDTH)),
            pl.BlockSpec((1, q, DT_PAD), lambda b, c: (b, c, (PROJ_COLS - DT_PAD) // DT_PAD)),
            pl.BlockSpec((CONV_W, CONV_DIM), lambda b, c: (0, 0)),
            row(CONV_DIM), row(DT_PAD), row(DT_PAD), row(SSD_WIDTH), row(SSD_WIDTH),
            pl.BlockSpec((1, CONV_W - 1, CONV_DIM), lambda b, c: (b, 0, 0)),
            pl.BlockSpec((1, SSD_STATE, SSD_WIDTH), lambda b, c: (b, 0, 0)),
        ],
        out_specs=[
            pl.BlockSpec((1, q, SSD_WIDTH), lambda b, c: (b, c, 0)),
            pl.BlockSpec((1, CONV_W - 1, CONV_DIM), lambda b, c: (b, 0, 0)),
            pl.BlockSpec((1, SSD_STATE, SSD_WIDTH), lambda b, c: (b, 0, 0)),
        ],
        out_shape=[
            jax.ShapeDtypeStruct((bsz, lp, SSD_WIDTH), BF16),
            jax.ShapeDtypeStruct((bsz, CONV_W - 1, CONV_DIM), F32),
            jax.ShapeDtypeStruct((bsz, SSD_STATE, SSD_WIDTH), F32),
        ],
        scratch_shapes=[pltpu.VMEM((CONV_DIM // LANES, PAD_HEAD + q, LANES), F32),
                        pltpu.VMEM((SSD_WIDTH // LANES, q, LANES), F32)],
        compiler_params=_params("parallel", "arbitrary"),
        name="ssd",
    )(proj3, proj3, proj3, cw, cb, dtb, alog, d_e, nrm, conv0, ht0)


def _mix_in_kernel(x_ref, g_ref, w_ref, cw_ref, cb_ref, dtb_ref, alog_ref, d_ref, nrm_ref, conv0_ref, h0_ref,
                   u5_ref, y_ref, convo_ref, ht_ref, pad_ref, zs_ref, dts_ref, *, tiles_per_seq):
    i = pl.program_id(0)
    nsub = MIX_TM // CHUNK
    in_seq = lax.rem(jnp.maximum(i - 1, 0), tiles_per_seq)

    @pl.when(i == 0)
    def _():
        pad_ref[...] = jnp.zeros_like(pad_ref)
        zs_ref[...] = jnp.zeros_like(zs_ref)
        dts_ref[...] = jnp.zeros_like(dts_ref)
        ht_ref[...] = jnp.zeros_like(ht_ref)

    @pl.when(jnp.logical_and(i >= 1, in_seq == 0))
    def _():
        for s in range(CONV_DIM // LANES):
            pad_ref[s, PAD_TAIL, :] = conv0_ref[0, :, _slab(s)]
        ht_ref[0] = h0_ref[0]

    h = _rms(x_ref[...], g_ref[...]).astype(BF16)
    bounds = list(range(0, PROJ_COLS, PROJ_PIECE)) + [PROJ_COLS]
    npieces = len(bounds) - 1
    per_chunk = npieces // nsub
    pieces = []

    def project(n):
        pieces.append(jnp.dot(h, w_ref[:, bounds[n]:bounds[n + 1]], preferred_element_type=F32))

    pres, locs = [], []
    for c in range(nsub + 1):
        if c < nsub:
            pres.append(_ssd_chunk_pre(dts_ref, cw_ref, cb_ref, dtb_ref, alog_ref, d_ref, pad_ref, zs_ref,
                                       base=c * CHUNK, valid=CHUNK))
        if c >= 1:
            locs.append(_ssd_chunk_post(pres[c - 1]))
        for n in range(c * per_chunk, min((c + 1) * per_chunk, nsub * per_chunk)):
            project(n)
    ys = [_ssd_chunk_state(loc, ht_ref) for loc in locs]
    for n in range(nsub * per_chunk, npieces):
        project(n)
    for c in range(nsub):
        _ssd_chunk_out(ys[c], nrm_ref, y_ref, base=c * CHUNK)

    for s in range(CONV_DIM // LANES):
        pad_ref[s, PAD_TAIL, :] = pad_ref[s, PAD_HEAD + MIX_TM - CONV_TAIL:PAD_HEAD + MIX_TM, :]

    def column_tile(c0):
        return pieces[c0 // PROJ_PIECE][:, c0 % PROJ_PIECE:c0 % PROJ_PIECE + LANES]

    for s in range(CONV_DIM // LANES):
        pad_ref[s, PAD_HEAD:PAD_HEAD + MIX_TM, :] = column_tile(s * LANES)
    for s in range(SSD_WIDTH // LANES):
        zs_ref[s] = column_tile(CONV_DIM + s * LANES)
        u5_ref[:, _slab(s)] = column_tile(CONV_DIM + SSD_WIDTH + s * LANES)
    dts_ref[0] = column_tile(PROJ_COLS - DT_PAD)

    @pl.when(jnp.logical_and(i >= 1, in_seq == tiles_per_seq - 1))
    def _():
        for s in range(CONV_DIM // LANES):
            convo_ref[0, :, _slab(s)] = pad_ref[s, PAD_TAIL, :]


def _mix_in(x1, l, w, conv0, ht0):
    t = x1.shape[0]
    bsz, nt, tps = t // l, t // MIX_TM, l // MIX_TM
    seq = lambda i: jnp.maximum(i - 1, 0) // tps
    row = lambda n: pl.BlockSpec((1, n), lambda i: (0, 0))
    u5, y, conv_new, ht_new = pl.pallas_call(
        functools.partial(_mix_in_kernel, tiles_per_seq=tps),
        grid=(nt + 1,),
        in_specs=[
            pl.BlockSpec((MIX_TM, D_MODEL), lambda i: (jnp.minimum(i, nt - 1), 0)),
            row(D_MODEL),
            pl.BlockSpec((D_MODEL, PROJ_COLS), lambda i: (0, 0), pipeline_mode=pl.Buffered(1)),
            pl.BlockSpec((CONV_W, CONV_DIM), lambda i: (0, 0)),
            row(CONV_DIM), row(DT_PAD), row(DT_PAD), row(SSD_WIDTH), row(SSD_WIDTH),
            pl.BlockSpec((1, CONV_W - 1, CONV_DIM), lambda i: (seq(i), 0, 0)),
            pl.BlockSpec((1, SSD_STATE, SSD_WIDTH), lambda i: (seq(i), 0, 0)),
        ],
        out_specs=[
            pl.BlockSpec((MIX_TM, S5_WIDTH), lambda i: (jnp.minimum(i, nt - 1), 0)),
            pl.BlockSpec((1, MIX_TM, SSD_WIDTH), lambda i: (jnp.maximum(i - 1, 0), 0, 0)),
            pl.BlockSpec((1, CONV_W - 1, CONV_DIM), lambda i: (seq(i), 0, 0)),
            pl.BlockSpec((1, SSD_STATE, SSD_WIDTH), lambda i: (seq(i), 0, 0)),
        ],
        out_shape=[
            jax.ShapeDtypeStruct((t, S5_WIDTH), F32),
            jax.ShapeDtypeStruct((nt, MIX_TM, SSD_WIDTH), BF16),
            jax.ShapeDtypeStruct((bsz, CONV_W - 1, CONV_DIM), F32),
            jax.ShapeDtypeStruct((bsz, SSD_STATE, SSD_WIDTH), F32),
        ],
        scratch_shapes=[pltpu.VMEM((CONV_DIM // LANES, PAD_HEAD + MIX_TM, LANES), F32),
                        pltpu.VMEM((SSD_WIDTH // LANES, MIX_TM, LANES), F32),
                        pltpu.VMEM((1, MIX_TM, DT_PAD), F32)],
        compiler_params=_params("arbitrary", vmem_limit=MIX_VMEM_LIMIT),
        name="mix_in",
    )(x1, w["norm_mix"], w["w_in"], w["conv_w"], w["conv_b"], w["dt_bias"], w["a_log"], w["d_ssd"],
      w["norm_ssd"], conv0, ht0)
    return u5, y.reshape(t, SSD_WIDTH), conv_new, ht_new


def _s5_prep_kernel(lre_ref, lim_ref, lstep_ref, bre_ref, bim_ref, lbr_ref, lbi_ref, bbr_ref, bbi_ref):
    lam_re = lre_ref[...]
    lam_im = lim_ref[...]
    step = jnp.exp(lstep_ref[...])
    mag = jnp.exp(lam_re * step)
    ang = lam_im * step
    lb_re = mag * jnp.cos(ang)
    lb_im = mag * jnp.sin(ang)
    den = lam_re * lam_re + lam_im * lam_im
    nr = lb_re - 1.0
    q_re = (nr * lam_re + lb_im * lam_im) / den
    q_im = (lb_im * lam_re - nr * lam_im) / den
    lbr_ref[...] = lb_re
    lbi_ref[...] = lb_im
    b_re = bre_ref[...]
    b_im = bim_ref[...]
    bbr_ref[...] = q_re[None] * b_re - q_im[None] * b_im
    bbi_ref[...] = q_re[None] * b_im + q_im[None] * b_re


def _s5_prep(lam_re, lam_im, log_step_e, b_re_t, b_im_t):
    gn = jax.ShapeDtypeStruct((S5_GROUPS, S5_STATE), F32)
    hgn = jax.ShapeDtypeStruct((S5_GROUP_CH, S5_GROUPS, S5_STATE), F32)
    return pl.pallas_call(_s5_prep_kernel, out_shape=[gn, gn, hgn, hgn], name="s5_prep")(
        lam_re, lam_im, log_step_e, b_re_t, b_im_t)


def _s5_kernel(u_ref, bdr_ref, bdi_ref, cdr_ref, cdi_ref, lbr_ref, lbi_ref, d_ref, h0r_ref, h0i_ref,
               y_ref, hor_ref, hoi_ref, up_ref, yp_ref, *slab_refs, lc, pitch):
    c = pl.program_id(1)

    @pl.when(c == 0)
    def _():
        hor_ref[...] = h0r_ref[...]
        hoi_ref[...] = h0i_ref[...]
        up_ref[...] = jnp.zeros_like(up_ref)

    for b in range(S5_NB):
        up_ref[b * pitch:b * pitch + lc, :] = u_ref[b]
    u = up_ref[...]
    ub = u.astype(BF16)

    spb = S5_SLABS // S5_QB
    hr_refs, hi_refs = slab_refs[:S5_QB], slab_refs[S5_QB:]

    def input_stage(qb):
        uq = ub[:, qb * LANES:(qb + 1) * LANES]
        bu_r = jnp.dot(uq, bdr_ref[qb], preferred_element_type=F32)
        bu_i = jnp.dot(uq, bdi_ref[qb], preferred_element_type=F32)
        for k in range(spb):
            hr_refs[qb][k] = bu_r[:, k * LANES:(k + 1) * LANES]
            hi_refs[qb][k] = bu_i[:, k * LANES:(k + 1) * LANES]

    def scan_stage(qb):
        hr_ref, hi_ref = hr_refs[qb], hi_refs[qb]
        for k in range(spb):
            sl = slice((qb * spb + k) * LANES, (qb * spb + k + 1) * LANES)
            lr = jnp.broadcast_to(lbr_ref[:, sl], (S5_NB, LANES))
            li = jnp.broadcast_to(lbi_ref[:, sl], (S5_NB, LANES))
            h_r, h_i = hor_ref[:, sl], hoi_ref[:, sl]
            for t in range(lc):
                rows = pl.ds(t, S5_NB, stride=pitch)
                n_r = lr * h_r - li * h_i + hr_ref[k, rows, :]
                n_i = lr * h_i + li * h_r + hi_ref[k, rows, :]
                hr_ref[k, rows, :] = n_r
                hi_ref[k, rows, :] = n_i
                h_r, h_i = n_r, n_i
            hor_ref[:, sl] = h_r
            hoi_ref[:, sl] = h_i

    def output_stage(qb):
        ln = slice(qb * LANES, (qb + 1) * LANES)
        h_re = jnp.concatenate([hr_refs[qb][k] for k in range(spb)], axis=1).astype(BF16)
        h_im = jnp.concatenate([hi_refs[qb][k] for k in range(spb)], axis=1).astype(BF16)
        yq = (jnp.dot(h_re, cdr_ref[qb], preferred_element_type=F32)
              - jnp.dot(h_im, cdi_ref[qb], preferred_element_type=F32))
        yp_ref[:, ln] = yq + d_ref[:, ln] * u[:, ln]

    for step in range(S5_QB + S5_OUT_LAG):
        if 0 <= step - 1 < S5_QB:
            scan_stage(step - 1)
        if 0 <= step - S5_OUT_LAG < S5_QB:
            output_stage(step - S5_OUT_LAG)
        if step < S5_QB:
            input_stage(step)

    for b in range(S5_NB):
        y_ref[b] = yp_ref[b * pitch:b * pitch + lc, :]


def _s5(proj3, col, bdr, bdi, cdr, cdi, lbr, lbi, d5, h0r, h0i):
    bsz, l, _ = proj3.shape
    lc = min(S5_LC, l)
    pitch = lc + 4
    mp = S5_NB * pitch
    kern = functools.partial(_s5_kernel, lc=lc, pitch=pitch)
    ns = S5_GROUPS * S5_STATE
    full = lambda *shape: pl.BlockSpec(shape, lambda b, c: (0,) * len(shape))
    state = pl.BlockSpec((S5_NB, ns), lambda b, c: (b, 0))
    return pl.pallas_call(
        kern,
        grid=(bsz // S5_NB, l // lc),
        in_specs=[
            pl.BlockSpec((S5_NB, lc, S5_WIDTH), lambda b, c: (b, c, col)),
            full(S5_QB, LANES, 4 * LANES), full(S5_QB, LANES, 4 * LANES),
            full(S5_QB, 4 * LANES, LANES), full(S5_QB, 4 * LANES, LANES),
            full(1, ns), full(1, ns), full(1, S5_WIDTH),
            state, state,
        ],
        out_specs=[pl.BlockSpec((S5_NB, lc, S5_WIDTH), lambda b, c: (b, c, 0)), state, state],
        out_shape=[
            jax.ShapeDtypeStruct((bsz, l, S5_WIDTH), F32),
            jax.ShapeDtypeStruct((bsz, ns), F32),
            jax.ShapeDtypeStruct((bsz, ns), F32),
        ],
        scratch_shapes=[
            pltpu.VMEM((mp, S5_WIDTH), F32),
            pltpu.VMEM((mp, S5_WIDTH), F32),
        ] + [pltpu.VMEM((S5_SLABS // S5_QB, mp, LANES), F32)] * (2 * S5_QB),
        compiler_params=_params("parallel", "arbitrary"),
        name="s5",
    )(proj3, bdr, bdi, cdr, cdi, lbr, lbi, d5, h0r, h0i)


def _outproj_kernel(x_ref, ya_ref, y5_ref, wglu_ref, bglu_ref, nrm_ref, w_ref, o_ref):
    y5 = y5_ref[...]
    g = 0.5 * y5 * (1.0 + lax.erf(y5 * (0.5 ** 0.5)))
    gl = jnp.dot(g.astype(BF16), wglu_ref[...], preferred_element_type=F32) + bglu_ref[...]
    yb = _rms(g * jax.nn.sigmoid(gl), nrm_ref[...]).astype(BF16)
    o_ref[...] = (x_ref[...]
                  + jnp.dot(ya_ref[...], w_ref[:SSD_WIDTH, :], preferred_element_type=F32)
                  + jnp.dot(yb, w_ref[SSD_WIDTH:, :], preferred_element_type=F32))


def _out_proj(x, ya, y5, wglu, bglu, nrm, w):
    t = x.shape[0]
    tm = min(OUT_TM, t)
    const = lambda *shape: pl.BlockSpec(shape, lambda i: (0,) * len(shape), pipeline_mode=pl.Buffered(1))
    return pl.pallas_call(
        _outproj_kernel,
        grid=(t // tm,),
        in_specs=[
            pl.BlockSpec((tm, D_MODEL), lambda i: (i, 0)),
            pl.BlockSpec((tm, SSD_WIDTH), lambda i: (i, 0)),
            pl.BlockSpec((tm, S5_WIDTH), lambda i: (i, 0)),
            const(S5_WIDTH, S5_WIDTH), const(1, S5_WIDTH), const(1, S5_WIDTH),
            const(D_MODEL, D_MODEL),
        ],
        out_specs=pl.BlockSpec((tm, D_MODEL), lambda i: (i, 0)),
        out_shape=jax.ShapeDtypeStruct((t, D_MODEL), F32),
        compiler_params=_params("parallel"),
        name="out_proj",
    )(x, ya, y5, wglu, bglu, nrm, w)


def _castpad_kernel(x_ref, o_ref, *, axis):
    if axis == 1:
        o_ref[:, :D_FF] = x_ref[...].astype(BF16)
        o_ref[:, D_FF:] = jnp.zeros((o_ref.shape[0], D_FF_PAD - D_FF), BF16)
    else:
        o_ref[:D_FF, :] = x_ref[...].astype(BF16)
        o_ref[D_FF:, :] = jnp.zeros((D_FF_PAD - D_FF, o_ref.shape[1]), BF16)


def _cast_pad(w, axis):
    other = w.shape[1 - axis]
    if axis == 1:
        in_block, out_block, out_shape = (CASTPAD_BLOCK, D_FF), (CASTPAD_BLOCK, D_FF_PAD), (other, D_FF_PAD)
        index = lambda j: (j, 0)
    else:
        in_block, out_block, out_shape = (D_FF, CASTPAD_BLOCK), (D_FF_PAD, CASTPAD_BLOCK), (D_FF_PAD, other)
        index = lambda j: (0, j)
    return pl.pallas_call(
        functools.partial(_castpad_kernel, axis=axis),
        grid=(other // CASTPAD_BLOCK,),
        in_specs=[pl.BlockSpec(in_block, index)],
        out_specs=pl.BlockSpec(out_block, index),
        out_shape=jax.ShapeDtypeStruct(out_shape, BF16),
        compiler_params=_params("parallel"),
        name="cast_pad",
    )(w)


def _winprep_kernel(x_ref, o_ref):
    c0, c1, c2 = SSD_WIDTH, SSD_WIDTH + CONV_DIM, SSD_WIDTH + CONV_DIM + SSD_HEADS
    rows = o_ref.shape[0]
    o_ref[:, :CONV_DIM] = x_ref[:, c0:c1].astype(BF16)
    o_ref[:, CONV_DIM:CONV_DIM + SSD_WIDTH] = x_ref[:, :c0].astype(BF16)
    o_ref[:, CONV_DIM + SSD_WIDTH:PROJ_COLS - DT_PAD] = x_ref[:, c2:].astype(BF16)
    dt = jnp.concatenate([x_ref[:, c1:c2], jnp.zeros((rows, DT_PAD - SSD_HEADS), F32)], axis=1)
    o_ref[:, PROJ_COLS - DT_PAD:] = dt.astype(BF16)


def _prep_w_in(w_in):
    k, n = w_in.shape
    return pl.pallas_call(
        _winprep_kernel,
        grid=(k // CASTPAD_BLOCK,),
        in_specs=[pl.BlockSpec((CASTPAD_BLOCK, n), lambda j: (j, 0))],
        out_specs=pl.BlockSpec((CASTPAD_BLOCK, PROJ_COLS), lambda j: (j, 0)),
        out_shape=jax.ShapeDtypeStruct((k, PROJ_COLS), BF16),
        compiler_params=_params("parallel"),
        name="prep_w_in",
    )(w_in)


def _prepare(p):
    w = {}
    for n in ("1", "2"):
        w["norm_ffn" + n] = p["norm_ffn" + n]
        w["wg" + n] = _cast_pad(p["w_ffn%s_gate" % n][0], 1)
        w["wu" + n] = _cast_pad(p["w_ffn%s_up" % n][0], 1)
        w["wd" + n] = _cast_pad(p["w_ffn%s_down" % n][0], 0)
    w["w_in"] = _prep_w_in(p["w_in"][0])
    w["norm_mix"] = p["norm_mix"]
    w["conv_w"] = p["conv_w"][0]
    w["conv_b"] = p["conv_b"]
    w["dt_bias"] = jnp.pad(p["dt_bias"], ((0, 0), (0, DT_PAD - SSD_HEADS)))
    w["a_log"] = jnp.pad(p["a_log"], ((0, 0), (0, DT_PAD - SSD_HEADS)))
    w["d_ssd"] = jnp.repeat(p["d_ssd"][0], SSD_HEADDIM)[None]
    w["norm_ssd"] = p["norm_ssd"]

    lbr, lbi, bbr, bbi = _s5_prep(
        p["s5_lambda_re"][0], p["s5_lambda_im"][0],
        jnp.broadcast_to(p["s5_log_step"][0][:, None], (S5_GROUPS, S5_STATE)),
        jnp.transpose(p["s5_b_re"][0], (2, 0, 1)), jnp.transpose(p["s5_b_im"][0], (2, 0, 1)))
    eye = jnp.eye(S5_QB, dtype=F32)
    gl = S5_GROUPS // S5_QB

    def bd(bb):
        t = jnp.einsum("hqgn,gk->qghkn", bb.reshape(S5_GROUP_CH, S5_QB, gl, S5_STATE), eye)
        return t.reshape(S5_QB, LANES, gl * S5_STATE).astype(BF16)

    def cd(cc):
        t = jnp.einsum("qghn,gk->qgnkh", cc.reshape(S5_QB, gl, S5_GROUP_CH, S5_STATE), eye)
        return t.reshape(S5_QB, gl * S5_STATE, LANES).astype(BF16)

    w["bdr"], w["bdi"] = bd(bbr), bd(bbi)
    w["cdr"], w["cdi"] = cd(p["s5_c_re"][0]), cd(p["s5_c_im"][0])
    w["lbr"] = lbr.reshape(1, -1)
    w["lbi"] = lbi.reshape(1, -1)
    w["s5_d"] = p["s5_d"][0].reshape(1, -1)
    w["w_glu"] = p["w_glu"][0].astype(BF16)
    w["b_glu"] = p["b_glu"]
    w["norm_s5"] = p["norm_s5"]
    w["w_out"] = p["w_out"][0].astype(BF16)
    w["norm_final"] = p["norm_final"][None]
    return w


def _trunk(x, conv0, ssd0, s5r0, s5i0, w):
    bsz, l, _ = x.shape
    t = bsz * l
    x1 = _ffn(x.reshape(t, D_MODEL), w["norm_ffn1"], w["wg1"], w["wu1"], w["wd1"])
    ht0 = jnp.swapaxes(ssd0.reshape(bsz, SSD_WIDTH, SSD_STATE), 1, 2)
    if l % MIX_TM == 0:
        u5, y_ssd, conv_new, ht_new = _mix_in(x1, l, w, conv0, ht0)
        s5_in, s5_col = u5.reshape(bsz, l, S5_WIDTH), 0
    else:
        proj = _in_proj(x1, w["norm_mix"], w["w_in"]).reshape(bsz, l, PROJ_COLS)
        if l % CHUNK == 0:
            proj_ssd, valid = proj, CHUNK
        else:
            assert l < CHUNK
            proj_ssd, valid = jnp.pad(proj, ((0, 0), (0, CHUNK - l), (0, 0))), l
        y_ssd, conv_new, ht_new = _ssd(proj_ssd, valid, w["conv_w"], w["conv_b"], w["dt_bias"], w["a_log"],
                                       w["d_ssd"], w["norm_ssd"], conv0, ht0)
        y_ssd = y_ssd[:, :l].reshape(t, SSD_WIDTH)
        s5_in, s5_col = proj, (CONV_DIM + SSD_WIDTH) // S5_WIDTH
    ssd_new = jnp.swapaxes(ht_new, 1, 2).reshape(bsz, SSD_HEADS, SSD_HEADDIM, SSD_STATE)

    y_s5, s5r_new, s5i_new = _s5(s5_in, s5_col, w["bdr"], w["bdi"], w["cdr"], w["cdi"], w["lbr"], w["lbi"],
                                 w["s5_d"], s5r0.reshape(bsz, -1), s5i0.reshape(bsz, -1))

    x2 = _out_proj(x1, y_ssd, y_s5.reshape(t, S5_WIDTH),
                   w["w_glu"], w["b_glu"], w["norm_s5"], w["w_out"])
    y = _ffn(x2, w["norm_ffn2"], w["wg2"], w["wu2"], w["wd2"], final_g=w["norm_final"])
    return (y.reshape(bsz, l, D_MODEL), conv_new[None], ssd_new[None],
            s5r_new.reshape(1, bsz, S5_GROUPS, S5_STATE), s5i_new.reshape(1, bsz, S5_GROUPS, S5_STATE))


def kernel(x_prompt, x_sample, cache_conv, state_ssd, state_s5_re, state_s5_im, norm_ffn1, w_ffn1_gate, w_ffn1_up, w_ffn1_down, norm_mix, w_in, conv_w, conv_b, dt_bias, a_log, d_ssd, norm_ssd, s5_lambda_re, s5_lambda_im, s5_log_step, s5_b_re, s5_b_im, s5_c_re, s5_c_im, s5_d, w_glu, b_glu, norm_s5, w_out, norm_ffn2, w_ffn2_gate, w_ffn2_up, w_ffn2_down, norm_final):
    p = dict(norm_ffn1=norm_ffn1, w_ffn1_gate=w_ffn1_gate, w_ffn1_up=w_ffn1_up, w_ffn1_down=w_ffn1_down,
             norm_mix=norm_mix, w_in=w_in, conv_w=conv_w, conv_b=conv_b, dt_bias=dt_bias, a_log=a_log,
             d_ssd=d_ssd, norm_ssd=norm_ssd, s5_lambda_re=s5_lambda_re, s5_lambda_im=s5_lambda_im,
             s5_log_step=s5_log_step, s5_b_re=s5_b_re, s5_b_im=s5_b_im, s5_c_re=s5_c_re, s5_c_im=s5_c_im,
             s5_d=s5_d, w_glu=w_glu, b_glu=b_glu, norm_s5=norm_s5, w_out=w_out, norm_ffn2=norm_ffn2,
             w_ffn2_gate=w_ffn2_gate, w_ffn2_up=w_ffn2_up, w_ffn2_down=w_ffn2_down, norm_final=norm_final)
    w = _prepare(p)
    bsz = x_prompt.shape[0]
    zeros = lambda *s: jnp.zeros(s, F32)
    out_p = _trunk(x_prompt, zeros(bsz, CONV_W - 1, CONV_DIM),
                   zeros(bsz, SSD_HEADS, SSD_HEADDIM, SSD_STATE),
                   zeros(bsz, S5_GROUPS, S5_STATE), zeros(bsz, S5_GROUPS, S5_STATE), w)
    out_s = _trunk(x_sample, cache_conv[0], state_ssd[0], state_s5_re[0], state_s5_im[0], w)
    return (out_p[0], out_s[0]) + out_p[1:] + out_s[1:]
```
